```python
import jax, jax.numpy as jnp
from jax import lax
import numpy as np

D_MODEL = 2048
BATCH = 8
SEQ = 2048
DEPTH = 2

D_MIX = D_MODEL
RWKV_HEAD_DIM = 64
RWKV_WIDTH = D_MODEL // 2
RWKV_HEADS = RWKV_WIDTH // RWKV_HEAD_DIM
RWKV_DECAY_RANK = 64
RWKV_ICLR_RANK = 64
RWKV_GN_EPS = 64e-5
MOBA_HEAD_DIM = 128
MOBA_WIDTH = D_MODEL // 4
MOBA_HEADS = MOBA_WIDTH // MOBA_HEAD_DIM
MOBA_BLOCK = 256
MOBA_TOPK = 3
MOBA_Q_CHUNK = 16
GLA_WIDTH = D_MODEL // 4
GLA_KEY_WIDTH = GLA_WIDTH // 2
GLA_HEADS = 4
GLA_KEY_DIM = GLA_KEY_WIDTH // GLA_HEADS
GLA_VALUE_DIM = GLA_WIDTH // GLA_HEADS
GLA_GATE_RANK = 16
GLA_GATE_TEMP = 16.0
GLA_CHUNK = 64
NORM_EPS = 1e-6

RWKV_SHIFT_SIZES = (RWKV_WIDTH, RWKV_WIDTH, RWKV_WIDTH, RWKV_DECAY_RANK, RWKV_ICLR_RANK)
RWKV_SHIFT_COLS = 3 * RWKV_WIDTH + RWKV_DECAY_RANK + RWKV_ICLR_RANK
IN_SIZES = (RWKV_SHIFT_COLS, RWKV_WIDTH,
            MOBA_WIDTH, MOBA_WIDTH, MOBA_WIDTH, MOBA_WIDTH,
            GLA_KEY_WIDTH, GLA_KEY_WIDTH, GLA_WIDTH, GLA_WIDTH, GLA_GATE_RANK)
D_IN = RWKV_SHIFT_COLS + RWKV_WIDTH + 4 * MOBA_WIDTH + 2 * GLA_KEY_WIDTH + 2 * GLA_WIDTH + GLA_GATE_RANK

kernel_name = 'hybrid_rwkv7_moba_gla_parallel_heads'


def _split(t, sizes):
    offsets = np.cumsum(np.array(sizes))[:-1].tolist()
    return jnp.split(t, offsets, axis=-1)


def _rmsnorm(x, w):
    xf = x.astype(jnp.float32)
    y = xf * lax.rsqrt(jnp.mean(xf * xf, axis=-1, keepdims=True) + NORM_EPS)
    return (y * w.astype(jnp.float32)).astype(x.dtype)


def _rwkv7_mix(p_shift, g, mu, w0, w2, a0, a2, k_k, k_a, r_k, ln_w, ln_b):
    B, S, _ = p_shift.shape
    H, N = RWKV_HEADS, RWKV_HEAD_DIM
    p = p_shift.astype(jnp.float32)
    prev = jnp.pad(p, ((0, 0), (1, 0), (0, 0)))[:, :-1]
    p = p + mu * (prev - p)
    r, k, v, wd, ad = _split(p, RWKV_SHIFT_SIZES)
    w = w0 + jnp.tanh(wd) @ w2
    w = -jax.nn.softplus(-w) - 0.5
    decay = jnp.exp(-jnp.exp(w))
    a = jax.nn.sigmoid(a0 + ad @ a2)
    r, k, v, decay, a = (t.reshape(B, S, H, N) for t in (r, k, v, decay, a))
    kk = k * k_k.reshape(H, N)
    kk = kk * lax.rsqrt(jnp.maximum(jnp.sum(kk * kk, axis=-1, keepdims=True), 1e-24))
    k = k * (1.0 + (a - 1.0) * k_a.reshape(H, N))
    a_vec = -kk
    b_vec = kk * a

    def step(state, inp):
        r_t, w_t, k_t, v_t, a_t, b_t = inp
        sa = jnp.einsum('bhvk,bhk->bhv', state, a_t)
        state = (state * w_t[:, :, None, :] + sa[..., None] * b_t[:, :, None, :]
                 + v_t[..., None] * k_t[:, :, None, :])
        return state, jnp.einsum('bhvk,bhk->bhv', state, r_t)

    xs = (jnp.moveaxis(r, 1, 0), jnp.moveaxis(decay, 1, 0), jnp.moveaxis(k, 1, 0),
          jnp.moveaxis(v, 1, 0), jnp.moveaxis(a_vec, 1, 0), jnp.moveaxis(b_vec, 1, 0))
    _, y = lax.scan(step, jnp.zeros((B, H, N, N), jnp.float32), xs)
    y = jnp.moveaxis(y, 0, 1)
    mean = jnp.mean(y, axis=-1, keepdims=True)
    var = jnp.mean(jnp.square(y - mean), axis=-1, keepdims=True)
    y = (y - mean) * lax.rsqrt(var + RWKV_GN_EPS) * ln_w.reshape(H, N) + ln_b.reshape(H, N)
    y = y + jnp.sum(r * k * r_k, axis=-1, keepdims=True) * v
    return y.reshape(B, S, RWKV_WIDTH) * jax.nn.silu(g.astype(jnp.float32))


def _moba_mix(q, k, v, g):
    B, S, _ = q.shape
    H, D = MOBA_HEADS, MOBA_HEAD_DIM
    n_blocks = -(-S // MOBA_BLOCK)
    s_pad = n_blocks * MOBA_BLOCK

    def heads(t):
        t = t.astype(jnp.float32).reshape(B, S, H, D).transpose(0, 2, 1, 3)
        return jnp.pad(t, ((0, 0), (0, 0), (0, s_pad - S), (0, 0)))

    qh, kh, vh = heads(q), heads(k), heads(v)
    kb = kh.reshape(B, H, n_blocks, MOBA_BLOCK, D)
    vb = vh.reshape(B, H, n_blocks, MOBA_BLOCK, D)
    k_mean = jnp.mean(kb, axis=3)
    q_block = jnp.arange(s_pad) // MOBA_BLOCK
    past = jnp.arange(n_blocks)[None, :] < q_block[:, None]
    gate = jnp.where(past, jnp.einsum('bhsd,bhnd->bhsn', qh, k_mean), -jnp.inf)
    topk = min(MOBA_TOPK, n_blocks)
    _, sel = lax.top_k(gate, topk)
    sel_valid = sel < q_block[None, None, :, None]
    scale = D ** -0.5
    bi = jnp.arange(B)[:, None, None, None]
    hi = jnp.arange(H)[None, :, None, None]

    def chunk(i):
        start = i * MOBA_Q_CHUNK
        q_c = lax.dynamic_slice_in_dim(qh, start, MOBA_Q_CHUNK, axis=2)
        sel_c = lax.dynamic_slice_in_dim(sel, start, MOBA_Q_CHUNK, axis=2)
        val_c = lax.dynamic_slice_in_dim(sel_valid, start, MOBA_Q_CHUNK, axis=2)
        blk = start // MOBA_BLOCK
        k_own = lax.dynamic_slice_in_dim(kh, blk * MOBA_BLOCK, MOBA_BLOCK, axis=2)
        v_own = lax.dynamic_slice_in_dim(vh, blk * MOBA_BLOCK, MOBA_BLOCK, axis=2)
        k_sel = kb[bi, hi, sel_c]
        v_sel = vb[bi, hi, sel_c]
        s_sel = jnp.einsum('bhqd,bhqnjd->bhqnj', q_c, k_sel) * scale
        s_sel = jnp.where(val_c[..., None], s_sel, -jnp.inf)
        s_sel = s_sel.reshape(B, H, MOBA_Q_CHUNK, topk * MOBA_BLOCK)
        q_pos = start + jnp.arange(MOBA_Q_CHUNK)
        k_pos = blk * MOBA_BLOCK + jnp.arange(MOBA_BLOCK)
        s_own = jnp.einsum('bhqd,bhjd->bhqj', q_c, k_own) * scale
        s_own = jnp.where(k_pos[None, :] <= q_pos[:, None], s_own, -jnp.inf)
        p = jax.nn.softmax(jnp.concatenate([s_sel, s_own], axis=-1), axis=-1)
        p_sel = p[..., :topk * MOBA_BLOCK].reshape(B, H, MOBA_Q_CHUNK, topk, MOBA_BLOCK)
        p_own = p[..., topk * MOBA_BLOCK:]
        return (jnp.einsum('bhqnj,bhqnjd->bhqd', p_sel, v_sel)
                + jnp.einsum('bhqj,bhjd->bhqd', p_own, v_own))

    o = lax.map(chunk, jnp.arange(s_pad // MOBA_Q_CHUNK))
    o = o.transpose(1, 2, 0, 3, 4).reshape(B, H, s_pad, D)[:, :, :S]
    o = o.transpose(0, 2, 1, 3).reshape(B, S, MOBA_WIDTH)
    return o * jax.nn.silu(g.astype(jnp.float32))


def _gla_mix(q, k, v, g, gate_low, a_up, a_b, norm_w):
    B, S, _ = q.shape
    H, DK, DV, C = GLA_HEADS, GLA_KEY_DIM, GLA_VALUE_DIM, GLA_CHUNK
    n = S // C
    log_a = jax.nn.log_sigmoid(gate_low.astype(jnp.float32) @ a_up + a_b) / GLA_GATE_TEMP

    def chunks(t, d):
        return t.astype(jnp.float32).reshape(B, n, C, H, d).transpose(0, 3, 1, 2, 4)

    qc = chunks(q, DK) * (DK ** -0.5)
    kc, vc, la = chunks(k, DK), chunks(v, DV), chunks(log_a, DK)
    b = jnp.cumsum(la, axis=3)
    b_last = b[:, :, :, -1:, :]
    qe = qc * jnp.exp(b)
    ke = kc * jnp.exp(-b)
    kd = kc * jnp.exp(b_last - b)
    causal = jnp.tril(jnp.ones((C, C), dtype=bool))
    att = jnp.where(causal, jnp.einsum('bhncd,bhnjd->bhncj', qe, ke), 0.0)
    o_intra = jnp.einsum('bhncj,bhnje->bhnce', att, vc)
    ds = jnp.einsum('bhncd,bhnce->bhnde', kd, vc)
    dec = jnp.exp(b_last[:, :, :, 0, :])

    def step(state, inp):
        ds_n, dec_n = inp
        return dec_n[..., None] * state + ds_n, state

    _, s_prev = lax.scan(step, jnp.zeros((B, H, DK, DV), jnp.float32),
                         (jnp.moveaxis(ds, 2, 0), jnp.moveaxis(dec, 2, 0)))
    s_prev = jnp.moveaxis(s_prev, 0, 2)
    o = o_intra + jnp.einsum('bhncd,bhnde->bhnce', qe, s_prev)
    o = o.transpose(0, 2, 3, 1, 4).reshape(B, S, H, DV)
    o = o * lax.rsqrt(jnp.mean(o * o, axis=-1, keepdims=True) + NORM_EPS) * norm_w
    return o.reshape(B, S, GLA_WIDTH) * jax.nn.silu(g.astype(jnp.float32))


def setup_inputs(seed: int = 0) -> dict:
    key = jax.random.key(seed)
    ks = jax.random.split(key, 20)
    f32 = jnp.float32
    nrm = lambda k_, shape, s: jax.random.normal(k_, shape, f32) * s
    return {
        'x': nrm(ks[0], (BATCH, SEQ, D_MODEL), 1.0),
        'norm_w': 1.0 + nrm(ks[1], (DEPTH, D_MODEL), 0.02),
        'w_in': nrm(ks[2], (DEPTH, D_MODEL, D_IN), D_MODEL ** -0.5),
        'w_out': nrm(ks[3], (DEPTH, D_MIX, D_MODEL), 0.5 * D_MIX ** -0.5),
        'rwkv_mu': jax.random.uniform(ks[4], (DEPTH, RWKV_SHIFT_COLS), f32),
        'rwkv_w0': jax.random.uniform(ks[5], (DEPTH, RWKV_WIDTH), f32, -4.0, 0.0),
        'rwkv_w2': nrm(ks[6], (DEPTH, RWKV_DECAY_RANK, RWKV_WIDTH), 0.1),
        'rwkv_a0': nrm(ks[7], (DEPTH, RWKV_WIDTH), 0.1),
        'rwkv_a2': nrm(ks[8], (DEPTH, RWKV_ICLR_RANK, RWKV_WIDTH), RWKV_ICLR_RANK ** -0.5),
        'rwkv_k_k': 0.85 + nrm(ks[9], (DEPTH, RWKV_WIDTH), 0.05),
        'rwkv_k_a': 1.0 + nrm(ks[10], (DEPTH, RWKV_WIDTH), 0.05),
        'rwkv_r_k': nrm(ks[11], (DEPTH, RWKV_HEADS, RWKV_HEAD_DIM), 0.1),
        'rwkv_ln_w': 1.0 + nrm(ks[12], (DEPTH, RWKV_WIDTH), 0.02),
        'rwkv_ln_b': nrm(ks[13], (DEPTH, RWKV_WIDTH), 0.02),
        'gla_a_up': nrm(ks[14], (DEPTH, GLA_GATE_RANK, GLA_KEY_WIDTH), GLA_GATE_RANK ** -0.5),
        'gla_a_b': nrm(ks[15], (DEPTH, GLA_KEY_WIDTH), 0.1),
        'gla_norm_w': 1.0 + nrm(ks[16], (DEPTH, GLA_VALUE_DIM), 0.02),
        'final_norm_w': 1.0 + nrm(ks[17], (D_MODEL,), 0.02),
    }


def reference(x, norm_w, w_in, w_out, rwkv_mu, rwkv_w0, rwkv_w2, rwkv_a0, rwkv_a2,
              rwkv_k_k, rwkv_k_a, rwkv_r_k, rwkv_ln_w, rwkv_ln_b,
              gla_a_up, gla_a_b, gla_norm_w, final_norm_w):
    for l in range(DEPTH):
        h = _rmsnorm(x, norm_w[l])
        p = h @ w_in[l]
        (p_rwkv, g_a, q_b, k_b, v_b, g_b,
         q_c, k_c, v_c, g_c, low_c) = _split(p, IN_SIZES)
        y_a = _rwkv7_mix(p_rwkv, g_a, rwkv_mu[l], rwkv_w0[l], rwkv_w2[l], rwkv_a0[l], rwkv_a2[l],
                         rwkv_k_k[l], rwkv_k_a[l], rwkv_r_k[l], rwkv_ln_w[l], rwkv_ln_b[l])
        y_b = _moba_mix(q_b, k_b, v_b, g_b)
        y_c = _gla_mix(q_c, k_c, v_c, g_c, low_c, gla_a_up[l], gla_a_b[l], gla_norm_w[l])
        mix = jnp.concatenate([y_a, y_b, y_c], axis=-1).astype(x.dtype)
        x = x + mix @ w_out[l]
    return _rmsnorm(x, final_norm_w)
```

```python
import functools

import jax
import jax.numpy as jnp
from jax import lax
from jax.experimental import pallas as pl
from jax.experimental.pallas import tpu as pltpu

F32 = jnp.float32
BF16 = jnp.bfloat16

_D = 2048
_RW = 1024
_RN = 64
_RLORA = 64
_GN_EPS = 64e-5
_MW = 512
_MD = 128
_MBLK = 256
_MTOPK = 3
_GW = 512
_GKW = 256
_GDK = 64
_GDV = 128
_GRANK = 16
_GTEMP = 16.0
_EPS = 1e-6
_C = 64
_GRP = 256

_O_GV, _O_GG = 0, 512
_O_R, _O_K, _O_V, _O_GA = 1024, 2048, 3072, 4096
_O_GQ, _O_GK = 5120, 5376
_O_MQ, _O_MK, _O_MV, _O_MG = 5632, 6144, 6656, 7168
_O_WA, _O_GL = 7680, 7808
_NP = 7936
assert all(o % _GW == 0 for o in (_O_GV, _O_GG))
assert all(o % _GRP == 0 for o in (_O_R, _O_K, _O_V, _O_GA, _O_GQ, _O_GK))
assert all(o % 128 == 0 for o in (_O_MQ, _O_MK, _O_MV, _O_MG, _O_WA, _O_GL))

_NN = (((1,), (0,)), ((), ()))
_NT = (((1,), (1,)), ((), ()))
_TN = (((0,), (0,)), ((), ()))

_VMEM_LIMIT = 56 * 1024 * 1024


def _parts(x, n):
    if x.dtype == BF16:
        return [x]
    out, rem = [], x
    for i in range(n):
        p = rem.astype(BF16)
        out.append(p)
        if i + 1 < n:
            rem = rem - p.astype(F32)
    return out


def _mm(a, b, dims=_NN, na=1, nb=1):
    pa = a if isinstance(a, list) else _parts(a, na)
    pb = b if isinstance(b, list) else _parts(b, nb)
    order = max(len(pa), len(pb))
    acc = None
    for i, x in enumerate(pa):
        for j, y in enumerate(pb):
            if i + j < order:
                t = lax.dot_general(x, y, dims, preferred_element_type=F32)
                acc = t if acc is None else acc + t
    return acc


def _softplus(x):
    return jnp.maximum(x, 0.0) + jnp.log(1.0 + jnp.exp(-jnp.abs(x)))


def _sigmoid(x):
    return 1.0 / (1.0 + jnp.exp(-x))


def _silu(x):
    return x * _sigmoid(x)


def _iota(shape, dim):
    return lax.broadcasted_iota(jnp.int32, shape, dim)


def _idiv(x, n):
    assert n & (n - 1) == 0
    return jnp.right_shift(x, n.bit_length() - 1)


def _stack_heads(x, head_w, n_heads):
    hid = _idiv(_iota(x.shape, 1), head_w)
    return jnp.concatenate([jnp.where(hid == g, x, 0.0) for g in range(n_heads)], axis=0)


def _unstack_heads(xs, c, n_heads):
    out = xs[0:c]
    for g in range(1, n_heads):
        out = out + xs[g * c:(g + 1) * c]
    return out


def _inproj_kernel(x_ref, nw_ref, w_ref, o_ref, h_ref):
    @pl.when(pl.program_id(1) == 0)
    def _():
        x = x_ref[...]
        ms = jnp.mean(x * x, axis=-1, keepdims=True)
        h_ref[...] = (x * lax.rsqrt(ms + _EPS) * nw_ref[...]).astype(BF16)

    o_ref[...] = jnp.dot(h_ref[...], w_ref[...], preferred_element_type=F32)


def _in_proj(x2, nw, w_bf16, tm=1024, tn=256):
    t, d = x2.shape
    n = w_bf16.shape[1]
    return pl.pallas_call(
        _inproj_kernel,
        out_shape=jax.ShapeDtypeStruct((t, n), F32),
        grid=(t // tm, n // tn),
        in_specs=[
            pl.BlockSpec((tm, d), lambda i, j: (i, 0)),
            pl.BlockSpec((1, d), lambda i, j: (0, 0)),
            pl.BlockSpec((d, tn), lambda i, j: (0, j)),
        ],
        out_specs=pl.BlockSpec((tm, tn), lambda i, j: (i, j)),
        scratch_shapes=[pltpu.VMEM((tm, d), BF16)],
        compiler_params=pltpu.CompilerParams(
            dimension_semantics=("parallel", "arbitrary"),
            vmem_limit_bytes=_VMEM_LIMIT),
        name="in_proj",
    )(x2, nw, w_bf16)


def _token_shift(x, prev_ref, mu):
    c = x.shape[0]
    rows = _iota(x.shape, 0)
    xp = jnp.where(rows == 0, prev_ref[7:8, :], pltpu.roll(x, 1, 0))
    prev_ref[...] = x[c - 8:c, :]
    return x + mu * (xp - x)


def _rwkv_kernel(pr_ref, pk_ref, pv_ref, pg_ref, pwa_ref,
                 mur_ref, muk_ref, muv_ref, muwa_ref,
                 w0_ref, w2_ref, a0_ref, a2_ref, kk_ref, ka_ref, rk_ref, lnw_ref, lnb_ref,
                 o_ref,
                 s_ref, prev_r, prev_k, prev_v, prev_wa):
    c = _C
    nh = _GRP // _RN
    w = _GRP

    @pl.when(pl.program_id(2) == 0)
    def _():
        s_ref[...] = jnp.zeros_like(s_ref)
        prev_r[...] = jnp.zeros_like(prev_r)
        prev_k[...] = jnp.zeros_like(prev_k)
        prev_v[...] = jnp.zeros_like(prev_v)
        prev_wa[...] = jnp.zeros_like(prev_wa)

    r = _token_shift(pr_ref[...], prev_r, mur_ref[...])
    k = _token_shift(pk_ref[...], prev_k, muk_ref[...])
    v = _token_shift(pv_ref[...], prev_v, muv_ref[...])
    wa = _token_shift(pwa_ref[...], prev_wa, muwa_ref[...])

    ri = _iota((nh * c, nh * c), 0)
    ci = _iota((nh * c, nh * c), 1)
    same = _idiv(ri, c) == _idiv(ci, c)
    strict = same & (ci < ri)
    incl = same & (ci <= ri)
    eye = jnp.where(ri == ci, 1.0, 0.0)
    li = _iota((w, w), 0)
    lj = _iota((w, w), 1)
    head_ones = jnp.where(_idiv(li, _RN) == _idiv(lj, _RN), 1.0, 0.0).astype(BF16)
    tri = jnp.where(_iota((c, c), 0) >= _iota((c, c), 1), 1.0, 0.0).astype(BF16)

    wdec = w0_ref[...] + _mm(jnp.tanh(wa), w2_ref[...], na=2, nb=2)
    wdec = -_softplus(-wdec) - 0.5
    lw = -jnp.exp(wdec)
    a_lr = _sigmoid(a0_ref[...] + _mm(wa, a2_ref[...], na=2, nb=2))

    kk = k * kk_ref[...]
    ss = _mm(kk * kk, head_ones, na=3)
    kk = kk * lax.rsqrt(jnp.maximum(ss, 1e-24))
    k2 = k * (1.0 + (a_lr - 1.0) * ka_ref[...])
    aa = -kk
    bb = kk * a_lr

    cl = _mm(tri, lw, nb=3)
    cl_end = cl[c - 1:c, :]
    e_in = jnp.exp(cl)
    e_prev = jnp.exp(cl - lw)
    e_neg = jnp.exp(-cl)
    e_end = jnp.exp(cl_end - cl)
    w_end = jnp.exp(cl_end)

    st = functools.partial(_stack_heads, head_w=_RN, n_heads=nh)
    at_s = _parts(st(aa * e_prev), 2)
    rt_full = st(r * e_in)
    rt_s = _parts(rt_full, 2)
    bt_s = _parts(st(bb * e_neg), 2)
    kt_s = _parts(st(k2 * e_neg), 2)
    bh_s = _parts(st(bb * e_end), 2)
    kh_s = _parts(st(k2 * e_end), 2)
    v_s = _parts(st(v), 2)

    a_ab = jnp.where(strict, _mm(at_s, bt_s, _NT), 0.0)
    a_ak = jnp.where(strict, _mm(at_s, kt_s, _NT), 0.0)
    a_rb = jnp.where(incl, _mm(rt_s, bt_s, _NT), 0.0)
    a_rk = jnp.where(incl, _mm(rt_s, kt_s, _NT), 0.0)

    tinv = eye + a_ab
    pw = a_ab
    n_sq = c.bit_length() - 2
    for _ in range(n_sq):
        pwp = _parts(pw, 2)
        pw = _mm(pwp, pwp)
        tinv = tinv + _mm(tinv, pw, na=2, nb=2)
    tinv = _parts(tinv, 2)
    a_rb = _parts(a_rb, 2)

    abar = _mm(tinv, at_s)
    ubar = _mm(tinv, _mm(a_ak, v_s, na=2), nb=2)
    abar_p = _parts(abar, 2)
    ubar_p = _parts(ubar, 2)
    rbar = rt_full + _mm(a_rb, abar_p)
    ybar = _mm(a_rb, ubar_p) + _mm(a_rk, v_s, na=2)
    m_p = eye * w_end + _mm(abar_p, bh_s, _TN)
    d_p = _mm(ubar_p, bh_s, _TN) + _mm(v_s, kh_s, _TN)

    s0 = s_ref[...]
    s0p = _parts(s0, 2)
    ys = _mm(rbar, s0p, _NT, na=2) + ybar
    s_ref[...] = _mm(s0p, m_p, na=2, nb=2) + d_p
    y = _unstack_heads(ys, c, nh)

    inv_n = 1.0 / _RN
    mean = _mm(y, head_ones, na=3) * inv_n
    yc = y - mean
    var = _mm(yc * yc, head_ones, na=3) * inv_n
    yn = yc * lax.rsqrt(var + _GN_EPS) * lnw_ref[...] + lnb_ref[...]
    bonus = _mm(r * k2 * rk_ref[...], head_ones, na=3) * v
    o_ref[...] = ((yn + bonus) * _silu(pg_ref[...])).astype(o_ref.dtype)


def _rwkv_mix(p3, mu_r, mu_k, mu_v, mu_wa, w0, w2p, a0, a2p, k_k, k_a, r_k, ln_w, ln_b):
    b, s, _ = p3.shape
    ng = _RW // _GRP
    nc = s // _C
    gb = _GRP // 256

    def pspec(off):
        return pl.BlockSpec((None, _C, _GRP), lambda bi, g, c, o=off // _GRP: (bi, c, o + g))

    vec = pl.BlockSpec((1, _GRP), lambda bi, g, c: (0, g))
    lora = pl.BlockSpec((128, _GRP), lambda bi, g, c: (0, g))
    return pl.pallas_call(
        _rwkv_kernel,
        out_shape=jax.ShapeDtypeStruct((b, s, _RW), BF16),
        grid=(b, ng, nc),
        in_specs=[
            pspec(_O_R), pspec(_O_K), pspec(_O_V), pspec(_O_GA),
            pl.BlockSpec((None, _C, 128), lambda bi, g, c: (bi, c, _O_WA // 128)),
            vec, vec, vec,
            pl.BlockSpec((1, 128), lambda bi, g, c: (0, 0)),
            vec, lora, vec, lora, vec, vec, vec, vec, vec,
        ],
        out_specs=pl.BlockSpec((None, _C, _GRP), lambda bi, g, c: (bi, c, g)),
        scratch_shapes=[
            pltpu.VMEM((_GRP, _GRP), F32),
            pltpu.VMEM((8, _GRP), F32), pltpu.VMEM((8, _GRP), F32),
            pltpu.VMEM((8, _GRP), F32), pltpu.VMEM((8, 128), F32),
        ],
        compiler_params=pltpu.CompilerParams(
            dimension_semantics=("parallel", "parallel", "arbitrary"),
            vmem_limit_bytes=_VMEM_LIMIT),
        name="rwkv_mix",
    )(p3, p3, p3, p3, p3, mu_r, mu_k, mu_v, mu_wa, w0, w2p, a0, a2p, k_k, k_a, r_k, ln_w, ln_b)


def _moba_kernel(q_ref, k_ref, v_ref, g_ref, o_ref, kmean_ref, m_ref, l_ref, acc_ref):
    i = pl.program_id(2)
    blk = _MBLK
    nb = k_ref.shape[0] // blk
    scale = _MD ** -0.5
    neg = -1e30

    @pl.when(i == 0)
    def _():
        kmean_ref[...] = jnp.zeros_like(kmean_ref)
        for j in range(nb):
            kmean_ref[j:j + 1, :] = jnp.mean(k_ref[j * blk:(j + 1) * blk, :], axis=0, keepdims=True)

    q = q_ref[...]
    npad = kmean_ref.shape[0]
    gate = _mm(kmean_ref[...], q, _NT, na=3, nb=3)
    bid = _iota((npad, blk), 0)
    gate = jnp.where(bid < i, gate, -jnp.inf)

    qs = (q * scale).astype(BF16)
    kpos = _iota((blk, blk), 0)
    qpos = _iota((blk, blk), 1)

    start = pl.multiple_of(i * blk, blk)
    k_own = k_ref[pl.ds(start, blk), :].astype(BF16)
    v_own = v_ref[pl.ds(start, blk), :].astype(BF16)
    s_t = lax.dot_general(k_own, qs, _NT, preferred_element_type=F32)
    s_t = jnp.where(kpos <= qpos, s_t, neg)
    m0 = jnp.max(s_t, axis=0, keepdims=True)
    p_t = jnp.exp(s_t - m0)
    m_ref[...] = m0
    l_ref[...] = jnp.sum(p_t, axis=0, keepdims=True)
    acc_ref[...] = lax.dot_general(v_own, p_t.astype(BF16), _TN, preferred_element_type=F32)

    for j in range(nb - 1):
        @pl.when(j < i)
        def _(j=j):
            gj = gate[j:j + 1, :]
            ahead = (gate > gj) | ((gate == gj) & (bid < j))
            rank = jnp.sum(jnp.where(ahead, 1.0, 0.0), axis=0, keepdims=True)
            sel = rank < float(_MTOPK)
            k_j = k_ref[j * blk:(j + 1) * blk, :].astype(BF16)
            v_j = v_ref[j * blk:(j + 1) * blk, :].astype(BF16)
            s_j = lax.dot_general(k_j, qs, _NT, preferred_element_type=F32)
            s_j = jnp.where(sel, s_j, neg)
            m_old = m_ref[...]
            m_new = jnp.maximum(m_old, jnp.max(s_j, axis=0, keepdims=True))
            alpha = jnp.exp(m_old - m_new)
            p_j = jnp.exp(s_j - m_new)
            m_ref[...] = m_new
            l_ref[...] = alpha * l_ref[...] + jnp.sum(p_j, axis=0, keepdims=True)
            acc_ref[...] = alpha * acc_ref[...] + lax.dot_general(
                v_j, p_j.astype(BF16), _TN, preferred_element_type=F32)

    o_t = acc_ref[...] / l_ref[...]
    o_ref[...] = (o_t.T * _silu(g_ref[...])).astype(o_ref.dtype)


def _moba_mix(p3):
    b, s, _ = p3.shape
    nh = _MW // _MD
    nq = s // _MBLK
    npad = -(-nq // 8) * 8

    def col(off):
        return off // _MD

    return pl.pallas_call(
        _moba_kernel,
        out_shape=jax.ShapeDtypeStruct((b, s, _MW), BF16),
        grid=(b, nh, nq),
        in_specs=[
            pl.BlockSpec((None, _MBLK, _MD), lambda bi, h, i: (bi, i, col(_O_MQ) + h)),
            pl.BlockSpec((None, s, _MD), lambda bi, h, i: (bi, 0, col(_O_MK) + h)),
            pl.BlockSpec((None, s, _MD), lambda bi, h, i: (bi, 0, col(_O_MV) + h)),
            pl.BlockSpec((None, _MBLK, _MD), lambda bi, h, i: (bi, i, col(_O_MG) + h)),
        ],
        out_specs=pl.BlockSpec((None, _MBLK, _MD), lambda bi, h, i: (bi, i, h)),
        scratch_shapes=[
            pltpu.VMEM((npad, _MD), F32),
            pltpu.VMEM((1, _MBLK), F32), pltpu.VMEM((1, _MBLK), F32),
            pltpu.VMEM((_MD, _MBLK), F32),
        ],
        compiler_params=pltpu.CompilerParams(
            dimension_semantics=("parallel", "parallel", "arbitrary"),
            vmem_limit_bytes=_VMEM_LIMIT),
        name="moba_mix",
    )(p3, p3, p3, p3)


def _gla_kernel(q_ref, k_ref, v_ref, g_ref, low_ref, aup_ref, ab_ref, nw_ref, o_ref, st_ref):
    c = _C
    nh = _GKW // _GDK

    @pl.when(pl.program_id(1) == 0)
    def _():
        st_ref[...] = jnp.zeros_like(st_ref)

    z = _mm(low_ref[...], aup_ref[...], na=2, nb=2) + ab_ref[...]
    la = -_softplus(-z) * (1.0 / _GTEMP)
    tri = jnp.where(_iota((c, c), 0) >= _iota((c, c), 1), 1.0, 0.0).astype(BF16)
    bcum = _mm(tri, la, nb=3)
    b_last = bcum[c - 1:c, :]
    q = q_ref[...] * (_GDK ** -0.5)
    k = k_ref[...]
    qe = _stack_heads(q * jnp.exp(bcum), _GDK, nh)
    ke = _stack_heads(k * jnp.exp(-bcum), _GDK, nh)
    kd = _stack_heads(k * jnp.exp(b_last - bcum), _GDK, nh)
    dec = jnp.exp(b_last)
    v_s = _stack_heads(v_ref[...], _GDV, nh)

    ri = _iota((nh * c, nh * c), 0)
    ci = _iota((nh * c, nh * c), 1)
    incl = (_idiv(ri, c) == _idiv(ci, c)) & (ci <= ri)
    qe_p = _parts(qe, 2)
    v_p = _parts(v_s, 2)
    att = jnp.where(incl, _mm(qe_p, ke, _NT, nb=2), 0.0)
    st = st_ref[...]
    o_s = _mm(att, v_p, na=2) + _mm(qe_p, st, _NT, nb=2)
    st_ref[...] = st * dec + _mm(v_p, kd, _TN, nb=2)
    o = _unstack_heads(o_s, c, nh)

    g = g_ref[...]
    outs = []
    for h in range(nh):
        oh = o[:, h * _GDV:(h + 1) * _GDV]
        ms = jnp.mean(oh * oh, axis=-1, keepdims=True)
        outs.append(oh * lax.rsqrt(ms + _EPS) * nw_ref[...])
    o_ref[...] = (jnp.concatenate(outs, axis=1) * _silu(g)).astype(o_ref.dtype)


def _gla_mix(p3, a_up_p, a_b, norm_w):
    b, s, _ = p3.shape
    nc = s // _C
    return pl.pallas_call(
        _gla_kernel,
        out_shape=jax.ShapeDtypeStruct((b, s, _GW), BF16),
        grid=(b, nc),
        in_specs=[
            pl.BlockSpec((None, _C, _GKW), lambda bi, c: (bi, c, _O_GQ // _GKW)),
            pl.BlockSpec((None, _C, _GKW), lambda bi, c: (bi, c, _O_GK // _GKW)),
            pl.BlockSpec((None, _C, _GW), lambda bi, c: (bi, c, _O_GV // _GW)),
            pl.BlockSpec((None, _C, _GW), lambda bi, c: (bi, c, _O_GG // _GW)),
            pl.BlockSpec((None, _C, 128), lambda bi, c: (bi, c, _O_GL // 128)),
            pl.BlockSpec((128, _GKW), lambda bi, c: (0, 0)),
            pl.BlockSpec((1, _GKW), lambda bi, c: (0, 0)),
            pl.BlockSpec((1, _GDV), lambda bi, c: (0, 0)),
        ],
        out_specs=pl.BlockSpec((None, _C, _GW), lambda bi, c: (bi, c, 0)),
        scratch_shapes=[pltpu.VMEM((_GW, _GKW), F32)],
        compiler_params=pltpu.CompilerParams(
            dimension_semantics=("parallel", "arbitrary"),
            vmem_limit_bytes=_VMEM_LIMIT),
        name="gla_mix",
    )(p3, p3, p3, p3, p3, a_up_p, a_b, norm_w)


def _outproj_kernel(x_ref, ya_ref, yb_ref, yc_ref, w_ref, fw_ref, o_ref, *, final_norm):
    acc = x_ref[...]
    acc = acc + jnp.dot(ya_ref[...], w_ref[0:_RW, :], preferred_element_type=F32)
    acc = acc + jnp.dot(yb_ref[...], w_ref[_RW:_RW + _MW, :], preferred_element_type=F32)
    acc = acc + jnp.dot(yc_ref[...], w_ref[_RW + _MW:, :], preferred_element_type=F32)
    if final_norm:
        ms = jnp.mean(acc * acc, axis=-1, keepdims=True)
        acc = acc * lax.rsqrt(ms + _EPS) * fw_ref[...]
    o_ref[...] = acc


def _out_proj(x2, ya, yb, yc, w_bf16, fw, final_norm, tm=512):
    t, d = x2.shape
    return pl.pallas_call(
        functools.partial(_outproj_kernel, final_norm=final_norm),
        out_shape=jax.ShapeDtypeStruct((t, d), F32),
        grid=(t // tm,),
        in_specs=[
            pl.BlockSpec((tm, d), lambda i: (i, 0)),
            pl.BlockSpec((tm, _RW), lambda i: (i, 0)),
            pl.BlockSpec((tm, _MW), lambda i: (i, 0)),
            pl.BlockSpec((tm, _GW), lambda i: (i, 0)),
            pl.BlockSpec((d, d), lambda i: (0, 0)),
            pl.BlockSpec((1, d), lambda i: (0, 0)),
        ],
        out_specs=pl.BlockSpec((tm, d), lambda i: (i, 0)),
        compiler_params=pltpu.CompilerParams(
            dimension_semantics=("parallel",),
            vmem_limit_bytes=_VMEM_LIMIT),
        name="out_proj",
    )(x2, ya, yb, yc, w_bf16, fw)


def _split_cols(w):
    sizes = (_RW, _RW, _RW, _RLORA, _RLORA, _RW, _MW, _MW, _MW, _MW, _GKW, _GKW, _GW, _GW, _GRANK)
    offs, o = [], 0
    for sz in sizes:
        offs.append((o, o + sz))
        o += sz
    return [w[..., a:b] for a, b in offs]


def _relayout_cols(w):
    r, k, v, wd, ad, ga, mq, mk, mv, mg, gq, gk, gv, gg, gl = _split_cols(w)
    pad = jnp.zeros(w.shape[:-1] + (_NP - _O_GL - _GRANK,), w.dtype)
    return jnp.concatenate([gv, gg, r, k, v, ga, gq, gk, mq, mk, mv, mg, wd, ad, gl, pad], axis=-1)


def kernel(x, norm_w, w_in, w_out, rwkv_mu, rwkv_w0, rwkv_w2, rwkv_a0, rwkv_a2, rwkv_k_k, rwkv_k_a,
           rwkv_r_k, rwkv_ln_w, rwkv_ln_b, gla_a_up, gla_a_b, gla_norm_w, final_norm_w):
    b, s, d = x.shape
    depth = norm_w.shape[0]
    x2 = x.reshape(b * s, d)
    zero_lora = jnp.zeros((_RLORA, _RW), F32)
    fw = final_norm_w.reshape(1, d)
    for l in range(depth):
        w_in_l = _relayout_cols(w_in[l]).astype(BF16)
        w_out_l = w_out[l].astype(BF16)
        mu = rwkv_mu[l]
        mu_r, mu_k, mu_v = (mu[i * _RW:(i + 1) * _RW].reshape(1, _RW) for i in range(3))
        mu_wa = mu[3 * _RW:].reshape(1, 2 * _RLORA)
        w2p = jnp.concatenate([rwkv_w2[l], zero_lora], axis=0)
        a2p = jnp.concatenate([zero_lora, rwkv_a2[l]], axis=0)
        a_up_p = jnp.concatenate([gla_a_up[l], jnp.zeros((128 - _GRANK, _GKW), F32)], axis=0)

        p = _in_proj(x2, norm_w[l].reshape(1, d), w_in_l)
        p3 = p.reshape(b, s, _NP)
        row = lambda a: a.reshape(1, _RW)
        y_a = _rwkv_mix(p3, mu_r, mu_k, mu_v, mu_wa, row(rwkv_w0[l]), w2p, row(rwkv_a0[l]), a2p,
                        row(rwkv_k_k[l]), row(rwkv_k_a[l]), row(rwkv_r_k[l]),
                        row(rwkv_ln_w[l]), row(rwkv_ln_b[l]))
        y_b = _moba_mix(p3)
        y_c = _gla_mix(p3, a_up_p, gla_a_b[l].reshape(1, _GKW), gla_norm_w[l].reshape(1, _GDV))
        x2 = _out_proj(x2, y_a.reshape(b * s, _RW), y_b.reshape(b * s, _MW), y_c.reshape(b * s, _GW),
                       w_out_l, fw, final_norm=(l == depth - 1))
    return x2.reshape(b, s, d)
```

```python
import functools

import jax
import jax.numpy as jnp
from jax import lax
from jax.experimental import pallas as pl
from jax.experimental.pallas import tpu as pltpu

F32 = jnp.float32
BF16 = jnp.bfloat16

_D = 2048
_RW = 1024
_RN = 64
_RLORA = 64
_GN_EPS = 64e-5
_MW = 512
_MD = 128
_MBLK = 256
_MTOPK = 3
_GW = 512
_GKW = 256
_GDK = 64
_GDV = 128
_GRANK = 16
_GTEMP = 16.0
_EPS = 1e-6
_C = 64
_GRP = 256

_O_GV, _O_GG = 0, 512
_O_R, _O_K, _O_V, _O_GA = 1024, 2048, 3072, 4096
_O_GQ, _O_GK = 5120, 5376
_O_MQ, _O_MK, _O_MV, _O_MG = 5632, 6144, 6656, 7168
_O_WA, _O_GL = 7680, 7808
_NP = 8192
assert all(o % _GW == 0 for o in (_O_GV, _O_GG))
assert all(o % _RW == 0 for o in (_O_R, _O_K, _O_V, _O_GA))
assert all(o % _GKW == 0 for o in (_O_GQ, _O_GK))
assert all(o % _MW == 0 for o in (_O_MQ, _O_MK, _O_MV, _O_MG))
assert all(o % 128 == 0 for o in (_O_WA, _O_GL))

_NN = (((1,), (0,)), ((), ()))
_NT = (((1,), (1,)), ((), ()))
_TN = (((0,), (0,)), ((), ()))

_VMEM_LIMIT = 56 * 1024 * 1024

_P_INV = 1
_P_MAIN = 1
_P_STATE = 2
_P_SUM = 3


def _parts(x, n):
    if x.dtype == BF16:
        return [x]
    out, rem = [], x
    for i in range(n):
        p = rem.astype(BF16)
        out.append(p)
        if i + 1 < n:
            rem = rem - p.astype(F32)
    return out


def _mm(a, b, dims=_NN, na=1, nb=1):
    pa = a if isinstance(a, list) else _parts(a, na)
    pb = b if isinstance(b, list) else _parts(b, nb)
    order = max(len(pa), len(pb))
    acc = None
    for i, x in enumerate(pa):
        for j, y in enumerate(pb):
            if i + j < order:
                t = lax.dot_general(x, y, dims, preferred_element_type=F32)
                acc = t if acc is None else acc + t
    return acc


def _softplus(x):
    return jnp.maximum(x, 0.0) + jnp.log(1.0 + jnp.exp(-jnp.abs(x)))


def _sigmoid(x):
    return 1.0 / (1.0 + jnp.exp(-x))


def _silu(x):
    return x * _sigmoid(x)


def _iota(shape, dim):
    return lax.broadcasted_iota(jnp.int32, shape, dim)


def _idiv(x, n):
    assert n & (n - 1) == 0
    return jnp.right_shift(x, n.bit_length() - 1)


def _stack_heads(x, head_w, n_heads):
    hid = _idiv(_iota(x.shape, 1), head_w)
    return jnp.concatenate([jnp.where(hid == g, x, 0.0) for g in range(n_heads)], axis=0)


def _unstack_heads(xs, c, n_heads):
    out = xs[0:c]
    for g in range(1, n_heads):
        out = out + xs[g * c:(g + 1) * c]
    return out


def _inproj_kernel(x_ref, nw_ref, w_ref, o_ref, h_ref):
    @pl.when(pl.program_id(1) == 0)
    def _():
        x = x_ref[...]
        ms = jnp.mean(x * x, axis=-1, keepdims=True)
        h_ref[...] = (x * lax.rsqrt(ms + _EPS) * nw_ref[...]).astype(BF16)

    o_ref[...] = jnp.dot(h_ref[...], w_ref[...], preferred_element_type=F32)


def _in_proj(x2, nw, w_bf16, tm=1024, tn=1024):
    t, d = x2.shape
    n = w_bf16.shape[1]
    return pl.pallas_call(
        _inproj_kernel,
        out_shape=jax.ShapeDtypeStruct((t, n), F32),
        grid=(t // tm, n // tn),
        in_specs=[
            pl.BlockSpec((tm, d), lambda i, j: (i, 0)),
            pl.BlockSpec((1, d), lambda i, j: (0, 0)),
            pl.BlockSpec((d, tn), lambda i, j: (0, j)),
        ],
        out_specs=pl.BlockSpec((tm, tn), lambda i, j: (i, j)),
        scratch_shapes=[pltpu.VMEM((tm, d), BF16)],
        compiler_params=pltpu.CompilerParams(
            dimension_semantics=("parallel", "arbitrary"),
            vmem_limit_bytes=_VMEM_LIMIT),
        name="in_proj",
    )(x2, nw, w_bf16)


def _token_shift(x, prev_ref, mu):
    c = x.shape[0]
    rows = _iota(x.shape, 0)
    xp = jnp.where(rows == 0, prev_ref[7:8, :], pltpu.roll(x, 1, 0))
    prev_ref[...] = x[c - 8:c, :]
    return x + mu * (xp - x)


def _head_sums(x, head_ones):
    cols = [_mm(x[:, g * _GRP:(g + 1) * _GRP], head_ones, na=_P_SUM) for g in range(x.shape[1] // _GRP)]
    return jnp.concatenate(cols, axis=1)


def _rwkv_groups(at, rt, bt, kt, bh, kh, v, w_end, s_ref, strict, incl, eye):
    c = _C
    nh = _GRP // _RN
    ng = at.shape[1] // _GRP
    gs = range(ng)

    def stacked(x, n):
        return [_parts(_stack_heads(x[:, g * _GRP:(g + 1) * _GRP], _RN, nh), n) for g in gs]

    rt_full = [_stack_heads(rt[:, g * _GRP:(g + 1) * _GRP], _RN, nh) for g in gs]
    at_s, bt_s, kt_s = stacked(at, _P_MAIN), stacked(bt, _P_MAIN), stacked(kt, _P_MAIN)
    rt_s = [_parts(x, _P_MAIN) for x in rt_full]
    bh_s, kh_s, v_s = stacked(bh, _P_MAIN), stacked(kh, _P_MAIN), stacked(v, _P_MAIN)

    a_ab = [_mm(at_s[g], bt_s[g], _NT) * strict for g in gs]
    a_ak = [_mm(at_s[g], kt_s[g], _NT) * strict for g in gs]
    a_rb = [_parts(_mm(rt_s[g], bt_s[g], _NT) * incl, _P_MAIN) for g in gs]
    a_rk = [_mm(rt_s[g], kt_s[g], _NT) * incl for g in gs]

    tinv = [eye + a_ab[g] for g in gs]
    pw = a_ab
    for _ in range(c.bit_length() - 2):
        pwp = [_parts(pw[g], _P_INV) for g in gs]
        pw = [_mm(pwp[g], pwp[g]) for g in gs]
        tinv = [tinv[g] + _mm(tinv[g], pw[g], na=_P_INV, nb=_P_INV) for g in gs]
    tinv = [_parts(tinv[g], _P_MAIN) for g in gs]

    uv = [_mm(a_ak[g], v_s[g], na=_P_MAIN) for g in gs]
    abar = [_parts(_mm(tinv[g], at_s[g]), _P_MAIN) for g in gs]
    ubar = [_parts(_mm(tinv[g], uv[g], nb=_P_MAIN), _P_MAIN) for g in gs]
    rbar = [rt_full[g] + _mm(a_rb[g], abar[g]) for g in gs]
    ybar = [_mm(a_rb[g], ubar[g]) + _mm(a_rk[g], v_s[g], na=_P_MAIN) for g in gs]
    m_p = [eye * w_end[:, g * _GRP:(g + 1) * _GRP] + _mm(abar[g], bh_s[g], _TN) for g in gs]
    d_p = [_mm(ubar[g], bh_s[g], _TN) + _mm(v_s[g], kh_s[g], _TN) for g in gs]

    s0p = [_parts(s_ref[g], _P_STATE) for g in gs]
    ys = [_mm(rbar[g], s0p[g], _NT, na=_P_STATE) + ybar[g] for g in gs]
    for g in gs:
        s_ref[g] = _mm(s0p[g], m_p[g], na=_P_STATE, nb=_P_STATE) + d_p[g]
    return jnp.concatenate([_unstack_heads(ys[g], c, nh) for g in gs], axis=1)


def _rwkv_kernel(pr_ref, pk_ref, pv_ref, pg_ref, pwa_ref,
                 mur_ref, muk_ref, muv_ref, muwa_ref,
                 w0_ref, w2_ref, a0_ref, a2_ref, kk_ref, ka_ref, rk_ref, lnw_ref, lnb_ref,
                 strict_ref, incl_ref, eye_ref, hones_ref, tri_ref,
                 o_ref,
                 s_ref, prev_r, prev_k, prev_v, prev_wa):
    c = _C
    ng = _RW // _GRP

    @pl.when(pl.program_id(1) == 0)
    def _():
        s_ref[...] = jnp.zeros_like(s_ref)
        prev_r[...] = jnp.zeros_like(prev_r)
        prev_k[...] = jnp.zeros_like(prev_k)
        prev_v[...] = jnp.zeros_like(prev_v)
        prev_wa[...] = jnp.zeros_like(prev_wa)

    r = _token_shift(pr_ref[...], prev_r, mur_ref[...])
    k = _token_shift(pk_ref[...], prev_k, muk_ref[...])
    v = _token_shift(pv_ref[...], prev_v, muv_ref[...])
    wa = _token_shift(pwa_ref[...], prev_wa, muwa_ref[...])
    head_ones = hones_ref[...]

    wdec = w0_ref[...] + _mm(jnp.tanh(wa), w2_ref[...], na=2, nb=2)
    wdec = -_softplus(-wdec) - 0.5
    lw = -jnp.exp(wdec)
    a_lr = _sigmoid(a0_ref[...] + _mm(wa, a2_ref[...], na=2, nb=2))

    kk = k * kk_ref[...]
    kk = kk * lax.rsqrt(jnp.maximum(_head_sums(kk * kk, head_ones), 1e-24))
    k2 = k * (1.0 + (a_lr - 1.0) * ka_ref[...])
    bb = kk * a_lr

    cl = _mm(tri_ref[...], lw, nb=_P_SUM)
    cl_end = cl[c - 1:c, :]
    e_in = jnp.exp(cl)
    e_neg = jnp.exp(-cl)
    e_end = jnp.exp(cl_end - cl)
    w_end = jnp.exp(cl_end)
    at = -kk * jnp.exp(cl - lw)
    rt = r * e_in
    bt = bb * e_neg
    kt = k2 * e_neg
    bh = bb * e_end
    kh = k2 * e_end

    y = _rwkv_groups(at, rt, bt, kt, bh, kh, v, w_end, s_ref,
                     strict_ref[...], incl_ref[...], eye_ref[...])

    inv_n = 1.0 / _RN
    mean = _head_sums(y, head_ones) * inv_n
    yc = y - mean
    var = _head_sums(yc * yc, head_ones) * inv_n
    yn = yc * lax.rsqrt(var + _GN_EPS) * lnw_ref[...] + lnb_ref[...]
    bonus = _head_sums(r * k2 * rk_ref[...], head_ones) * v
    o_ref[...] = ((yn + bonus) * _silu(pg_ref[...])).astype(o_ref.dtype)


def _rwkv_consts():
    n = _GRP
    ri = lax.broadcasted_iota(jnp.int32, (n, n), 0)
    ci = lax.broadcasted_iota(jnp.int32, (n, n), 1)
    same = (ri // _C) == (ci // _C)
    strict = (same & (ci < ri)).astype(F32)
    incl = (same & (ci <= ri)).astype(F32)
    eye = (ri == ci).astype(F32)
    head_ones = ((ri // _RN) == (ci // _RN)).astype(BF16)
    ti = lax.broadcasted_iota(jnp.int32, (_C, _C), 0)
    tj = lax.broadcasted_iota(jnp.int32, (_C, _C), 1)
    tri = (ti >= tj).astype(BF16)
    return strict, incl, eye, head_ones, tri


def _rwkv_mix(p3, mu_r, mu_k, mu_v, mu_wa, w0, w2p, a0, a2p, k_k, k_a, r_k, ln_w, ln_b):
    b, s, _ = p3.shape
    nc = s // _C
    ng = _RW // _GRP

    def pspec(off):
        return pl.BlockSpec((None, _C, _RW), lambda bi, c, o=off // _RW: (bi, c, o))

    def full(shape):
        return pl.BlockSpec(shape, lambda bi, c: (0,) * len(shape))

    vec = full((1, _RW))
    lora = full((2 * _RLORA, _RW))
    sq = full((_GRP, _GRP))
    return pl.pallas_call(
        _rwkv_kernel,
        out_shape=jax.ShapeDtypeStruct((b, s, _RW), BF16),
        grid=(b, nc),
        in_specs=[
            pspec(_O_R), pspec(_O_K), pspec(_O_V), pspec(_O_GA),
            pl.BlockSpec((None, _C, 128), lambda bi, c: (bi, c, _O_WA // 128)),
            vec, vec, vec, full((1, 2 * _RLORA)),
            vec, lora, vec, lora, vec, vec, vec, vec, vec,
            sq, sq, sq, sq, full((_C, _C)),
        ],
        out_specs=pl.BlockSpec((None, _C, _RW), lambda bi, c: (bi, c, 0)),
        scratch_shapes=[
            pltpu.VMEM((ng, _GRP, _GRP), F32),
            pltpu.VMEM((8, _RW), F32), pltpu.VMEM((8, _RW), F32),
            pltpu.VMEM((8, _RW), F32), pltpu.VMEM((8, 2 * _RLORA), F32),
        ],
        compiler_params=pltpu.CompilerParams(
            dimension_semantics=("parallel", "arbitrary"),
            vmem_limit_bytes=_VMEM_LIMIT),
        name="rwkv_mix",
    )(p3, p3, p3, p3, p3, mu_r, mu_k, mu_v, mu_wa, w0, w2p, a0, a2p, k_k, k_a, r_k, ln_w, ln_b,
      *_rwkv_consts())


def _moba_kernel(q_ref, k_ref, v_ref, g_ref, o_ref, kmean_ref, m_ref, l_ref, acc_ref):
    i = pl.program_id(1)
    blk = _MBLK
    nb = k_ref.shape[0] // blk
    nh = _MW // _MD
    hs = range(nh)
    scale = _MD ** -0.5
    neg = -1e30

    def hsl(h):
        return slice(h * _MD, (h + 1) * _MD)

    @pl.when(i == 0)
    def _():
        kmean_ref[...] = jnp.zeros_like(kmean_ref)
        for j in range(nb):
            kmean_ref[j:j + 1, :] = jnp.mean(k_ref[j * blk:(j + 1) * blk, :], axis=0, keepdims=True)

    npad = kmean_ref.shape[0]
    bid = _iota((npad, blk), 0)
    q = [q_ref[:, hsl(h)] for h in hs]
    gate = [jnp.where(bid < i, _mm(kmean_ref[:, hsl(h)], q[h], _NT, na=3, nb=3), -jnp.inf)
            for h in hs]
    qs = [(q[h] * scale).astype(BF16) for h in hs]
    kpos = _iota((blk, blk), 0)
    qpos = _iota((blk, blk), 1)

    start = pl.multiple_of(i * blk, blk)
    s_t = [lax.dot_general(k_ref[pl.ds(start, blk), hsl(h)].astype(BF16), qs[h], _NT,
                           preferred_element_type=F32) for h in hs]
    s_t = [jnp.where(kpos <= qpos, s_t[h], neg) for h in hs]
    m0 = [jnp.max(s_t[h], axis=0, keepdims=True) for h in hs]
    p_t = [jnp.exp(s_t[h] - m0[h]) for h in hs]
    for h in hs:
        m_ref[h] = m0[h]
        l_ref[h] = jnp.sum(p_t[h], axis=0, keepdims=True)
        acc_ref[h] = lax.dot_general(v_ref[pl.ds(start, blk), hsl(h)].astype(BF16), p_t[h].astype(BF16), _TN,
                                     preferred_element_type=F32)

    for j in range(nb - 1):
        @pl.when(j < i)
        def _(j=j):
            rows = slice(j * blk, (j + 1) * blk)
            s_j = [lax.dot_general(k_ref[rows, hsl(h)].astype(BF16), qs[h], _NT,
                                   preferred_element_type=F32) for h in hs]
            p_j, alpha = [], []
            for h in hs:
                gj = gate[h][j:j + 1, :]
                ahead = (gate[h] > gj) | ((gate[h] == gj) & (bid < j))
                rank = jnp.sum(jnp.where(ahead, 1.0, 0.0), axis=0, keepdims=True)
                s_h = jnp.where(rank < float(_MTOPK), s_j[h], neg)
                m_old = m_ref[h]
                m_new = jnp.maximum(m_old, jnp.max(s_h, axis=0, keepdims=True))
                alpha.append(jnp.exp(m_old - m_new))
                p_j.append(jnp.exp(s_h - m_new))
                m_ref[h] = m_new
            for h in hs:
                l_ref[h] = alpha[h] * l_ref[h] + jnp.sum(p_j[h], axis=0, keepdims=True)
                acc_ref[h] = alpha[h] * acc_ref[h] + lax.dot_general(
                    v_ref[rows, hsl(h)].astype(BF16), p_j[h].astype(BF16), _TN, preferred_element_type=F32)

    o = jnp.concatenate([(acc_ref[h] / l_ref[h]).T for h in hs], axis=1)
    o_ref[...] = (o * _silu(g_ref[...])).astype(o_ref.dtype)


def _moba_mix(p3):
    b, s, _ = p3.shape
    nh = _MW // _MD
    nq = s // _MBLK
    npad = -(-nq // 8) * 8
    return pl.pallas_call(
        _moba_kernel,
        out_shape=jax.ShapeDtypeStruct((b, s, _MW), BF16),
        grid=(b, nq),
        in_specs=[
            pl.BlockSpec((None, _MBLK, _MW), lambda bi, i: (bi, i, _O_MQ // _MW)),
            pl.BlockSpec((None, s, _MW), lambda bi, i: (bi, 0, _O_MK // _MW)),
            pl.BlockSpec((None, s, _MW), lambda bi, i: (bi, 0, _O_MV // _MW)),
            pl.BlockSpec((None, _MBLK, _MW), lambda bi, i: (bi, i, _O_MG // _MW)),
        ],
        out_specs=pl.BlockSpec((None, _MBLK, _MW), lambda bi, i: (bi, i, 0)),
        scratch_shapes=[
            pltpu.VMEM((npad, _MW), F32),
            pltpu.VMEM((nh, 1, _MBLK), F32), pltpu.VMEM((nh, 1, _MBLK), F32),
            pltpu.VMEM((nh, _MD, _MBLK), F32),
        ],
        compiler_params=pltpu.CompilerParams(
            dimension_semantics=("parallel", "arbitrary"),
            vmem_limit_bytes=_VMEM_LIMIT),
        name="moba_mix",
    )(p3, p3, p3, p3)


def _gla_kernel(q_ref, k_ref, v_ref, g_ref, low_ref, aup_ref, ab_ref, nw_ref, incl_ref, tri_ref,
                o_ref, st_ref):
    c = _C
    nh = _GKW // _GDK
    bs = range(q_ref.shape[0])

    @pl.when(pl.program_id(1) == 0)
    def _():
        st_ref[...] = jnp.zeros_like(st_ref)

    incl, tri = incl_ref[...], tri_ref[...]
    z = [_mm(low_ref[b], aup_ref[...], na=2, nb=2) + ab_ref[...] for b in bs]
    la = [-_softplus(-z[b]) * (1.0 / _GTEMP) for b in bs]
    bcum = [_mm(tri, la[b], nb=_P_SUM) for b in bs]
    b_last = [bcum[b][c - 1:c, :] for b in bs]
    k = [k_ref[b] for b in bs]
    qe = [_parts(_stack_heads(q_ref[b] * (_GDK ** -0.5) * jnp.exp(bcum[b]), _GDK, nh), 1) for b in bs]
    ke = [_stack_heads(k[b] * jnp.exp(-bcum[b]), _GDK, nh) for b in bs]
    kd = [_stack_heads(k[b] * jnp.exp(b_last[b] - bcum[b]), _GDK, nh) for b in bs]
    v_s = [_parts(_stack_heads(v_ref[b], _GDV, nh), 1) for b in bs]

    att = [_mm(qe[b], ke[b], _NT) * incl for b in bs]
    st = [st_ref[b] for b in bs]
    o_s = [_mm(att[b], v_s[b]) + _mm(qe[b], st[b], _NT) for b in bs]
    for b in bs:
        st_ref[b] = st[b] * jnp.exp(b_last[b]) + _mm(v_s[b], kd[b], _TN)

    for b in bs:
        o = _unstack_heads(o_s[b], c, nh)
        outs = []
        for h in range(nh):
            oh = o[:, h * _GDV:(h + 1) * _GDV]
            ms = jnp.mean(oh * oh, axis=-1, keepdims=True)
            outs.append(oh * lax.rsqrt(ms + _EPS) * nw_ref[...])
        o_ref[b] = (jnp.concatenate(outs, axis=1) * _silu(g_ref[b])).astype(o_ref.dtype)


def _gla_mix(p3, a_up_p, a_b, norm_w, bb=4):
    b, s, _ = p3.shape
    nc = s // _C
    bb = min(bb, b)
    assert b % bb == 0
    n = _GKW // _GDK * _C
    ri = lax.broadcasted_iota(jnp.int32, (n, n), 0)
    ci = lax.broadcasted_iota(jnp.int32, (n, n), 1)
    incl = (((ri // _C) == (ci // _C)) & (ci <= ri)).astype(F32)
    tri = (lax.broadcasted_iota(jnp.int32, (_C, _C), 0) >= lax.broadcasted_iota(jnp.int32, (_C, _C), 1)).astype(BF16)

    def full(shape):
        return pl.BlockSpec(shape, lambda bi, c: (0,) * len(shape))

    return pl.pallas_call(
        _gla_kernel,
        out_shape=jax.ShapeDtypeStruct((b, s, _GW), BF16),
        grid=(b // bb, nc),
        in_specs=[
            pl.BlockSpec((bb, _C, _GKW), lambda bi, c: (bi, c, _O_GQ // _GKW)),
            pl.BlockSpec((bb, _C, _GKW), lambda bi, c: (bi, c, _O_GK // _GKW)),
            pl.BlockSpec((bb, _C, _GW), lambda bi, c: (bi, c, _O_GV // _GW)),
            pl.BlockSpec((bb, _C, _GW), lambda bi, c: (bi, c, _O_GG // _GW)),
            pl.BlockSpec((bb, _C, 128), lambda bi, c: (bi, c, _O_GL // 128)),
            full((128, _GKW)), full((1, _GKW)), full((1, _GDV)), full((n, n)), full((_C, _C)),
        ],
        out_specs=pl.BlockSpec((bb, _C, _GW), lambda bi, c: (bi, c, 0)),
        scratch_shapes=[pltpu.VMEM((bb, _GW, _GKW), F32)],
        compiler_params=pltpu.CompilerParams(
            dimension_semantics=("parallel", "arbitrary"),
            vmem_limit_bytes=_VMEM_LIMIT),
        name="gla_mix",
    )(p3, p3, p3, p3, p3, a_up_p, a_b, norm_w, incl, tri)


def _outproj_kernel(x_ref, ya_ref, yb_ref, yc_ref, w_ref, fw_ref, o_ref, *, final_norm):
    acc = x_ref[...]
    acc = acc + jnp.dot(ya_ref[...], w_ref[0:_RW, :], preferred_element_type=F32)
    acc = acc + jnp.dot(yb_ref[...], w_ref[_RW:_RW + _MW, :], preferred_element_type=F32)
    acc = acc + jnp.dot(yc_ref[...], w_ref[_RW + _MW:, :], preferred_element_type=F32)
    if final_norm:
        ms = jnp.mean(acc * acc, axis=-1, keepdims=True)
        acc = acc * lax.rsqrt(ms + _EPS) * fw_ref[...]
    o_ref[...] = acc


def _out_proj(x2, ya, yb, yc, w_bf16, fw, final_norm, tm=512):
    t, d = x2.shape
    return pl.pallas_call(
        functools.partial(_outproj_kernel, final_norm=final_norm),
        out_shape=jax.ShapeDtypeStruct((t, d), F32),
        grid=(t // tm,),
        in_specs=[
            pl.BlockSpec((tm, d), lambda i: (i, 0)),
            pl.BlockSpec((tm, _RW), lambda i: (i, 0)),
            pl.BlockSpec((tm, _MW), lambda i: (i, 0)),
            pl.BlockSpec((tm, _GW), lambda i: (i, 0)),
            pl.BlockSpec((d, d), lambda i: (0, 0)),
            pl.BlockSpec((1, d), lambda i: (0, 0)),
        ],
        out_specs=pl.BlockSpec((tm, d), lambda i: (i, 0)),
        compiler_params=pltpu.CompilerParams(
            dimension_semantics=("parallel",),
            vmem_limit_bytes=_VMEM_LIMIT),
        name="out_proj",
    )(x2, ya, yb, yc, w_bf16, fw)


def _split_cols(w):
    sizes = (_RW, _RW, _RW, _RLORA, _RLORA, _RW, _MW, _MW, _MW, _MW, _GKW, _GKW, _GW, _GW, _GRANK)
    offs, o = [], 0
    for sz in sizes:
        offs.append((o, o + sz))
        o += sz
    return [w[..., a:b] for a, b in offs]


def _relayout_cols(w):
    r, k, v, wd, ad, ga, mq, mk, mv, mg, gq, gk, gv, gg, gl = _split_cols(w)
    pad = jnp.zeros(w.shape[:-1] + (_NP - _O_GL - _GRANK,), w.dtype)
    return jnp.concatenate([gv, gg, r, k, v, ga, gq, gk, mq, mk, mv, mg, wd, ad, gl, pad], axis=-1)


def kernel(x, norm_w, w_in, w_out, rwkv_mu, rwkv_w0, rwkv_w2, rwkv_a0, rwkv_a2, rwkv_k_k, rwkv_k_a,
           rwkv_r_k, rwkv_ln_w, rwkv_ln_b, gla_a_up, gla_a_b, gla_norm_w, final_norm_w):
    b, s, d = x.shape
    depth = norm_w.shape[0]
    x2 = x.reshape(b * s, d)
    zero_lora = jnp.zeros((_RLORA, _RW), F32)
    fw = final_norm_w.reshape(1, d)
    for l in range(depth):
        w_in_l = _relayout_cols(w_in[l]).astype(BF16)
        w_out_l = w_out[l].astype(BF16)
        mu = rwkv_mu[l]
        mu_r, mu_k, mu_v = (mu[i * _RW:(i + 1) * _RW].reshape(1, _RW) for i in range(3))
        mu_wa = mu[3 * _RW:].reshape(1, 2 * _RLORA)
        w2p = jnp.concatenate([rwkv_w2[l], zero_lora], axis=0)
        a2p = jnp.concatenate([zero_lora, rwkv_a2[l]], axis=0)
        a_up_p = jnp.concatenate([gla_a_up[l], jnp.zeros((128 - _GRANK, _GKW), F32)], axis=0)

        p = _in_proj(x2, norm_w[l].reshape(1, d), w_in_l)
        p3 = p.reshape(b, s, _NP)
        row = lambda a: a.reshape(1, _RW)
        y_a = _rwkv_mix(p3, mu_r, mu_k, mu_v, mu_wa, row(rwkv_w0[l]), w2p, row(rwkv_a0[l]), a2p,
                        row(rwkv_k_k[l]), row(rwkv_k_a[l]), row(rwkv_r_k[l]),
                        row(rwkv_ln_w[l]), row(rwkv_ln_b[l]))
        y_b = _moba_mix(p3)
        y_c = _gla_mix(p3, a_up_p, gla_a_b[l].reshape(1, _GKW), gla_norm_w[l].reshape(1, _GDV))
        x2 = _out_proj(x2, y_a.reshape(b * s, _RW), y_b.reshape(b * s, _MW), y_c.reshape(b * s, _GW),
                       w_out_l, fw, final_norm=(l == depth - 1))
    return x2.reshape(b, s, d)
```

```python
import functools

import jax
import jax.numpy as jnp
from jax import lax
from jax.experimental import pallas as pl
from jax.experimental.pallas import tpu as pltpu

F32 = jnp.float32
BF16 = jnp.bfloat16

_D = 2048
_RW = 1024
_RN = 64
_RLORA = 64
_GN_EPS = 64e-5
_MW = 512
_MD = 128
_MBLK = 256
_MTOPK = 3
_GW = 512
_GKW = 256
_GDK = 64
_GDV = 128
_GRANK = 16
_GTEMP = 16.0
_EPS = 1e-6
_C = 64
_GRP = 256

_O_GV, _O_GG = 0, 512
_O_R, _O_K, _O_V, _O_GA = 1024, 2048, 3072, 4096
_O_GQ, _O_GK = 5120, 5376
_O_MQ, _O_MK, _O_MV, _O_MG = 5632, 6144, 6656, 7168
_O_WA, _O_GL = 7680, 7808
_NP = 8192
assert all(o % _GW == 0 for o in (_O_GV, _O_GG))
assert all(o % _RW == 0 for o in (_O_R, _O_K, _O_V, _O_GA))
assert all(o % _GKW == 0 for o in (_O_GQ, _O_GK))
assert all(o % _MW == 0 for o in (_O_MQ, _O_MK, _O_MV, _O_MG))
assert all(o % 128 == 0 for o in (_O_WA, _O_GL))

_NN = (((1,), (0,)), ((), ()))
_NT = (((1,), (1,)), ((), ()))
_TN = (((0,), (0,)), ((), ()))

_VMEM_LIMIT = 56 * 1024 * 1024

_P_INV = 1
_P_MAIN = 1
_P_STATE = 1
_P_SUM = 3
_P_HEADSUM = 2


def _parts(x, n):
    if x.dtype == BF16:
        return [x]
    out, rem = [], x
    for i in range(n):
        p = rem.astype(BF16)
        out.append(p)
        if i + 1 < n:
            rem = rem - p.astype(F32)
    return out


def _mm(a, b, dims=_NN, na=1, nb=1):
    pa = a if isinstance(a, list) else _parts(a, na)
    pb = b if isinstance(b, list) else _parts(b, nb)
    order = max(len(pa), len(pb))
    free_axis = 1 if dims[0][0] == (0,) else 0
    m = pa[0].shape[free_axis]
    acc = None
    for j, y in enumerate(pb):
        xs = pa[:order - j]
        if not xs:
            continue
        x = xs[0] if len(xs) == 1 else jnp.concatenate(xs, axis=free_axis)
        t = lax.dot_general(x, y, dims, preferred_element_type=F32)
        for i in range(len(xs)):
            blk = t[i * m:(i + 1) * m]
            acc = blk if acc is None else acc + blk
    return acc


def _softplus(x):
    return jnp.maximum(x, 0.0) + jnp.log(1.0 + jnp.exp(-jnp.abs(x)))


def _sigmoid(x):
    return 1.0 / (1.0 + jnp.exp(-x))


def _silu(x):
    return x * _sigmoid(x)


def _iota(shape, dim):
    return lax.broadcasted_iota(jnp.int32, shape, dim)


def _idiv(x, n):
    assert n & (n - 1) == 0
    return jnp.right_shift(x, n.bit_length() - 1)


def _stack_heads(x, head_w, n_heads):
    hid = _idiv(_iota(x.shape, 1), head_w)
    return jnp.concatenate([jnp.where(hid == g, x, 0.0) for g in range(n_heads)], axis=0)


def _unstack_heads(xs, c, n_heads):
    out = xs[0:c]
    for g in range(1, n_heads):
        out = out + xs[g * c:(g + 1) * c]
    return out


def _inproj_kernel(x_ref, nw_ref, w_ref, o_ref, h_ref):
    @pl.when(pl.program_id(1) == 0)
    def _():
        x = x_ref[...]
        ms = jnp.mean(x * x, axis=-1, keepdims=True)
        h_ref[...] = (x * lax.rsqrt(ms + _EPS) * nw_ref[...]).astype(BF16)

    o_ref[...] = jnp.dot(h_ref[...], w_ref[...], preferred_element_type=F32)


def _in_proj(x2, nw, w_bf16, tm=1024, tn=1024):
    t, d = x2.shape
    n = w_bf16.shape[1]
    return pl.pallas_call(
        _inproj_kernel,
        out_shape=jax.ShapeDtypeStruct((t, n), F32),
        grid=(t // tm, n // tn),
        in_specs=[
            pl.BlockSpec((tm, d), lambda i, j: (i, 0)),
            pl.BlockSpec((1, d), lambda i, j: (0, 0)),
            pl.BlockSpec((d, tn), lambda i, j: (0, j)),
        ],
        out_specs=pl.BlockSpec((tm, tn), lambda i, j: (i, j)),
        scratch_shapes=[pltpu.VMEM((tm, d), BF16)],
        compiler_params=pltpu.CompilerParams(
            dimension_semantics=("parallel", "arbitrary"),
            vmem_limit_bytes=_VMEM_LIMIT),
        name="in_proj",
    )(x2, nw, w_bf16)


def _token_shift(x, prev_ref, mu):
    c = x.shape[0]
    rows = _iota(x.shape, 0)
    xp = jnp.where(rows == 0, prev_ref[7:8, :], pltpu.roll(x, 1, 0))
    prev_ref[...] = x[c - 8:c, :]
    return x + mu * (xp - x)


def _head_sums(x, head_ones):
    c = x.shape[0]
    ng = x.shape[1] // _GRP
    rows = [p[:, g * _GRP:(g + 1) * _GRP] for p in _parts(x, _P_HEADSUM) for g in range(ng)]
    t = lax.dot_general(jnp.concatenate(rows, axis=0), head_ones, _NN, preferred_element_type=F32)
    cols = []
    for g in range(ng):
        acc = t[g * c:(g + 1) * c]
        for i in range(1, _P_HEADSUM):
            acc = acc + t[(i * ng + g) * c:(i * ng + g + 1) * c]
        cols.append(acc)
    return jnp.concatenate(cols, axis=1)


def _stack_lanes(x, lane_masks):
    return jnp.concatenate([x * m for m in lane_masks], axis=0)


def _rwkv_groups(at, rt, bt, kt, bh, kh, v, w_end, st_ref, strict, incl, eye_c, eye, same):
    c = _C
    nh = _GRP // _RN
    gs = range(at.shape[1] // _GRP)
    lane_head = _idiv(_iota((1, _GRP), 1), _RN)
    hm = [jnp.where(lane_head == h, 1.0, 0.0).astype(BF16) for h in range(nh)]

    def grp(x, g):
        return x[:, g * _GRP:(g + 1) * _GRP]

    def bf(x):
        return x.astype(BF16)

    def dot(a, b, dims=_NN):
        return lax.dot_general(a, b, dims, preferred_element_type=F32)

    def stack(x):
        return _stack_lanes(bf(x), hm)

    at_b = [bf(grp(at, g)) for g in gs]
    v_b = [bf(grp(v, g)) for g in gs]
    bh_b = [bf(grp(bh, g)) for g in gs]
    lhs = [jnp.concatenate([at_b[g], bf(grp(rt, g))], axis=0) for g in gs]
    rhs = [jnp.concatenate([stack(grp(bt, g)), stack(grp(kt, g))], axis=0) for g in gs]
    a_all = [dot(lhs[g], rhs[g], _NT) for g in gs]
    a_ab = [a_all[g][:c, :nh * c] * strict for g in gs]
    a_ak = [bf(a_all[g][:c, nh * c:] * strict) for g in gs]
    a_rb = [bf(a_all[g][c:, :nh * c] * incl) for g in gs]
    a_rk = [bf(a_all[g][c:, nh * c:] * incl) for g in gs]

    x = [eye_c + a_ab[g] for g in gs]
    p_b = [bf(a_ab[g]) for g in gs]
    p = [dot(p_b[g], _stack_lanes(p_b[g], hm)) for g in gs]
    n_sq = c.bit_length() - 2
    for it in range(n_sq):
        p_bd = [stack(p[g]) for g in gs]
        if it + 1 < n_sq:
            xp = [dot(jnp.concatenate([bf(x[g]), bf(p[g])], axis=0), p_bd[g]) for g in gs]
            x = [x[g] + xp[g][:c] for g in gs]
            p = [xp[g][c:] for g in gs]
        else:
            x = [x[g] + dot(bf(x[g]), p_bd[g]) for g in gs]
    t = [bf(x[g]) for g in gs]

    v_s = [_stack_lanes(v_b[g], hm) for g in gs]
    abar = [dot(t[g], _stack_lanes(at_b[g], hm)) for g in gs]
    uv = [dot(a_ak[g], v_s[g]) for g in gs]
    ubar = [dot(t[g], stack(uv[g])) for g in gs]
    abar_b = [bf(abar[g]) for g in gs]
    ubar_b = [bf(ubar[g]) for g in gs]
    rbar = [grp(rt, g) + dot(a_rb[g], _stack_lanes(abar_b[g], hm)) for g in gs]
    ybar = [dot(a_rb[g], _stack_lanes(ubar_b[g], hm)) + dot(a_rk[g], v_s[g]) for g in gs]
    mt = [eye * grp(w_end, g) + same * dot(bh_b[g], abar_b[g], _TN) for g in gs]
    dt = [same * dot(jnp.concatenate([bh_b[g], bf(grp(kh, g))], axis=0),
                     jnp.concatenate([ubar_b[g], v_b[g]], axis=0), _TN) for g in gs]

    stp = [_parts(st_ref[g], _P_STATE) for g in gs]
    y = [_mm(rbar[g], stp[g], na=_P_STATE) + ybar[g] for g in gs]
    for g in gs:
        st_ref[g] = _mm(mt[g], stp[g], na=_P_STATE, nb=_P_STATE) + dt[g]
    return jnp.concatenate(y, axis=1)


def _rwkv_kernel(pr_ref, pk_ref, pv_ref, pg_ref, pwa_ref,
                 mur_ref, muk_ref, muv_ref, muwa_ref,
                 w0_ref, w2_ref, a0_ref, a2_ref, kk_ref, ka_ref, rk_ref, lnw_ref, lnb_ref,
                 strict_ref, incl_ref, eyec_ref, eye_ref, same_ref, hones_ref, tri_ref,
                 o_ref,
                 s_ref, prev_r, prev_k, prev_v, prev_wa):
    c = _C
    nb = pr_ref.shape[0]

    @pl.when(pl.program_id(1) == 0)
    def _():
        s_ref[...] = jnp.zeros_like(s_ref)
        prev_r[...] = jnp.zeros_like(prev_r)
        prev_k[...] = jnp.zeros_like(prev_k)
        prev_v[...] = jnp.zeros_like(prev_v)
        prev_wa[...] = jnp.zeros_like(prev_wa)

    def lanes(ref):
        return jnp.concatenate([ref[b] for b in range(nb)], axis=1)

    def rep(ref):
        return jnp.concatenate([ref[...]] * nb, axis=1)

    def unrows(x):
        return jnp.concatenate([x[b * c:(b + 1) * c] for b in range(nb)], axis=1)

    r = _token_shift(lanes(pr_ref), prev_r, rep(mur_ref))
    k = _token_shift(lanes(pk_ref), prev_k, rep(muk_ref))
    v = _token_shift(lanes(pv_ref), prev_v, rep(muv_ref))
    wa = _token_shift(lanes(pwa_ref), prev_wa, rep(muwa_ref))
    wa = jnp.concatenate([wa[:, b * 128:(b + 1) * 128] for b in range(nb)], axis=0)
    head_ones = hones_ref[...]

    wdec = rep(w0_ref) + unrows(_mm(jnp.tanh(wa), w2_ref[...], na=2))
    wdec = -_softplus(-wdec) - 0.5
    lw = -jnp.exp(wdec)
    a_lr = _sigmoid(rep(a0_ref) + unrows(_mm(wa, a2_ref[...], na=2)))

    kk = k * rep(kk_ref)
    kk = kk * lax.rsqrt(jnp.maximum(_head_sums(kk * kk, head_ones), 1e-24))
    k2 = k * (1.0 + (a_lr - 1.0) * rep(ka_ref))
    bb = kk * a_lr

    cl = _mm(tri_ref[...], lw, nb=_P_SUM)
    cl_end = cl[c - 1:c, :]
    e_in = jnp.exp(cl)
    e_neg = jnp.exp(-cl)
    e_end = jnp.exp(cl_end - cl)
    w_end = jnp.exp(cl_end)
    at = -kk * jnp.exp(cl - lw)
    rt = r * e_in
    bt = bb * e_neg
    kt = k2 * e_neg
    bh = bb * e_end
    kh = k2 * e_end

    y = _rwkv_groups(at, rt, bt, kt, bh, kh, v, w_end, s_ref,
                     strict_ref[...], incl_ref[...], eyec_ref[...], eye_ref[...], same_ref[...])

    inv_n = 1.0 / _RN
    mean = _head_sums(y, head_ones) * inv_n
    yc = y - mean
    var = _head_sums(yc * yc, head_ones) * inv_n
    yn = yc * lax.rsqrt(var + _GN_EPS) * rep(lnw_ref) + rep(lnb_ref)
    bonus = _head_sums(r * k2 * rep(rk_ref), head_ones) * v
    out = yn + bonus
    for b in range(nb):
        o_ref[b] = (out[:, b * _RW:(b + 1) * _RW] * _silu(pg_ref[b])).astype(o_ref.dtype)


def _rwkv_consts():
    n = _GRP
    t = lax.broadcasted_iota(jnp.int32, (_C, n // _RN * _C), 0)
    s = lax.broadcasted_iota(jnp.int32, (_C, n // _RN * _C), 1) % _C
    strict = (s < t).astype(F32)
    incl = (s <= t).astype(F32)
    eye_c = (s == t).astype(F32)
    ri = lax.broadcasted_iota(jnp.int32, (n, n), 0)
    ci = lax.broadcasted_iota(jnp.int32, (n, n), 1)
    eye = (ri == ci).astype(F32)
    same = (ri // _RN) == (ci // _RN)
    ti = lax.broadcasted_iota(jnp.int32, (_C, _C), 0)
    tj = lax.broadcasted_iota(jnp.int32, (_C, _C), 1)
    tri = (ti >= tj).astype(BF16)
    return strict, incl, eye_c, eye, same.astype(F32), same.astype(BF16), tri


def _rwkv_mix(p3, mu_r, mu_k, mu_v, mu_wa, w0, w2p, a0, a2p, k_k, k_a, r_k, ln_w, ln_b, bb=4):
    b, s, _ = p3.shape
    nc = s // _C
    bb = min(bb, b)
    assert b % bb == 0
    ng = bb * _RW // _GRP

    def pspec(off):
        return pl.BlockSpec((bb, _C, _RW), lambda bi, c, o=off // _RW: (bi, c, o))

    def full(shape):
        return pl.BlockSpec(shape, lambda bi, c: (0,) * len(shape))

    vec = full((1, _RW))
    lora = full((2 * _RLORA, _RW))
    sq = full((_GRP, _GRP))
    cat = full((_C, _GRP // _RN * _C))
    return pl.pallas_call(
        _rwkv_kernel,
        out_shape=jax.ShapeDtypeStruct((b, s, _RW), BF16),
        grid=(b // bb, nc),
        in_specs=[
            pspec(_O_R), pspec(_O_K), pspec(_O_V), pspec(_O_GA),
            pl.BlockSpec((bb, _C, 128), lambda bi, c: (bi, c, _O_WA // 128)),
            vec, vec, vec, full((1, 2 * _RLORA)),
            vec, lora, vec, lora, vec, vec, vec, vec, vec,
            cat, cat, cat, sq, sq, sq, full((_C, _C)),
        ],
        out_specs=pl.BlockSpec((bb, _C, _RW), lambda bi, c: (bi, c, 0)),
        scratch_shapes=[
            pltpu.VMEM((ng, _GRP, _GRP), F32),
            pltpu.VMEM((8, bb * _RW), F32), pltpu.VMEM((8, bb * _RW), F32),
            pltpu.VMEM((8, bb * _RW), F32), pltpu.VMEM((8, bb * 2 * _RLORA), F32),
        ],
        compiler_params=pltpu.CompilerParams(
            dimension_semantics=("parallel", "arbitrary"),
            vmem_limit_bytes=_VMEM_LIMIT),
        name="rwkv_mix",
    )(p3, p3, p3, p3, p3, mu_r, mu_k, mu_v, mu_wa, w0, w2p, a0, a2p, k_k, k_a, r_k, ln_w, ln_b,
      *_rwkv_consts())


def _moba_kernel(q_ref, k_ref, v_ref, g_ref, o_ref, kmean_ref, m_ref, l_ref, acc_ref):
    i = pl.program_id(1)
    blk = _MBLK
    nb = k_ref.shape[0] // blk
    nh = _MW // _MD
    hs = range(nh)
    scale = _MD ** -0.5
    neg = -1e30

    def hsl(h):
        return slice(h * _MD, (h + 1) * _MD)

    @pl.when(i == 0)
    def _():
        kmean_ref[...] = jnp.zeros_like(kmean_ref)
        for j in range(nb):
            kmean_ref[j:j + 1, :] = jnp.mean(k_ref[j * blk:(j + 1) * blk, :], axis=0, keepdims=True)

    npad = kmean_ref.shape[0]
    bid = _iota((npad, blk), 0)
    q = [q_ref[:, hsl(h)] for h in hs]
    gate = [jnp.where(bid < i, _mm(kmean_ref[:, hsl(h)], q[h], _NT, na=3, nb=3), -jnp.inf)
            for h in hs]
    qs = [(q[h] * scale).astype(BF16) for h in hs]
    kpos = _iota((blk, blk), 0)
    qpos = _iota((blk, blk), 1)

    start = pl.multiple_of(i * blk, blk)
    s_t = [lax.dot_general(k_ref[pl.ds(start, blk), hsl(h)].astype(BF16), qs[h], _NT,
                           preferred_element_type=F32) for h in hs]
    s_t = [jnp.where(kpos <= qpos, s_t[h], neg) for h in hs]
    m0 = [jnp.max(s_t[h], axis=0, keepdims=True) for h in hs]
    p_t = [jnp.exp(s_t[h] - m0[h]) for h in hs]
    for h in hs:
        m_ref[h] = m0[h]
        l_ref[h] = jnp.sum(p_t[h], axis=0, keepdims=True)
        acc_ref[h] = lax.dot_general(v_ref[pl.ds(start, blk), hsl(h)].astype(BF16), p_t[h].astype(BF16), _TN,
                                     preferred_element_type=F32)

    for j in range(nb - 1):
        @pl.when(j < i)
        def _(j=j):
            rows = slice(j * blk, (j + 1) * blk)
            s_j = [lax.dot_general(k_ref[rows, hsl(h)].astype(BF16), qs[h], _NT,
                                   preferred_element_type=F32) for h in hs]
            p_j, alpha = [], []
            for h in hs:
                gj = gate[h][j:j + 1, :]
                ahead = (gate[h] > gj) | ((gate[h] == gj) & (bid < j))
                rank = jnp.sum(jnp.where(ahead, 1.0, 0.0), axis=0, keepdims=True)
                s_h = jnp.where(rank < float(_MTOPK), s_j[h], neg)
                m_old = m_ref[h]
                m_new = jnp.maximum(m_old, jnp.max(s_h, axis=0, keepdims=True))
                alpha.append(jnp.exp(m_old - m_new))
                p_j.append(jnp.exp(s_h - m_new))
                m_ref[h] = m_new
            for h in hs:
                l_ref[h] = alpha[h] * l_ref[h] + jnp.sum(p_j[h], axis=0, keepdims=True)
                acc_ref[h] = alpha[h] * acc_ref[h] + lax.dot_general(
                    v_ref[rows, hsl(h)].astype(BF16), p_j[h].astype(BF16), _TN, preferred_element_type=F32)

    o = jnp.concatenate([(acc_ref[h] / l_ref[h]).T for h in hs], axis=1)
    o_ref[...] = (o * _silu(g_ref[...])).astype(o_ref.dtype)


def _moba_mix(p3):
    b, s, _ = p3.shape
    nh = _MW // _MD
    nq = s // _MBLK
    npad = -(-nq // 16) * 16
    return pl.pallas_call(
        _moba_kernel,
        out_shape=jax.ShapeDtypeStruct((b, s, _MW), BF16),
        grid=(b, nq),
        in_specs=[
            pl.BlockSpec((None, _MBLK, _MW), lambda bi, i: (bi, i, _O_MQ // _MW)),
            pl.BlockSpec((None, s, _MW), lambda bi, i: (bi, 0, _O_MK // _MW)),
            pl.BlockSpec((None, s, _MW), lambda bi, i: (bi, 0, _O_MV // _MW)),
            pl.BlockSpec((None, _MBLK, _MW), lambda bi, i: (bi, i, _O_MG // _MW)),
        ],
        out_specs=pl.BlockSpec((None, _MBLK, _MW), lambda bi, i: (bi, i, 0)),
        scratch_shapes=[
            pltpu.VMEM((npad, _MW), F32),
            pltpu.VMEM((nh, 1, _MBLK), F32), pltpu.VMEM((nh, 1, _MBLK), F32),
            pltpu.VMEM((nh, _MD, _MBLK), F32),
        ],
        compiler_params=pltpu.CompilerParams(
            dimension_semantics=("parallel", "arbitrary"),
            vmem_limit_bytes=_VMEM_LIMIT),
        name="moba_mix",
    )(p3, p3, p3, p3)


def _gla_kernel(q_ref, k_ref, v_ref, g_ref, low_ref, aup_ref, ab_ref, nw_ref, incl_ref, tri_ref,
                o_ref, st_ref):
    c = _C
    nh = _GKW // _GDK
    bs = range(q_ref.shape[0])

    @pl.when(pl.program_id(1) == 0)
    def _():
        st_ref[...] = jnp.zeros_like(st_ref)

    incl, tri = incl_ref[...], tri_ref[...]
    z = [_mm(low_ref[b], aup_ref[...], na=2, nb=2) + ab_ref[...] for b in bs]
    la = [-_softplus(-z[b]) * (1.0 / _GTEMP) for b in bs]
    bcum = [_mm(tri, la[b], nb=_P_SUM) for b in bs]
    b_last = [bcum[b][c - 1:c, :] for b in bs]
    k = [k_ref[b] for b in bs]
    qe = [_parts(_stack_heads(q_ref[b] * (_GDK ** -0.5) * jnp.exp(bcum[b]), _GDK, nh), 1) for b in bs]
    ke = [_stack_heads(k[b] * jnp.exp(-bcum[b]), _GDK, nh) for b in bs]
    kd = [_stack_heads(k[b] * jnp.exp(b_last[b] - bcum[b]), _GDK, nh) for b in bs]
    v_s = [_parts(_stack_heads(v_ref[b], _GDV, nh), 1) for b in bs]

    att = [_mm(qe[b], ke[b], _NT) * incl for b in bs]
    st = [st_ref[b] for b in bs]
    o_s = [_mm(att[b], v_s[b]) + _mm(qe[b], st[b], _NT) for b in bs]
    for b in bs:
        st_ref[b] = st[b] * jnp.exp(b_last[b]) + _mm(v_s[b], kd[b], _TN)

    for b in bs:
        o = _unstack_heads(o_s[b], c, nh)
        outs = []
        for h in range(nh):
            oh = o[:, h * _GDV:(h + 1) * _GDV]
            ms = jnp.mean(oh * oh, axis=-1, keepdims=True)
            outs.append(oh * lax.rsqrt(ms + _EPS) * nw_ref[...])
        o_ref[b] = (jnp.concatenate(outs, axis=1) * _silu(g_ref[b])).astype(o_ref.dtype)


def _gla_mix(p3, a_up_p, a_b, norm_w, bb=4):
    b, s, _ = p3.shape
    nc = s // _C
    bb = min(bb, b)
    assert b % bb == 0
    n = _GKW // _GDK * _C
    ri = lax.broadcasted_iota(jnp.int32, (n, n), 0)
    ci = lax.broadcasted_iota(jnp.int32, (n, n), 1)
    incl = (((ri // _C) == (ci // _C)) & (ci <= ri)).astype(F32)
    tri = (lax.broadcasted_iota(jnp.int32, (_C, _C), 0) >= lax.broadcasted_iota(jnp.int32, (_C, _C), 1)).astype(BF16)

    def full(shape):
        return pl.BlockSpec(shape, lambda bi, c: (0,) * len(shape))

    return pl.pallas_call(
        _gla_kernel,
        out_shape=jax.ShapeDtypeStruct((b, s, _GW), BF16),
        grid=(b // bb, nc),
        in_specs=[
            pl.BlockSpec((bb, _C, _GKW), lambda bi, c: (bi, c, _O_GQ // _GKW)),
            pl.BlockSpec((bb, _C, _GKW), lambda bi, c: (bi, c, _O_GK // _GKW)),
            pl.BlockSpec((bb, _C, _GW), lambda bi, c: (bi, c, _O_GV // _GW)),
            pl.BlockSpec((bb, _C, _GW), lambda bi, c: (bi, c, _O_GG // _GW)),
            pl.BlockSpec((bb, _C, 128), lambda bi, c: (bi, c, _O_GL // 128)),
            full((128, _GKW)), full((1, _GKW)), full((1, _GDV)), full((n, n)), full((_C, _C)),
        ],
        out_specs=pl.BlockSpec((bb, _C, _GW), lambda bi, c: (bi, c, 0)),
        scratch_shapes=[pltpu.VMEM((bb, _GW, _GKW), F32)],
        compiler_params=pltpu.CompilerParams(
            dimension_semantics=("parallel", "arbitrary"),
            vmem_limit_bytes=_VMEM_LIMIT),
        name="gla_mix",
    )(p3, p3, p3, p3, p3, a_up_p, a_b, norm_w, incl, tri)


def _outproj_kernel(x_ref, ya_ref, yb_ref, yc_ref, w_ref, fw_ref, o_ref, *, final_norm):
    acc = x_ref[...]
    acc = acc + jnp.dot(ya_ref[...], w_ref[0:_RW, :], preferred_element_type=F32)
    acc = acc + jnp.dot(yb_ref[...], w_ref[_RW:_RW + _MW, :], preferred_element_type=F32)
    acc = acc + jnp.dot(yc_ref[...], w_ref[_RW + _MW:, :], preferred_element_type=F32)
    if final_norm:
        ms = jnp.mean(acc * acc, axis=-1, keepdims=True)
        acc = acc * lax.rsqrt(ms + _EPS) * fw_ref[...]
    o_ref[...] = acc


def _out_proj(x2, ya, yb, yc, w_bf16, fw, final_norm, tm=512):
    t, d = x2.shape
    return pl.pallas_call(
        functools.partial(_outproj_kernel, final_norm=final_norm),
        out_shape=jax.ShapeDtypeStruct((t, d), F32),
        grid=(t // tm,),
        in_specs=[
            pl.BlockSpec((tm, d), lambda i: (i, 0)),
            pl.BlockSpec((tm, _RW), lambda i: (i, 0)),
            pl.BlockSpec((tm, _MW), lambda i: (i, 0)),
            pl.BlockSpec((tm, _GW), lambda i: (i, 0)),
            pl.BlockSpec((d, d), lambda i: (0, 0)),
            pl.BlockSpec((1, d), lambda i: (0, 0)),
        ],
        out_specs=pl.BlockSpec((tm, d), lambda i: (i, 0)),
        compiler_params=pltpu.CompilerParams(
            dimension_semantics=("parallel",),
            vmem_limit_bytes=_VMEM_LIMIT),
        name="out_proj",
    )(x2, ya, yb, yc, w_bf16, fw)


def _split_cols(w):
    sizes = (_RW, _RW, _RW, _RLORA, _RLORA, _RW, _MW, _MW, _MW, _MW, _GKW, _GKW, _GW, _GW, _GRANK)
    offs, o = [], 0
    for sz in sizes:
        offs.append((o, o + sz))
        o += sz
    return [w[..., a:b] for a, b in offs]


def _relayout_cols(w):
    r, k, v, wd, ad, ga, mq, mk, mv, mg, gq, gk, gv, gg, gl = _split_cols(w)
    pad = jnp.zeros(w.shape[:-1] + (_NP - _O_GL - _GRANK,), w.dtype)
    return jnp.concatenate([gv, gg, r, k, v, ga, gq, gk, mq, mk, mv, mg, wd, ad, gl, pad], axis=-1)


def kernel(x, norm_w, w_in, w_out, rwkv_mu, rwkv_w0, rwkv_w2, rwkv_a0, rwkv_a2, rwkv_k_k, rwkv_k_a,
           rwkv_r_k, rwkv_ln_w, rwkv_ln_b, gla_a_up, gla_a_b, gla_norm_w, final_norm_w):
    b, s, d = x.shape
    depth = norm_w.shape[0]
    x2 = x.reshape(b * s, d)
    zero_lora = jnp.zeros((_RLORA, _RW), F32)
    fw = final_norm_w.reshape(1, d)
    for l in range(depth):
        w_in_l = _relayout_cols(w_in[l].astype(BF16))
        w_out_l = w_out[l].astype(BF16)
        mu = rwkv_mu[l]
        mu_r, mu_k, mu_v = (mu[i * _RW:(i + 1) * _RW].reshape(1, _RW) for i in range(3))
        mu_wa = mu[3 * _RW:].reshape(1, 2 * _RLORA)
        w2p = jnp.concatenate([rwkv_w2[l], zero_lora], axis=0)
        a2p = jnp.concatenate([zero_lora, rwkv_a2[l]], axis=0)
        a_up_p = jnp.concatenate([gla_a_up[l], jnp.zeros((128 - _GRANK, _GKW), F32)], axis=0)

        p = _in_proj(x2, norm_w[l].reshape(1, d), w_in_l)
        p3 = p.reshape(b, s, _NP)
        row = lambda a: a.reshape(1, _RW)
        y_a = _rwkv_mix(p3, mu_r, mu_k, mu_v, mu_wa, row(rwkv_w0[l]), w2p, row(rwkv_a0[l]), a2p,
                        row(rwkv_k_k[l]), row(rwkv_k_a[l]), row(rwkv_r_k[l]),
                        row(rwkv_ln_w[l]), row(rwkv_ln_b[l]))
        y_b = _moba_mix(p3)
        y_c = _gla_mix(p3, a_up_p, gla_a_b[l].reshape(1, _GKW), gla_norm_w[l].reshape(1, _GDV))
        x2 = _out_proj(x2, y_a.reshape(b * s, _RW), y_b.reshape(b * s, _MW), y_c.reshape(b * s, _GW),
                       w_out_l, fw, final_norm=(l == depth - 1))
    return x2.reshape(b, s, d)
```

```python
import functools

import jax
import jax.numpy as jnp
from jax import lax
from jax.experimental import pallas as pl
from jax.experimental.pallas import tpu as pltpu

F32 = jnp.float32
BF16 = jnp.bfloat16

_D = 2048
_RW = 1024
_RN = 64
_RLORA = 64
_GN_EPS = 64e-5
_MW = 512
_MD = 128
_MBLK = 256
_MTOPK = 3
_GW = 512
_GKW = 256
_GDK = 64
_GDV = 128
_GRANK = 16
_GTEMP = 16.0
_EPS = 1e-6
_C = 64
_GRP = 256

_D_IN = 7824
_RUNS = ((0, 3072, 0),
         (3200, 4224, 3072),
         (4224, 6272, 4096),
         (6784, 7808, 6144),
         (6272, 6784, 7168),
         (3072, 3200, 7680),
         (7808, 7824, 7808))
_O_R, _O_K, _O_V, _O_GA = 0, 1024, 2048, 3072
_O_MQ, _O_MK, _O_MV, _O_MG = 4096, 4608, 5120, 5632
_O_GV, _O_GG = 6144, 6656
_O_GQ, _O_GK = 7168, 7424
_O_WA, _O_GL = 7680, 7808
_NP = 8192
assert all(o % _GW == 0 for o in (_O_GV, _O_GG))
assert all(o % _RW == 0 for o in (_O_R, _O_K, _O_V, _O_GA))
assert all(o % _GKW == 0 for o in (_O_GQ, _O_GK))
assert all(o % _MW == 0 for o in (_O_MQ, _O_MK, _O_MV, _O_MG))
assert all(o % 128 == 0 for o in (_O_WA, _O_GL))

_NN = (((1,), (0,)), ((), ()))
_NT = (((1,), (1,)), ((), ()))
_TN = (((0,), (0,)), ((), ()))

_VMEM_LIMIT = 56 * 1024 * 1024

_P_INV = 1
_P_MAIN = 1
_P_STATE = 1
_P_SUM = 3
_P_HEADSUM = 2


def _parts(x, n):
    if x.dtype == BF16:
        return [x]
    out, rem = [], x
    for i in range(n):
        p = rem.astype(BF16)
        out.append(p)
        if i + 1 < n:
            rem = rem - p.astype(F32)
    return out


def _mm(a, b, dims=_NN, na=1, nb=1):
    pa = a if isinstance(a, list) else _parts(a, na)
    pb = b if isinstance(b, list) else _parts(b, nb)
    order = max(len(pa), len(pb))
    free_axis = 1 if dims[0][0] == (0,) else 0
    m = pa[0].shape[free_axis]
    acc = None
    for j, y in enumerate(pb):
        xs = pa[:order - j]
        if not xs:
            continue
        x = xs[0] if len(xs) == 1 else jnp.concatenate(xs, axis=free_axis)
        t = lax.dot_general(x, y, dims, preferred_element_type=F32)
        for i in range(len(xs)):
            blk = t[i * m:(i + 1) * m]
            acc = blk if acc is None else acc + blk
    return acc


def _softplus(x):
    return jnp.maximum(x, 0.0) + jnp.log(1.0 + jnp.exp(-jnp.abs(x)))


def _sigmoid(x):
    return 1.0 / (1.0 + jnp.exp(-x))


def _silu(x):
    return x * _sigmoid(x)


def _iota(shape, dim):
    return lax.broadcasted_iota(jnp.int32, shape, dim)


def _idiv(x, n):
    assert n & (n - 1) == 0
    return jnp.right_shift(x, n.bit_length() - 1)


def _stack_heads(x, head_w, n_heads):
    hid = _idiv(_iota(x.shape, 1), head_w)
    return jnp.concatenate([jnp.where(hid == g, x, 0.0) for g in range(n_heads)], axis=0)


def _unstack_heads(xs, c, n_heads):
    out = xs[0:c]
    for g in range(1, n_heads):
        out = out + xs[g * c:(g + 1) * c]
    return out


def _inproj_kernel(x_ref, nw_ref, w_ref, o_ref, h_ref):
    @pl.when(pl.program_id(1) == 0)
    def _():
        x = x_ref[...]
        ms = jnp.mean(x * x, axis=-1, keepdims=True)
        h_ref[...] = (x * lax.rsqrt(ms + _EPS) * nw_ref[...]).astype(BF16)

    o_ref[...] = jnp.dot(h_ref[...], w_ref[...], preferred_element_type=F32)


def _in_proj(x2, nw, w_bf16, layer, tm=1024, tn=1024):
    t, d = x2.shape
    n = w_bf16.shape[2]
    return pl.pallas_call(
        _inproj_kernel,
        out_shape=jax.ShapeDtypeStruct((t, n), F32),
        grid=(t // tm, n // tn),
        in_specs=[
            pl.BlockSpec((tm, d), lambda i, j: (i, 0)),
            pl.BlockSpec((1, d), lambda i, j: (0, 0)),
            pl.BlockSpec((None, d, tn), lambda i, j: (layer, 0, j)),
        ],
        out_specs=pl.BlockSpec((tm, tn), lambda i, j: (i, j)),
        scratch_shapes=[pltpu.VMEM((tm, d), BF16)],
        compiler_params=pltpu.CompilerParams(
            dimension_semantics=("parallel", "arbitrary"),
            vmem_limit_bytes=_VMEM_LIMIT),
        name="in_proj",
    )(x2, nw, w_bf16)


def _token_shift(x, prev_ref, mu):
    c = x.shape[0]
    rows = _iota(x.shape, 0)
    xp = jnp.where(rows == 0, prev_ref[7:8, :], pltpu.roll(x, 1, 0))
    prev_ref[...] = x[c - 8:c, :]
    return x + mu * (xp - x)


def _head_sums(x, head_ones):
    c = x.shape[0]
    ng = x.shape[1] // _GRP
    rows = [p[:, g * _GRP:(g + 1) * _GRP] for p in _parts(x, _P_HEADSUM) for g in range(ng)]
    t = lax.dot_general(jnp.concatenate(rows, axis=0), head_ones, _NN, preferred_element_type=F32)
    cols = []
    for g in range(ng):
        acc = t[g * c:(g + 1) * c]
        for i in range(1, _P_HEADSUM):
            acc = acc + t[(i * ng + g) * c:(i * ng + g + 1) * c]
        cols.append(acc)
    return jnp.concatenate(cols, axis=1)


def _stack_lanes(x, lane_masks):
    return jnp.concatenate([x * m for m in lane_masks], axis=0)


def _rwkv_groups(at, rt, bt, kt, bh, kh, v, w_end, st_ref, strict, incl, eye_c, eye, same):
    c = _C
    nh = _GRP // _RN
    gs = range(at.shape[1] // _GRP)
    lane_head = _idiv(_iota((1, _GRP), 1), _RN)
    hm = [jnp.where(lane_head == h, 1.0, 0.0).astype(BF16) for h in range(nh)]

    def grp(x, g):
        return x[:, g * _GRP:(g + 1) * _GRP]

    def bf(x):
        return x.astype(BF16)

    def dot(a, b, dims=_NN):
        return lax.dot_general(a, b, dims, preferred_element_type=F32)

    def stack(x):
        return _stack_lanes(bf(x), hm)

    at_b = [bf(grp(at, g)) for g in gs]
    v_b = [bf(grp(v, g)) for g in gs]
    bh_b = [bf(grp(bh, g)) for g in gs]
    lhs = [jnp.concatenate([at_b[g], bf(grp(rt, g))], axis=0) for g in gs]
    rhs = [jnp.concatenate([stack(grp(bt, g)), stack(grp(kt, g))], axis=0) for g in gs]
    a_all = [dot(lhs[g], rhs[g], _NT) for g in gs]
    a_ab = [a_all[g][:c, :nh * c] * strict for g in gs]
    a_ak = [bf(a_all[g][:c, nh * c:] * strict) for g in gs]
    a_rb = [bf(a_all[g][c:, :nh * c] * incl) for g in gs]
    a_rk = [bf(a_all[g][c:, nh * c:] * incl) for g in gs]

    x = [eye_c + a_ab[g] for g in gs]
    p_b = [bf(a_ab[g]) for g in gs]
    p = [dot(p_b[g], _stack_lanes(p_b[g], hm)) for g in gs]
    n_sq = c.bit_length() - 2
    for it in range(n_sq):
        p_bd = [stack(p[g]) for g in gs]
        if it + 1 < n_sq:
            xp = [dot(jnp.concatenate([bf(x[g]), bf(p[g])], axis=0), p_bd[g]) for g in gs]
            x = [x[g] + xp[g][:c] for g in gs]
            p = [xp[g][c:] for g in gs]
        else:
            x = [x[g] + dot(bf(x[g]), p_bd[g]) for g in gs]
    t = [bf(x[g]) for g in gs]

    v_s = [_stack_lanes(v_b[g], hm) for g in gs]
    abar = [dot(t[g], _stack_lanes(at_b[g], hm)) for g in gs]
    uv = [dot(a_ak[g], v_s[g]) for g in gs]
    ubar = [dot(t[g], stack(uv[g])) for g in gs]
    abar_b = [bf(abar[g]) for g in gs]
    ubar_b = [bf(ubar[g]) for g in gs]
    rbar = [grp(rt, g) + dot(a_rb[g], _stack_lanes(abar_b[g], hm)) for g in gs]
    ybar = [dot(a_rb[g], _stack_lanes(ubar_b[g], hm)) + dot(a_rk[g], v_s[g]) for g in gs]
    mt = [eye * grp(w_end, g) + same * dot(bh_b[g], abar_b[g], _TN) for g in gs]
    dt = [same * dot(jnp.concatenate([bh_b[g], bf(grp(kh, g))], axis=0),
                     jnp.concatenate([ubar_b[g], v_b[g]], axis=0), _TN) for g in gs]

    stp = [_parts(st_ref[g], _P_STATE) for g in gs]
    y = [_mm(rbar[g], stp[g], na=_P_STATE) + ybar[g] for g in gs]
    for g in gs:
        st_ref[g] = _mm(mt[g], stp[g], na=_P_STATE, nb=_P_STATE) + dt[g]
    return jnp.concatenate(y, axis=1)


def _rwkv_kernel(pr_ref, pk_ref, pv_ref, pg_ref, pwa_ref,
                 mur_ref, muk_ref, muv_ref, muwa_ref,
                 w0_ref, w2_ref, a0_ref, a2_ref, kk_ref, ka_ref, rk_ref, lnw_ref, lnb_ref,
                 strict_ref, incl_ref, eyec_ref, eye_ref, same_ref, hones_ref, tri_ref,
                 o_ref,
                 s_ref, prev_r, prev_k, prev_v, prev_wa):
    c = _C
    nb = pr_ref.shape[0]

    @pl.when(pl.program_id(1) == 0)
    def _():
        s_ref[...] = jnp.zeros_like(s_ref)
        prev_r[...] = jnp.zeros_like(prev_r)
        prev_k[...] = jnp.zeros_like(prev_k)
        prev_v[...] = jnp.zeros_like(prev_v)
        prev_wa[...] = jnp.zeros_like(prev_wa)

    head_ones = hones_ref[...]

    def lanes(ref):
        return jnp.concatenate([ref[b] for b in range(nb)], axis=1)

    def rep(ref):
        return jnp.concatenate([ref[...]] * nb, axis=1)

    def unrows(x):
        return jnp.concatenate([x[b * c:(b + 1) * c] for b in range(nb)], axis=1)

    r = _token_shift(lanes(pr_ref), prev_r, rep(mur_ref))
    k = _token_shift(lanes(pk_ref), prev_k, rep(muk_ref))
    v = _token_shift(lanes(pv_ref), prev_v, rep(muv_ref))
    wa = _token_shift(lanes(pwa_ref), prev_wa, rep(muwa_ref))
    wa = jnp.concatenate([wa[:, b * 128:(b + 1) * 128] for b in range(nb)], axis=0)

    wdec = rep(w0_ref) + unrows(_mm(jnp.tanh(wa), w2_ref[...], na=2))
    wdec = -_softplus(-wdec) - 0.5
    lw = -jnp.exp(wdec)
    a_lr = _sigmoid(rep(a0_ref) + unrows(_mm(wa, a2_ref[...], na=2)))
    kk = k * rep(kk_ref)
    kk = kk * lax.rsqrt(jnp.maximum(_head_sums(kk * kk, head_ones), 1e-24))
    k2 = k * (1.0 + (a_lr - 1.0) * rep(ka_ref))
    bb = kk * a_lr
    cl = _mm(tri_ref[...], lw, nb=_P_SUM)
    cl_end = cl[c - 1:c, :]
    e_in = jnp.exp(cl)
    e_neg = jnp.exp(-cl)
    e_end = jnp.exp(cl_end - cl)
    w_end = jnp.exp(cl_end)
    at = -kk * jnp.exp(cl - lw)
    rt = r * e_in
    bt = bb * e_neg
    kt = k2 * e_neg
    bh = bb * e_end
    kh = k2 * e_end

    y = _rwkv_groups(at, rt, bt, kt, bh, kh, v, w_end, s_ref,
                     strict_ref[...], incl_ref[...], eyec_ref[...], eye_ref[...], same_ref[...])

    inv_n = 1.0 / _RN
    mean = _head_sums(y, head_ones) * inv_n
    yc = y - mean
    var = _head_sums(yc * yc, head_ones) * inv_n
    yn = yc * lax.rsqrt(var + _GN_EPS) * rep(lnw_ref) + rep(lnb_ref)
    bonus = _head_sums(r * k2 * rep(rk_ref), head_ones) * v
    out = yn + bonus
    for b in range(nb):
        o_ref[b] = (out[:, b * _RW:(b + 1) * _RW] * _silu(pg_ref[b])).astype(o_ref.dtype)


def _rwkv_consts():
    n = _GRP
    t = lax.broadcasted_iota(jnp.int32, (_C, n // _RN * _C), 0)
    s = lax.broadcasted_iota(jnp.int32, (_C, n // _RN * _C), 1) % _C
    strict = (s < t).astype(F32)
    incl = (s <= t).astype(F32)
    eye_c = (s == t).astype(F32)
    ri = lax.broadcasted_iota(jnp.int32, (n, n), 0)
    ci = lax.broadcasted_iota(jnp.int32, (n, n), 1)
    eye = (ri == ci).astype(F32)
    same = (ri // _RN) == (ci // _RN)
    ti = lax.broadcasted_iota(jnp.int32, (_C, _C), 0)
    tj = lax.broadcasted_iota(jnp.int32, (_C, _C), 1)
    tri = (ti >= tj).astype(BF16)
    return strict, incl, eye_c, eye, same.astype(F32), same.astype(BF16), tri


def _rwkv_mix(p3, mu_r, mu_k, mu_v, mu_wa, w0, w2p, a0, a2p, k_k, k_a, r_k, ln_w, ln_b, bb=4):
    b, s, _ = p3.shape
    nc = s // _C
    bb = min(bb, b)
    assert b % bb == 0
    ng = bb * _RW // _GRP

    def pspec(off):
        return pl.BlockSpec((bb, _C, _RW), lambda bi, c, o=off // _RW: (bi, c, o))

    def full(shape):
        return pl.BlockSpec(shape, lambda bi, c: (0,) * len(shape))

    vec = full((1, _RW))
    lora = full((2 * _RLORA, _RW))
    sq = full((_GRP, _GRP))
    cat = full((_C, _GRP // _RN * _C))
    return pl.pallas_call(
        _rwkv_kernel,
        out_shape=jax.ShapeDtypeStruct((b, s, _RW), BF16),
        grid=(b // bb, nc),
        in_specs=[
            pspec(_O_R), pspec(_O_K), pspec(_O_V), pspec(_O_GA),
            pl.BlockSpec((bb, _C, 128), lambda bi, c: (bi, c, _O_WA // 128)),
            vec, vec, vec, full((1, 2 * _RLORA)),
            vec, lora, vec, lora, vec, vec, vec, vec, vec,
            cat, cat, cat, sq, sq, sq, full((_C, _C)),
        ],
        out_specs=pl.BlockSpec((bb, _C, _RW), lambda bi, c: (bi, c, 0)),
        scratch_shapes=[
            pltpu.VMEM((ng, _GRP, _GRP), F32),
            pltpu.VMEM((8, bb * _RW), F32), pltpu.VMEM((8, bb * _RW), F32),
            pltpu.VMEM((8, bb * _RW), F32), pltpu.VMEM((8, bb * 2 * _RLORA), F32),
        ],
        compiler_params=pltpu.CompilerParams(
            dimension_semantics=("parallel", "arbitrary"),
            vmem_limit_bytes=_VMEM_LIMIT),
        name="rwkv_mix",
    )(p3, p3, p3, p3, p3, mu_r, mu_k, mu_v, mu_wa, w0, w2p, a0, a2p, k_k, k_a, r_k, ln_w, ln_b,
      *_rwkv_consts())


def _moba_block(ii, q_ref, k_ref, v_ref, g_ref, o_ref, kmean_ref):
    blk = _MBLK
    nh = _MW // _MD
    hs = range(nh)
    scale = _MD ** -0.5
    neg = -1e30
    npast = ii * blk

    def hsl(h):
        return slice(h * _MD, (h + 1) * _MD)

    def dot(a, b, dims):
        return lax.dot_general(a, b, dims, preferred_element_type=F32)

    q = [q_ref[:, hsl(h)] for h in hs]
    qs = [(q[h] * scale).astype(BF16) for h in hs]
    own = slice(npast, npast + blk)
    kpos = _iota((blk, blk), 0)
    qpos = _iota((blk, blk), 1)
    s_own = [jnp.where(kpos <= qpos, dot(k_ref[own, hsl(h)].astype(BF16), qs[h], _NT), neg) for h in hs]
    m = [jnp.max(s_own[h], axis=0, keepdims=True) for h in hs]

    if ii > 0:
        npad = kmean_ref.shape[0]
        bid = _iota((npad, blk), 0)
        gate = [jnp.where(bid < ii, _mm(kmean_ref[:, hsl(h)], q[h], _NT, na=3, nb=3), -jnp.inf)
                for h in hs]
        s_past = [dot(k_ref[0:npast, hsl(h)].astype(BF16), qs[h], _NT) for h in hs]
        masked = []
        for h in hs:
            rows = []
            for j in range(ii):
                gj = gate[h][j:j + 1, :]
                ahead = (gate[h] > gj) | ((gate[h] == gj) & (bid < j))
                rank = jnp.sum(jnp.where(ahead, 1.0, 0.0), axis=0, keepdims=True)
                rows.append(jnp.where(rank < float(_MTOPK), s_past[h][j * blk:(j + 1) * blk], neg))
            masked.append(rows[0] if ii == 1 else jnp.concatenate(rows, axis=0))
        m = [jnp.maximum(m[h], jnp.max(masked[h], axis=0, keepdims=True)) for h in hs]
        p_past = [jnp.exp(masked[h] - m[h]) for h in hs]

    p_own = [jnp.exp(s_own[h] - m[h]) for h in hs]
    l = [jnp.sum(p_own[h], axis=0, keepdims=True) for h in hs]
    acc = [dot(v_ref[own, hsl(h)].astype(BF16), p_own[h].astype(BF16), _TN) for h in hs]
    if ii > 0:
        l = [l[h] + jnp.sum(p_past[h], axis=0, keepdims=True) for h in hs]
        acc = [acc[h] + dot(v_ref[0:npast, hsl(h)].astype(BF16), p_past[h].astype(BF16), _TN) for h in hs]

    o = jnp.concatenate([(acc[h] / l[h]).T for h in hs], axis=1)
    o_ref[...] = (o * _silu(g_ref[...])).astype(o_ref.dtype)


def _moba_kernel(q_ref, k_ref, v_ref, g_ref, o_ref, kmean_ref):
    i = pl.program_id(1)
    blk = _MBLK
    nb = k_ref.shape[0] // blk

    @pl.when(i == 0)
    def _():
        kmean_ref[...] = jnp.zeros_like(kmean_ref)
        for j in range(nb):
            kmean_ref[j:j + 1, :] = jnp.mean(k_ref[j * blk:(j + 1) * blk, :], axis=0, keepdims=True)

    for ii in range(nb):
        pl.when(i == ii)(functools.partial(_moba_block, ii, q_ref, k_ref, v_ref, g_ref, o_ref, kmean_ref))


def _moba_mix(p3):
    b, s, _ = p3.shape
    nq = s // _MBLK
    npad = -(-nq // 16) * 16
    return pl.pallas_call(
        _moba_kernel,
        out_shape=jax.ShapeDtypeStruct((b, s, _MW), BF16),
        grid=(b, nq),
        in_specs=[
            pl.BlockSpec((None, _MBLK, _MW), lambda bi, i: (bi, i, _O_MQ // _MW)),
            pl.BlockSpec((None, s, _MW), lambda bi, i: (bi, 0, _O_MK // _MW)),
            pl.BlockSpec((None, s, _MW), lambda bi, i: (bi, 0, _O_MV // _MW)),
            pl.BlockSpec((None, _MBLK, _MW), lambda bi, i: (bi, i, _O_MG // _MW)),
        ],
        out_specs=pl.BlockSpec((None, _MBLK, _MW), lambda bi, i: (bi, i, 0)),
        scratch_shapes=[pltpu.VMEM((npad, _MW), F32)],
        compiler_params=pltpu.CompilerParams(
            dimension_semantics=("parallel", "arbitrary"),
            vmem_limit_bytes=_VMEM_LIMIT),
        name="moba_mix",
    )(p3, p3, p3, p3)


def _gla_kernel(q_ref, k_ref, v_ref, g_ref, low_ref, aup_ref, ab_ref, nw_ref, incl_ref, tri_ref,
                o_ref, st_ref):
    c = _C
    nh = _GKW // _GDK
    bs = range(q_ref.shape[0])

    @pl.when(pl.program_id(1) == 0)
    def _():
        st_ref[...] = jnp.zeros_like(st_ref)

    incl, tri = incl_ref[...], tri_ref[...]
    z = [_mm(low_ref[b], aup_ref[...], na=2, nb=2) + ab_ref[...] for b in bs]
    la = [-_softplus(-z[b]) * (1.0 / _GTEMP) for b in bs]
    bcum = [_mm(tri, la[b], nb=_P_SUM) for b in bs]
    b_last = [bcum[b][c - 1:c, :] for b in bs]
    k = [k_ref[b] for b in bs]
    qe = [_parts(_stack_heads(q_ref[b] * (_GDK ** -0.5) * jnp.exp(bcum[b]), _GDK, nh), 1) for b in bs]
    ke = [_stack_heads(k[b] * jnp.exp(-bcum[b]), _GDK, nh) for b in bs]
    kd = [_stack_heads(k[b] * jnp.exp(b_last[b] - bcum[b]), _GDK, nh) for b in bs]
    v_s = [_parts(_stack_heads(v_ref[b], _GDV, nh), 1) for b in bs]

    att = [_mm(qe[b], ke[b], _NT) * incl for b in bs]
    st = [st_ref[b] for b in bs]
    o_s = [_mm(att[b], v_s[b]) + _mm(qe[b], st[b], _NT) for b in bs]
    for b in bs:
        st_ref[b] = st[b] * jnp.exp(b_last[b]) + _mm(v_s[b], kd[b], _TN)

    for b in bs:
        o = _unstack_heads(o_s[b], c, nh)
        outs = []
        for h in range(nh):
            oh = o[:, h * _GDV:(h + 1) * _GDV]
            ms = jnp.mean(oh * oh, axis=-1, keepdims=True)
            outs.append(oh * lax.rsqrt(ms + _EPS) * nw_ref[...])
        o_ref[b] = (jnp.concatenate(outs, axis=1) * _silu(g_ref[b])).astype(o_ref.dtype)


def _gla_mix(p3, a_up_p, a_b, norm_w, bb=4):
    b, s, _ = p3.shape
    nc = s // _C
    bb = min(bb, b)
    assert b % bb == 0
    n = _GKW // _GDK * _C
    ri = lax.broadcasted_iota(jnp.int32, (n, n), 0)
    ci = lax.broadcasted_iota(jnp.int32, (n, n), 1)
    incl = (((ri // _C) == (ci // _C)) & (ci <= ri)).astype(F32)
    tri = (lax.broadcasted_iota(jnp.int32, (_C, _C), 0) >= lax.broadcasted_iota(jnp.int32, (_C, _C), 1)).astype(BF16)

    def full(shape):
        return pl.BlockSpec(shape, lambda bi, c: (0,) * len(shape))

    return pl.pallas_call(
        _gla_kernel,
        out_shape=jax.ShapeDtypeStruct((b, s, _GW), BF16),
        grid=(b // bb, nc),
        in_specs=[
            pl.BlockSpec((bb, _C, _GKW), lambda bi, c: (bi, c, _O_GQ // _GKW)),
            pl.BlockSpec((bb, _C, _GKW), lambda bi, c: (bi, c, _O_GK // _GKW)),
            pl.BlockSpec((bb, _C, _GW), lambda bi, c: (bi, c, _O_GV // _GW)),
            pl.BlockSpec((bb, _C, _GW), lambda bi, c: (bi, c, _O_GG // _GW)),
            pl.BlockSpec((bb, _C, 128), lambda bi, c: (bi, c, _O_GL // 128)),
            full((128, _GKW)), full((1, _GKW)), full((1, _GDV)), full((n, n)), full((_C, _C)),
        ],
        out_specs=pl.BlockSpec((bb, _C, _GW), lambda bi, c: (bi, c, 0)),
        scratch_shapes=[pltpu.VMEM((bb, _GW, _GKW), F32)],
        compiler_params=pltpu.CompilerParams(
            dimension_semantics=("parallel", "arbitrary"),
            vmem_limit_bytes=_VMEM_LIMIT),
        name="gla_mix",
    )(p3, p3, p3, p3, p3, a_up_p, a_b, norm_w, incl, tri)


def _outproj_kernel(x_ref, ya_ref, yb_ref, yc_ref, w_ref, fw_ref, o_ref, *, final_norm):
    acc = x_ref[...]
    acc = acc + jnp.dot(ya_ref[...], w_ref[0:_RW, :], preferred_element_type=F32)
    acc = acc + jnp.dot(yb_ref[...], w_ref[_RW:_RW + _MW, :], preferred_element_type=F32)
    acc = acc + jnp.dot(yc_ref[...], w_ref[_RW + _MW:, :], preferred_element_type=F32)
    if final_norm:
        ms = jnp.mean(acc * acc, axis=-1, keepdims=True)
        acc = acc * lax.rsqrt(ms + _EPS) * fw_ref[...]
    o_ref[...] = acc


def _out_proj(x2, ya, yb, yc, w_bf16, fw, final_norm, tm=512):
    t, d = x2.shape
    return pl.pallas_call(
        functools.partial(_outproj_kernel, final_norm=final_norm),
        out_shape=jax.ShapeDtypeStruct((t, d), F32),
        grid=(t // tm,),
        in_specs=[
            pl.BlockSpec((tm, d), lambda i: (i, 0)),
            pl.BlockSpec((tm, _RW), lambda i: (i, 0)),
            pl.BlockSpec((tm, _MW), lambda i: (i, 0)),
            pl.BlockSpec((tm, _GW), lambda i: (i, 0)),
            pl.BlockSpec((d, d), lambda i: (0, 0)),
            pl.BlockSpec((1, d), lambda i: (0, 0)),
        ],
        out_specs=pl.BlockSpec((tm, d), lambda i: (i, 0)),
        compiler_params=pltpu.CompilerParams(
            dimension_semantics=("parallel",),
            vmem_limit_bytes=_VMEM_LIMIT),
        name="out_proj",
    )(x2, ya, yb, yc, w_bf16, fw)


def _wprep_kernel(w_ref, o_ref):
    rows = w_ref.shape[0]
    for a, b, o in _RUNS[:-1]:
        o_ref[:, o:o + (b - a)] = w_ref[:, a:b].astype(BF16)
    a, b, o = _RUNS[-1]
    tail = jnp.concatenate([w_ref[:, a:b], jnp.zeros((rows, 128 - (b - a)), F32)], axis=1)
    o_ref[:, o:o + 128] = tail.astype(BF16)
    o_ref[:, o + 128:] = jnp.zeros((rows, _NP - o - 128), BF16)


def _prep_w_in(w_in, tr=256):
    depth, d, n = w_in.shape
    assert n == _D_IN and d % tr == 0
    return pl.pallas_call(
        _wprep_kernel,
        out_shape=jax.ShapeDtypeStruct((depth, d, _NP), BF16),
        grid=(depth, d // tr),
        in_specs=[pl.BlockSpec((None, tr, n), lambda l, i: (l, i, 0))],
        out_specs=pl.BlockSpec((None, tr, _NP), lambda l, i: (l, i, 0)),
        compiler_params=pltpu.CompilerParams(
            dimension_semantics=("parallel", "parallel"),
            vmem_limit_bytes=_VMEM_LIMIT),
        name="w_prep",
    )(w_in)


def kernel(x, norm_w, w_in, w_out, rwkv_mu, rwkv_w0, rwkv_w2, rwkv_a0, rwkv_a2, rwkv_k_k, rwkv_k_a,
           rwkv_r_k, rwkv_ln_w, rwkv_ln_b, gla_a_up, gla_a_b, gla_norm_w, final_norm_w):
    b, s, d = x.shape
    depth = norm_w.shape[0]
    x2 = x.reshape(b * s, d)
    zero_lora = jnp.zeros((_RLORA, _RW), F32)
    fw = final_norm_w.reshape(1, d)
    w_in_p = _prep_w_in(w_in)
    for l in range(depth):
        w_out_l = w_out[l].astype(BF16)
        mu = rwkv_mu[l]
        mu_r, mu_k, mu_v = (mu[i * _RW:(i + 1) * _RW].reshape(1, _RW) for i in range(3))
        mu_wa = mu[3 * _RW:].reshape(1, 2 * _RLORA)
        w2p = jnp.concatenate([rwkv_w2[l], zero_lora], axis=0)
        a2p = jnp.concatenate([zero_lora, rwkv_a2[l]], axis=0)
        a_up_p = jnp.concatenate([gla_a_up[l], jnp.zeros((128 - _GRANK, _GKW), F32)], axis=0)

        p = _in_proj(x2, norm_w[l].reshape(1, d), w_in_p, l)
        p3 = p.reshape(b, s, _NP)
        row = lambda a: a.reshape(1, _RW)
        y_a = _rwkv_mix(p3, mu_r, mu_k, mu_v, mu_wa, row(rwkv_w0[l]), w2p, row(rwkv_a0[l]), a2p,
                        row(rwkv_k_k[l]), row(rwkv_k_a[l]), row(rwkv_r_k[l]),
                        row(rwkv_ln_w[l]), row(rwkv_ln_b[l]))
        y_b = _moba_mix(p3)
        y_c = _gla_mix(p3, a_up_p, gla_a_b[l].reshape(1, _GKW), gla_norm_w[l].reshape(1, _GDV))
        x2 = _out_proj(x2, y_a.reshape(b * s, _RW), y_b.reshape(b * s, _MW), y_c.reshape(b * s, _GW),
                       w_out_l, fw, final_norm=(l == depth - 1))
    return x2.reshape(b, s, d)
```

```python
import functools

import jax
import jax.numpy as jnp
from jax import lax
from jax.experimental import pallas as pl
from jax.experimental.pallas import tpu as pltpu

F32 = jnp.float32
BF16 = jnp.bfloat16

_D = 2048
_RW = 1024
_RN = 64
_RLORA = 64
_GN_EPS = 64e-5
_MW = 512
_MD = 128
_MBLK = 256
_MTOPK = 3
_GW = 512
_GKW = 256
_GDK = 64
_GDV = 128
_GRANK = 16
_GTEMP = 16.0
_EPS = 1e-6
_C = 64
_GRP = 256

_D_IN = 7824
_RUNS = ((0, 3072, 0),
         (3200, 4224, 3072),
         (4224, 6272, 4096),
         (6784, 7808, 6144),
         (6272, 6784, 7168),
         (3072, 3200, 7680),
         (7808, 7824, 7808))
_O_R, _O_K, _O_V, _O_GA = 0, 1024, 2048, 3072
_O_MQ, _O_MK, _O_MV, _O_MG = 4096, 4608, 5120, 5632
_O_GV, _O_GG = 6144, 6656
_O_GQ, _O_GK = 7168, 7424
_O_WA, _O_GL = 7680, 7808
_NP = 8192
assert all(o % _GW == 0 for o in (_O_GV, _O_GG))
assert all(o % _RW == 0 for o in (_O_R, _O_K, _O_V, _O_GA))
assert all(o % _GKW == 0 for o in (_O_GQ, _O_GK))
assert all(o % _MW == 0 for o in (_O_MQ, _O_MK, _O_MV, _O_MG))
assert all(o % 128 == 0 for o in (_O_WA, _O_GL))

_NN = (((1,), (0,)), ((), ()))
_NT = (((1,), (1,)), ((), ()))
_TN = (((0,), (0,)), ((), ()))

_VMEM_LIMIT = 56 * 1024 * 1024

_P_INV = 1
_P_MAIN = 1
_P_STATE = 1
_P_SUM = 3
_P_HEADSUM = 2


def _parts(x, n):
    if x.dtype == BF16:
        return [x]
    out, rem = [], x
    for i in range(n):
        p = rem.astype(BF16)
        out.append(p)
        if i + 1 < n:
            rem = rem - p.astype(F32)
    return out


def _mm(a, b, dims=_NN, na=1, nb=1):
    pa = a if isinstance(a, list) else _parts(a, na)
    pb = b if isinstance(b, list) else _parts(b, nb)
    order = max(len(pa), len(pb))
    free_axis = 1 if dims[0][0] == (0,) else 0
    m = pa[0].shape[free_axis]
    acc = None
    for j, y in enumerate(pb):
        xs = pa[:order - j]
        if not xs:
            continue
        x = xs[0] if len(xs) == 1 else jnp.concatenate(xs, axis=free_axis)
        t = lax.dot_general(x, y, dims, preferred_element_type=F32)
        for i in range(len(xs)):
            blk = t[i * m:(i + 1) * m]
            acc = blk if acc is None else acc + blk
    return acc


def _softplus(x):
    return jnp.maximum(x, 0.0) + jnp.log(1.0 + jnp.exp(-jnp.abs(x)))


def _sigmoid(x):
    return 1.0 / (1.0 + jnp.exp(-x))


def _silu(x):
    return x * _sigmoid(x)


def _iota(shape, dim):
    return lax.broadcasted_iota(jnp.int32, shape, dim)


def _idiv(x, n):
    assert n & (n - 1) == 0
    return jnp.right_shift(x, n.bit_length() - 1)


def _stack_heads(x, head_w, n_heads):
    hid = _idiv(_iota(x.shape, 1), head_w)
    return jnp.concatenate([jnp.where(hid == g, x, 0.0) for g in range(n_heads)], axis=0)


def _unstack_heads(xs, c, n_heads):
    out = xs[0:c]
    for g in range(1, n_heads):
        out = out + xs[g * c:(g + 1) * c]
    return out


def _inproj_kernel(x_ref, nw_ref, w_ref, o_ref, h_ref):
    @pl.when(pl.program_id(1) == 0)
    def _():
        x = x_ref[...]
        ms = jnp.mean(x * x, axis=-1, keepdims=True)
        h_ref[...] = (x * lax.rsqrt(ms + _EPS) * nw_ref[...]).astype(BF16)

    o_ref[...] = lax.dot_general(h_ref[...], w_ref[...], _NT, preferred_element_type=F32)


def _in_proj(x2, nw, w_bf16, layer, tm=1024, tn=1024):
    t, d = x2.shape
    n = w_bf16.shape[1]
    return pl.pallas_call(
        _inproj_kernel,
        out_shape=jax.ShapeDtypeStruct((t, n), F32),
        grid=(t // tm, n // tn),
        in_specs=[
            pl.BlockSpec((tm, d), lambda i, j: (i, 0)),
            pl.BlockSpec((1, d), lambda i, j: (0, 0)),
            pl.BlockSpec((None, tn, d), lambda i, j: (layer, j, 0)),
        ],
        out_specs=pl.BlockSpec((tm, tn), lambda i, j: (i, j)),
        scratch_shapes=[pltpu.VMEM((tm, d), BF16)],
        compiler_params=pltpu.CompilerParams(
            dimension_semantics=("parallel", "arbitrary"),
            vmem_limit_bytes=_VMEM_LIMIT),
        name="in_proj",
    )(x2, nw, w_bf16)


def _token_shift(x, prev_ref, mu):
    c = x.shape[0]
    rows = _iota(x.shape, 0)
    xp = jnp.where(rows == 0, prev_ref[7:8, :], pltpu.roll(x, 1, 0))
    prev_ref[...] = x[c - 8:c, :]
    return x + mu * (xp - x)


def _head_sums(x, head_ones):
    return _head_sums_multi([x], head_ones)[0]


def _head_sums_multi(xs, head_ones):
    c = xs[0].shape[0]
    ng = xs[0].shape[1] // _GRP
    rows = [p[:, g * _GRP:(g + 1) * _GRP] for x in xs for p in _parts(x, _P_HEADSUM) for g in range(ng)]
    t = lax.dot_general(jnp.concatenate(rows, axis=0), head_ones, _NN, preferred_element_type=F32)
    outs = []
    for k in range(len(xs)):
        cols = []
        for g in range(ng):
            acc = None
            for i in range(_P_HEADSUM):
                r0 = ((k * _P_HEADSUM + i) * ng + g) * c
                acc = t[r0:r0 + c] if acc is None else acc + t[r0:r0 + c]
            cols.append(acc)
        outs.append(jnp.concatenate(cols, axis=1))
    return outs


def _stack_lanes(x, lane_masks):
    return jnp.concatenate([x * m for m in lane_masks], axis=0)


def _rwkv_groups(at, rt, bt, kt, bh, kh, v, own, w_end, st_ref, strict, incl, eye_c, eye, same):
    c = _C
    nh = _GRP // _RN
    gs = range(at.shape[1] // _GRP)
    lane_head = _idiv(_iota((1, _GRP), 1), _RN)
    hm = [jnp.where(lane_head == h, 1.0, 0.0).astype(BF16) for h in range(nh)]

    def grp(x, g):
        return x[:, g * _GRP:(g + 1) * _GRP]

    def bf(x):
        return x.astype(BF16)

    def dot(a, b, dims=_NN):
        return lax.dot_general(a, b, dims, preferred_element_type=F32)

    def stack(x):
        return _stack_lanes(bf(x), hm)

    at_b = [bf(grp(at, g)) for g in gs]
    v_b = [bf(grp(v, g)) for g in gs]
    bh_b = [bf(grp(bh, g)) for g in gs]
    lhs = [jnp.concatenate([at_b[g], bf(grp(rt, g))], axis=0) for g in gs]
    rhs = [jnp.concatenate([stack(grp(bt, g)), stack(grp(kt, g))], axis=0) for g in gs]
    a_all = [dot(lhs[g], rhs[g], _NT) for g in gs]
    a_ab = [a_all[g][:c, :nh * c] * strict for g in gs]
    a_ak = [bf(a_all[g][:c, nh * c:] * strict) for g in gs]
    a_rb = [bf(a_all[g][c:, :nh * c] * incl) for g in gs]
    a_rk = [bf(a_all[g][c:, nh * c:] * strict) for g in gs]

    x = [eye_c + a_ab[g] for g in gs]
    p_b = [bf(a_ab[g]) for g in gs]
    p = [dot(p_b[g], _stack_lanes(p_b[g], hm)) for g in gs]
    n_sq = c.bit_length() - 2
    for it in range(n_sq):
        p_bd = [stack(p[g]) for g in gs]
        if it + 1 < n_sq:
            xp = [dot(jnp.concatenate([bf(x[g]), bf(p[g])], axis=0), p_bd[g]) for g in gs]
            x = [x[g] + xp[g][:c] for g in gs]
            p = [xp[g][c:] for g in gs]
        else:
            x = [x[g] + dot(bf(x[g]), p_bd[g]) for g in gs]
    t = [bf(x[g]) for g in gs]

    v_s = [_stack_lanes(v_b[g], hm) for g in gs]
    abar = [dot(t[g], _stack_lanes(at_b[g], hm)) for g in gs]
    uv = [dot(a_ak[g], v_s[g]) for g in gs]
    ubar = [dot(t[g], stack(uv[g])) for g in gs]
    abar_b = [bf(abar[g]) for g in gs]
    ubar_b = [bf(ubar[g]) for g in gs]
    rbar = [grp(rt, g) + dot(a_rb[g], _stack_lanes(abar_b[g], hm)) for g in gs]
    ybar = [dot(a_rb[g], _stack_lanes(ubar_b[g], hm)) + dot(a_rk[g], v_s[g]) + grp(own, g) for g in gs]
    mt = [eye * grp(w_end, g) + same * dot(bh_b[g], abar_b[g], _TN) for g in gs]
    dt = [same * dot(jnp.concatenate([bh_b[g], bf(grp(kh, g))], axis=0),
                     jnp.concatenate([ubar_b[g], v_b[g]], axis=0), _TN) for g in gs]

    stp = [_parts(st_ref[g], _P_STATE) for g in gs]
    y = [_mm(rbar[g], stp[g], na=_P_STATE) + ybar[g] for g in gs]
    for g in gs:
        st_ref[g] = _mm(mt[g], stp[g], na=_P_STATE, nb=_P_STATE) + dt[g]
    return jnp.concatenate(y, axis=1)


def _rwkv_kernel(pr_ref, pk_ref, pv_ref, pg_ref, pwa_ref,
                 mur_ref, muk_ref, muv_ref, muwa_ref,
                 w0_ref, w2_ref, a0_ref, a2_ref, kk_ref, ka_ref, rk_ref, lnw_ref, lnb_ref,
                 strict_ref, incl_ref, eyec_ref, eye_ref, same_ref, hones_ref, tri_ref,
                 o_ref,
                 s_ref, prev_r, prev_k, prev_v, prev_wa):
    c = _C
    nb = pr_ref.shape[0]

    @pl.when(pl.program_id(1) == 0)
    def _():
        s_ref[...] = jnp.zeros_like(s_ref)
        prev_r[...] = jnp.zeros_like(prev_r)
        prev_k[...] = jnp.zeros_like(prev_k)
        prev_v[...] = jnp.zeros_like(prev_v)
        prev_wa[...] = jnp.zeros_like(prev_wa)

    head_ones = hones_ref[...]

    def lanes(ref):
        return jnp.concatenate([ref[b] for b in range(nb)], axis=1)

    def rep(ref):
        return jnp.concatenate([ref[...]] * nb, axis=1)

    def unrows(x):
        return jnp.concatenate([x[b * c:(b + 1) * c] for b in range(nb)], axis=1)

    r = _token_shift(lanes(pr_ref), prev_r, rep(mur_ref))
    k = _token_shift(lanes(pk_ref), prev_k, rep(muk_ref))
    v = _token_shift(lanes(pv_ref), prev_v, rep(muv_ref))
    wa = _token_shift(lanes(pwa_ref), prev_wa, rep(muwa_ref))
    wa = jnp.concatenate([wa[:, b * 128:(b + 1) * 128] for b in range(nb)], axis=0)

    wdec = rep(w0_ref) + unrows(_mm(jnp.tanh(wa), w2_ref[...], na=2))
    wdec = -_softplus(-wdec) - 0.5
    lw = -jnp.exp(wdec)
    a_lr = _sigmoid(rep(a0_ref) + unrows(_mm(wa, a2_ref[...], na=2)))
    kk = k * rep(kk_ref)
    kk = kk * lax.rsqrt(jnp.maximum(_head_sums(kk * kk, head_ones), 1e-24))
    k2 = k * (1.0 + (a_lr - 1.0) * rep(ka_ref))
    bb = kk * a_lr
    cl = _mm(tri_ref[...], lw, nb=_P_SUM)
    cl_end = cl[c - 1:c, :]
    e_in = jnp.exp(cl)
    e_neg = jnp.exp(-cl)
    e_end = jnp.exp(cl_end - cl)
    w_end = jnp.exp(cl_end)
    at = -kk * jnp.exp(cl - lw)
    rt = r * e_in
    bt = bb * e_neg
    kt = k2 * e_neg
    bh = bb * e_end
    kh = k2 * e_end

    rk = r * k2
    rk_sum, bonus_sum = _head_sums_multi([rk, rk * rep(rk_ref)], head_ones)
    y = _rwkv_groups(at, rt, bt, kt, bh, kh, v, rk_sum * v, w_end, s_ref,
                     strict_ref[...], incl_ref[...], eyec_ref[...], eye_ref[...], same_ref[...])

    inv_n = 1.0 / _RN
    mean = _head_sums(y, head_ones) * inv_n
    yc = y - mean
    var = _head_sums(yc * yc, head_ones) * inv_n
    yn = yc * lax.rsqrt(var + _GN_EPS) * rep(lnw_ref) + rep(lnb_ref)
    out = yn + bonus_sum * v
    for b in range(nb):
        o_ref[b] = (out[:, b * _RW:(b + 1) * _RW] * _silu(pg_ref[b])).astype(o_ref.dtype)


def _rwkv_consts():
    n = _GRP
    t = lax.broadcasted_iota(jnp.int32, (_C, n // _RN * _C), 0)
    s = lax.broadcasted_iota(jnp.int32, (_C, n // _RN * _C), 1) % _C
    strict = (s < t).astype(F32)
    incl = (s <= t).astype(F32)
    eye_c = (s == t).astype(F32)
    ri = lax.broadcasted_iota(jnp.int32, (n, n), 0)
    ci = lax.broadcasted_iota(jnp.int32, (n, n), 1)
    eye = (ri == ci).astype(F32)
    same = (ri // _RN) == (ci // _RN)
    ti = lax.broadcasted_iota(jnp.int32, (_C, _C), 0)
    tj = lax.broadcasted_iota(jnp.int32, (_C, _C), 1)
    tri = (ti >= tj).astype(BF16)
    return strict, incl, eye_c, eye, same.astype(F32), same.astype(BF16), tri


def _rwkv_mix(p3, mu_r, mu_k, mu_v, mu_wa, w0, w2p, a0, a2p, k_k, k_a, r_k, ln_w, ln_b, bb=4):
    b, s, _ = p3.shape
    nc = s // _C
    bb = min(bb, b)
    assert b % bb == 0
    ng = bb * _RW // _GRP

    def pspec(off):
        return pl.BlockSpec((bb, _C, _RW), lambda bi, c, o=off // _RW: (bi, c, o))

    def full(shape):
        return pl.BlockSpec(shape, lambda bi, c: (0,) * len(shape))

    vec = full((1, _RW))
    lora = full((2 * _RLORA, _RW))
    sq = full((_GRP, _GRP))
    cat = full((_C, _GRP // _RN * _C))
    return pl.pallas_call(
        _rwkv_kernel,
        out_shape=jax.ShapeDtypeStruct((b, s, _RW), BF16),
        grid=(b // bb, nc),
        in_specs=[
            pspec(_O_R), pspec(_O_K), pspec(_O_V), pspec(_O_GA),
            pl.BlockSpec((bb, _C, 128), lambda bi, c: (bi, c, _O_WA // 128)),
            vec, vec, vec, full((1, 2 * _RLORA)),
            vec, lora, vec, lora, vec, vec, vec, vec, vec,
            cat, cat, cat, sq, sq, sq, full((_C, _C)),
        ],
        out_specs=pl.BlockSpec((bb, _C, _RW), lambda bi, c: (bi, c, 0)),
        scratch_shapes=[
            pltpu.VMEM((ng, _GRP, _GRP), F32),
            pltpu.VMEM((8, bb * _RW), F32), pltpu.VMEM((8, bb * _RW), F32),
            pltpu.VMEM((8, bb * _RW), F32), pltpu.VMEM((8, bb * 2 * _RLORA), F32),
        ],
        compiler_params=pltpu.CompilerParams(
            dimension_semantics=("parallel", "arbitrary"),
            vmem_limit_bytes=_VMEM_LIMIT),
        name="rwkv_mix",
    )(p3, p3, p3, p3, p3, mu_r, mu_k, mu_v, mu_wa, w0, w2p, a0, a2p, k_k, k_a, r_k, ln_w, ln_b,
      *_rwkv_consts())


def _moba_block(ii, q_ref, k_ref, v_ref, g_ref, o_ref, kmean_ref):
    blk = _MBLK
    nh = _MW // _MD
    hs = range(nh)
    scale = _MD ** -0.5
    neg = -1e30
    npast = ii * blk

    def hsl(h):
        return slice(h * _MD, (h + 1) * _MD)

    def dot(a, b, dims):
        return lax.dot_general(a, b, dims, preferred_element_type=F32)

    q = [q_ref[:, hsl(h)] for h in hs]
    qs = [(q[h] * scale).astype(BF16) for h in hs]
    own = slice(npast, npast + blk)
    kpos = _iota((blk, blk), 0)
    qpos = _iota((blk, blk), 1)
    s_own = [jnp.where(kpos <= qpos, dot(k_ref[own, hsl(h)].astype(BF16), qs[h], _NT), neg) for h in hs]
    m = [jnp.max(s_own[h], axis=0, keepdims=True) for h in hs]

    if ii > 0:
        npad = kmean_ref.shape[0]
        bid = _iota((npad, blk), 0)
        gate = [jnp.where(bid < ii, _mm(kmean_ref[:, hsl(h)], q[h], _NT, na=3, nb=3), -jnp.inf)
                for h in hs]
        s_past = [dot(k_ref[0:npast, hsl(h)].astype(BF16), qs[h], _NT) for h in hs]
        masked = []
        for h in hs:
            rows = []
            for j in range(ii):
                gj = gate[h][j:j + 1, :]
                ahead = (gate[h] > gj) | ((gate[h] == gj) & (bid < j))
                rank = jnp.sum(jnp.where(ahead, 1.0, 0.0), axis=0, keepdims=True)
                rows.append(jnp.where(rank < float(_MTOPK), s_past[h][j * blk:(j + 1) * blk], neg))
            masked.append(rows[0] if ii == 1 else jnp.concatenate(rows, axis=0))
        m = [jnp.maximum(m[h], jnp.max(masked[h], axis=0, keepdims=True)) for h in hs]
        p_past = [jnp.exp(masked[h] - m[h]) for h in hs]

    p_own = [jnp.exp(s_own[h] - m[h]) for h in hs]
    l = [jnp.sum(p_own[h], axis=0, keepdims=True) for h in hs]
    acc = [dot(v_ref[own, hsl(h)].astype(BF16), p_own[h].astype(BF16), _TN) for h in hs]
    if ii > 0:
        l = [l[h] + jnp.sum(p_past[h], axis=0, keepdims=True) for h in hs]
        acc = [acc[h] + dot(v_ref[0:npast, hsl(h)].astype(BF16), p_past[h].astype(BF16), _TN) for h in hs]

    o = jnp.concatenate([(acc[h] / l[h]).T for h in hs], axis=1)
    o_ref[...] = (o * _silu(g_ref[...])).astype(o_ref.dtype)


def _moba_kernel(q_ref, k_ref, v_ref, g_ref, o_ref, kmean_ref):
    i = pl.program_id(1)
    blk = _MBLK
    nb = k_ref.shape[0] // blk

    @pl.when(i == 0)
    def _():
        kmean_ref[...] = jnp.zeros_like(kmean_ref)
        for j in range(nb):
            kmean_ref[j:j + 1, :] = jnp.mean(k_ref[j * blk:(j + 1) * blk, :], axis=0, keepdims=True)

    for ii in range(nb):
        pl.when(i == ii)(functools.partial(_moba_block, ii, q_ref, k_ref, v_ref, g_ref, o_ref, kmean_ref))


def _moba_mix(p3):
    b, s, _ = p3.shape
    nq = s // _MBLK
    npad = -(-nq // 16) * 16
    return pl.pallas_call(
        _moba_kernel,
        out_shape=jax.ShapeDtypeStruct((b, s, _MW), BF16),
        grid=(b, nq),
        in_specs=[
            pl.BlockSpec((None, _MBLK, _MW), lambda bi, i: (bi, i, _O_MQ // _MW)),
            pl.BlockSpec((None, s, _MW), lambda bi, i: (bi, 0, _O_MK // _MW)),
            pl.BlockSpec((None, s, _MW), lambda bi, i: (bi, 0, _O_MV // _MW)),
            pl.BlockSpec((None, _MBLK, _MW), lambda bi, i: (bi, i, _O_MG // _MW)),
        ],
        out_specs=pl.BlockSpec((None, _MBLK, _MW), lambda bi, i: (bi, i, 0)),
        scratch_shapes=[pltpu.VMEM((npad, _MW), F32)],
        compiler_params=pltpu.CompilerParams(
            dimension_semantics=("parallel", "arbitrary"),
            vmem_limit_bytes=_VMEM_LIMIT),
        name="moba_mix",
    )(p3, p3, p3, p3)


def _gla_kernel(q_ref, k_ref, v_ref, g_ref, low_ref, aup_ref, ab_ref, nw_ref, incl_ref, tri_ref,
                o_ref, st_ref):
    c = _C
    nh = _GKW // _GDK
    bs = range(q_ref.shape[0])

    @pl.when(pl.program_id(1) == 0)
    def _():
        st_ref[...] = jnp.zeros_like(st_ref)

    incl, tri = incl_ref[...], tri_ref[...]
    z = [_mm(low_ref[b], aup_ref[...], na=2, nb=2) + ab_ref[...] for b in bs]
    la = [-_softplus(-z[b]) * (1.0 / _GTEMP) for b in bs]
    bcum = [_mm(tri, la[b], nb=_P_SUM) for b in bs]
    b_last = [bcum[b][c - 1:c, :] for b in bs]
    k = [k_ref[b] for b in bs]
    qe = [_parts(_stack_heads(q_ref[b] * (_GDK ** -0.5) * jnp.exp(bcum[b]), _GDK, nh), 1) for b in bs]
    ke = [_stack_heads(k[b] * jnp.exp(-bcum[b]), _GDK, nh) for b in bs]
    kd = [_stack_heads(k[b] * jnp.exp(b_last[b] - bcum[b]), _GDK, nh) for b in bs]
    v_s = [_parts(_stack_heads(v_ref[b], _GDV, nh), 1) for b in bs]

    att = [_mm(qe[b], ke[b], _NT) * incl for b in bs]
    st = [st_ref[b] for b in bs]
    o_s = [_mm(att[b], v_s[b]) + _mm(qe[b], st[b], _NT) for b in bs]
    for b in bs:
        st_ref[b] = st[b] * jnp.exp(b_last[b]) + _mm(v_s[b], kd[b], _TN)

    for b in bs:
        o = _unstack_heads(o_s[b], c, nh)
        outs = []
        for h in range(nh):
            oh = o[:, h * _GDV:(h + 1) * _GDV]
            ms = jnp.mean(oh * oh, axis=-1, keepdims=True)
            outs.append(oh * lax.rsqrt(ms + _EPS) * nw_ref[...])
        o_ref[b] = (jnp.concatenate(outs, axis=1) * _silu(g_ref[b])).astype(o_ref.dtype)


def _gla_mix(p3, a_up_p, a_b, norm_w, bb=4):
    b, s, _ = p3.shape
    nc = s // _C
    bb = min(bb, b)
    assert b % bb == 0
    n = _GKW // _GDK * _C
    ri = lax.broadcasted_iota(jnp.int32, (n, n), 0)
    ci = lax.broadcasted_iota(jnp.int32, (n, n), 1)
    incl = (((ri // _C) == (ci // _C)) & (ci <= ri)).astype(F32)
    tri = (lax.broadcasted_iota(jnp.int32, (_C, _C), 0) >= lax.broadcasted_iota(jnp.int32, (_C, _C), 1)).astype(BF16)

    def full(shape):
        return pl.BlockSpec(shape, lambda bi, c: (0,) * len(shape))

    return pl.pallas_call(
        _gla_kernel,
        out_shape=jax.ShapeDtypeStruct((b, s, _GW), BF16),
        grid=(b // bb, nc),
        in_specs=[
            pl.BlockSpec((bb, _C, _GKW), lambda bi, c: (bi, c, _O_GQ // _GKW)),
            pl.BlockSpec((bb, _C, _GKW), lambda bi, c: (bi, c, _O_GK // _GKW)),
            pl.BlockSpec((bb, _C, _GW), lambda bi, c: (bi, c, _O_GV // _GW)),
            pl.BlockSpec((bb, _C, _GW), lambda bi, c: (bi, c, _O_GG // _GW)),
            pl.BlockSpec((bb, _C, 128), lambda bi, c: (bi, c, _O_GL // 128)),
            full((128, _GKW)), full((1, _GKW)), full((1, _GDV)), full((n, n)), full((_C, _C)),
        ],
        out_specs=pl.BlockSpec((bb, _C, _GW), lambda bi, c: (bi, c, 0)),
        scratch_shapes=[pltpu.VMEM((bb, _GW, _GKW), F32)],
        compiler_params=pltpu.CompilerParams(
            dimension_semantics=("parallel", "arbitrary"),
            vmem_limit_bytes=_VMEM_LIMIT),
        name="gla_mix",
    )(p3, p3, p3, p3, p3, a_up_p, a_b, norm_w, incl, tri)


def _outproj_kernel(x_ref, ya_ref, yb_ref, yc_ref, w_ref, fw_ref, o_ref, *, final_norm):
    acc = x_ref[...]
    acc = acc + jnp.dot(ya_ref[...], w_ref[0:_RW, :], preferred_element_type=F32)
    acc = acc + jnp.dot(yb_ref[...], w_ref[_RW:_RW + _MW, :], preferred_element_type=F32)
    acc = acc + jnp.dot(yc_ref[...], w_ref[_RW + _MW:, :], preferred_element_type=F32)
    if final_norm:
        ms = jnp.mean(acc * acc, axis=-1, keepdims=True)
        acc = acc * lax.rsqrt(ms + _EPS) * fw_ref[...]
    o_ref[...] = acc


def _out_proj(x2, ya, yb, yc, w_bf16, fw, final_norm, tm=512):
    t, d = x2.shape
    return pl.pallas_call(
        functools.partial(_outproj_kernel, final_norm=final_norm),
        out_shape=jax.ShapeDtypeStruct((t, d), F32),
        grid=(t // tm,),
        in_specs=[
            pl.BlockSpec((tm, d), lambda i: (i, 0)),
            pl.BlockSpec((tm, _RW), lambda i: (i, 0)),
            pl.BlockSpec((tm, _MW), lambda i: (i, 0)),
            pl.BlockSpec((tm, _GW), lambda i: (i, 0)),
            pl.BlockSpec((d, d), lambda i: (0, 0)),
            pl.BlockSpec((1, d), lambda i: (0, 0)),
        ],
        out_specs=pl.BlockSpec((tm, d), lambda i: (i, 0)),
        compiler_params=pltpu.CompilerParams(
            dimension_semantics=("parallel",),
            vmem_limit_bytes=_VMEM_LIMIT),
        name="out_proj",
    )(x2, ya, yb, yc, w_bf16, fw)


_WTILE = 128


def _wprep_src_tile(j):
    src = jnp.int32(_D_IN // _WTILE)
    for a, b, o in _RUNS:
        lo = o // _WTILE
        hi = lo + -(-(b - a) // _WTILE)
        src = jnp.where((j >= lo) & (j < hi), j + (a - o) // _WTILE, src)
    return src


def _wprep_kernel(w_ref, o_ref):
    j = pl.program_id(1)
    last_dst = _RUNS[-1][2] + _RUNS[-1][1] - _RUNS[-1][0]
    valid = jnp.clip(last_dst - j * _WTILE, 0, _WTILE)
    x = w_ref[...]
    o_ref[...] = jnp.where(_iota(x.shape, 0) < valid, x, 0.0).astype(BF16)


def _prep_w_in(w_t):
    depth, n, d = w_t.shape
    assert n == _D_IN and all(a % _WTILE == 0 and o % _WTILE == 0 for a, _, o in _RUNS)
    return pl.pallas_call(
        _wprep_kernel,
        out_shape=jax.ShapeDtypeStruct((depth, _NP, d), BF16),
        grid=(depth, _NP // _WTILE),
        in_specs=[pl.BlockSpec((None, _WTILE, d), lambda l, j: (l, _wprep_src_tile(j), 0))],
        out_specs=pl.BlockSpec((None, _WTILE, d), lambda l, j: (l, j, 0)),
        compiler_params=pltpu.CompilerParams(
            dimension_semantics=("parallel", "parallel"),
            vmem_limit_bytes=_VMEM_LIMIT),
        name="w_prep",
    )(w_t)


def kernel(x, norm_w, w_in, w_out, rwkv_mu, rwkv_w0, rwkv_w2, rwkv_a0, rwkv_a2, rwkv_k_k, rwkv_k_a,
           rwkv_r_k, rwkv_ln_w, rwkv_ln_b, gla_a_up, gla_a_b, gla_norm_w, final_norm_w):
    b, s, d = x.shape
    depth = norm_w.shape[0]
    x2 = x.reshape(b * s, d)
    zero_lora = jnp.zeros((_RLORA, _RW), F32)
    fw = final_norm_w.reshape(1, d)
    w_in_p = _prep_w_in(jnp.swapaxes(w_in, 1, 2))
    for l in range(depth):
        w_out_l = w_out[l].astype(BF16)
        mu = rwkv_mu[l]
        mu_r, mu_k, mu_v = (mu[i * _RW:(i + 1) * _RW].reshape(1, _RW) for i in range(3))
        mu_wa = mu[3 * _RW:].reshape(1, 2 * _RLORA)
        w2p = jnp.concatenate([rwkv_w2[l], zero_lora], axis=0)
        a2p = jnp.concatenate([zero_lora, rwkv_a2[l]], axis=0)
        a_up_p = jnp.concatenate([gla_a_up[l], jnp.zeros((128 - _GRANK, _GKW), F32)], axis=0)

        p = _in_proj(x2, norm_w[l].reshape(1, d), w_in_p, l)
        p3 = p.reshape(b, s, _NP)
        row = lambda a: a.reshape(1, _RW)
        y_a = _rwkv_mix(p3, mu_r, mu_k, mu_v, mu_wa, row(rwkv_w0[l]), w2p, row(rwkv_a0[l]), a2p,
                        row(rwkv_k_k[l]), row(rwkv_k_a[l]), row(rwkv_r_k[l]),
                        row(rwkv_ln_w[l]), row(rwkv_ln_b[l]))
        y_b = _moba_mix(p3)
        y_c = _gla_mix(p3, a_up_p, gla_a_b[l].reshape(1, _GKW), gla_norm_w[l].reshape(1, _GDV))
        x2 = _out_proj(x2, y_a.reshape(b * s, _RW), y_b.reshape(b * s, _MW), y_c.reshape(b * s, _GW),
                       w_out_l, fw, final_norm=(l == depth - 1))
    return x2.reshape(b, s, d)
```

```python
import functools

import jax
import jax.numpy as jnp
from jax import lax
from jax.experimental import pallas as pl
from jax.experimental.pallas import tpu as pltpu

F32 = jnp.float32
BF16 = jnp.bfloat16

_D = 2048
_RW = 1024
_RN = 64
_RLORA = 64
_GN_EPS = 64e-5
_MW = 512
_MD = 128
_MBLK = 256
_MTOPK = 3
_GW = 512
_GKW = 256
_GDK = 64
_GDV = 128
_GRANK = 16
_GTEMP = 16.0
_EPS = 1e-6
_C = 64
_GRP = 256

_D_IN = 7824
_RUNS = ((0, 3072, 0),
         (3200, 4224, 3072),
         (4224, 6272, 4096),
         (6784, 7808, 6144),
         (6272, 6784, 7168),
         (3072, 3200, 7680),
         (7808, 7824, 7808))
_O_R, _O_K, _O_V, _O_GA = 0, 1024, 2048, 3072
_O_MQ, _O_MK, _O_MV, _O_MG = 4096, 4608, 5120, 5632
_O_GV, _O_GG = 6144, 6656
_O_GQ, _O_GK = 7168, 7424
_O_WA, _O_GL = 7680, 7808
_NP = 8192
assert all(o % _GW == 0 for o in (_O_GV, _O_GG))
assert all(o % _RW == 0 for o in (_O_R, _O_K, _O_V, _O_GA))
assert all(o % _GKW == 0 for o in (_O_GQ, _O_GK))
assert all(o % _MW == 0 for o in (_O_MQ, _O_MK, _O_MV, _O_MG))
assert all(o % 128 == 0 for o in (_O_WA, _O_GL))

_NN = (((1,), (0,)), ((), ()))
_NT = (((1,), (1,)), ((), ()))
_TN = (((0,), (0,)), ((), ()))

_VMEM_LIMIT = 56 * 1024 * 1024

_P_INV = 1
_P_MAIN = 1
_P_STATE = 1
_P_SUM = 3
_P_HEADSUM = 2


def _parts(x, n):
    if x.dtype == BF16:
        return [x]
    out, rem = [], x
    for i in range(n):
        p = rem.astype(BF16)
        out.append(p)
        if i + 1 < n:
            rem = rem - p.astype(F32)
    return out


def _mm(a, b, dims=_NN, na=1, nb=1):
    pa = a if isinstance(a, list) else _parts(a, na)
    pb = b if isinstance(b, list) else _parts(b, nb)
    order = max(len(pa), len(pb))
    free_axis = 1 if dims[0][0] == (0,) else 0
    m = pa[0].shape[free_axis]
    acc = None
    for j, y in enumerate(pb):
        xs = pa[:order - j]
        if not xs:
            continue
        x = xs[0] if len(xs) == 1 else jnp.concatenate(xs, axis=free_axis)
        t = lax.dot_general(x, y, dims, preferred_element_type=F32)
        for i in range(len(xs)):
            blk = t[i * m:(i + 1) * m]
            acc = blk if acc is None else acc + blk
    return acc


def _softplus(x):
    return jnp.maximum(x, 0.0) + jnp.log(1.0 + jnp.exp(-jnp.abs(x)))


def _sigmoid(x):
    return 1.0 / (1.0 + jnp.exp(-x))


def _silu(x):
    return x * _sigmoid(x)


def _iota(shape, dim):
    return lax.broadcasted_iota(jnp.int32, shape, dim)


def _idiv(x, n):
    assert n & (n - 1) == 0
    return jnp.right_shift(x, n.bit_length() - 1)


def _stack_heads(x, head_w, n_heads):
    hid = _idiv(_iota(x.shape, 1), head_w)
    return jnp.concatenate([jnp.where(hid == g, x, 0.0) for g in range(n_heads)], axis=0)


def _unstack_heads(xs, c, n_heads):
    out = xs[0:c]
    for g in range(1, n_heads):
        out = out + xs[g * c:(g + 1) * c]
    return out


def _inproj_kernel(x_ref, nw_ref, w_ref, o_ref, h_ref):
    @pl.when(pl.program_id(1) == 0)
    def _():
        x = x_ref[...]
        ms = jnp.mean(x * x, axis=-1, keepdims=True)
        h_ref[...] = (x * lax.rsqrt(ms + _EPS) * nw_ref[...]).astype(BF16)

    o_ref[...] = lax.dot_general(h_ref[...], w_ref[...], _NT, preferred_element_type=F32)


def _in_proj(x2, nw, w_bf16, layer, tm=1024, tn=1024):
    t, d = x2.shape
    n = w_bf16.shape[1]
    return pl.pallas_call(
        _inproj_kernel,
        out_shape=jax.ShapeDtypeStruct((t, n), F32),
        grid=(t // tm, n // tn),
        in_specs=[
            pl.BlockSpec((tm, d), lambda i, j: (i, 0)),
            pl.BlockSpec((1, d), lambda i, j: (0, 0)),
            pl.BlockSpec((None, tn, d), lambda i, j: (layer, j, 0)),
        ],
        out_specs=pl.BlockSpec((tm, tn), lambda i, j: (i, j)),
        scratch_shapes=[pltpu.VMEM((tm, d), BF16)],
        compiler_params=pltpu.CompilerParams(
            dimension_semantics=("parallel", "arbitrary"),
            vmem_limit_bytes=_VMEM_LIMIT),
        name="in_proj",
    )(x2, nw, w_bf16)


def _token_shift(x, prev_ref, mu):
    c = x.shape[0]
    rows = _iota(x.shape, 0)
    xp = jnp.where(rows == 0, prev_ref[7:8, :], pltpu.roll(x, 1, 0))
    prev_ref[...] = x[c - 8:c, :]
    return x + mu * (xp - x)


def _head_sums(x, head_ones):
    c = x.shape[0]
    ng = x.shape[1] // _GRP
    rows = [p[:, g * _GRP:(g + 1) * _GRP] for p in _parts(x, _P_HEADSUM) for g in range(ng)]
    t = lax.dot_general(jnp.concatenate(rows, axis=0), head_ones, _NN, preferred_element_type=F32)
    cols = []
    for g in range(ng):
        acc = t[g * c:(g + 1) * c]
        for i in range(1, _P_HEADSUM):
            acc = acc + t[(i * ng + g) * c:(i * ng + g + 1) * c]
        cols.append(acc)
    return jnp.concatenate(cols, axis=1)


def _stack_lanes(x, lane_masks):
    return jnp.concatenate([x * m for m in lane_masks], axis=0)


def _rwkv_groups(at, rt, bt, kt, bh, kh, v, w_end, st_ref, strict, incl, eye_c, eye, same):
    c = _C
    nh = _GRP // _RN
    gs = range(at.shape[1] // _GRP)
    lane_head = _idiv(_iota((1, _GRP), 1), _RN)
    hm = [jnp.where(lane_head == h, 1.0, 0.0).astype(BF16) for h in range(nh)]

    def grp(x, g):
        return x[:, g * _GRP:(g + 1) * _GRP]

    def bf(x):
        return x.astype(BF16)

    def dot(a, b, dims=_NN):
        return lax.dot_general(a, b, dims, preferred_element_type=F32)

    def stack(x):
        return _stack_lanes(bf(x), hm)

    at_b = [bf(grp(at, g)) for g in gs]
    v_b = [bf(grp(v, g)) for g in gs]
    bh_b = [bf(grp(bh, g)) for g in gs]
    lhs = [jnp.concatenate([at_b[g], bf(grp(rt, g))], axis=0) for g in gs]
    rhs = [jnp.concatenate([stack(grp(bt, g)), stack(grp(kt, g))], axis=0) for g in gs]
    a_all = [dot(lhs[g], rhs[g], _NT) for g in gs]
    a_ab = [a_all[g][:c, :nh * c] * strict for g in gs]
    a_ak = [bf(a_all[g][:c, nh * c:] * strict) for g in gs]
    a_rb = [bf(a_all[g][c:, :nh * c] * incl) for g in gs]
    a_rk = [bf(a_all[g][c:, nh * c:] * incl) for g in gs]

    x = [eye_c + a_ab[g] for g in gs]
    p_b = [bf(a_ab[g]) for g in gs]
    p = [dot(p_b[g], _stack_lanes(p_b[g], hm)) for g in gs]
    n_sq = c.bit_length() - 2
    for it in range(n_sq):
        p_bd = [stack(p[g]) for g in gs]
        if it + 1 < n_sq:
            xp = [dot(jnp.concatenate([bf(x[g]), bf(p[g])], axis=0), p_bd[g]) for g in gs]
            x = [x[g] + xp[g][:c] for g in gs]
            p = [xp[g][c:] for g in gs]
        else:
            x = [x[g] + dot(bf(x[g]), p_bd[g]) for g in gs]
    t = [bf(x[g]) for g in gs]

    v_s = [_stack_lanes(v_b[g], hm) for g in gs]
    abar = [dot(t[g], _stack_lanes(at_b[g], hm)) for g in gs]
    uv = [dot(a_ak[g], v_s[g]) for g in gs]
    ubar = [dot(t[g], stack(uv[g])) for g in gs]
    abar_b = [bf(abar[g]) for g in gs]
    ubar_b = [bf(ubar[g]) for g in gs]
    rbar = [grp(rt, g) + dot(a_rb[g], _stack_lanes(abar_b[g], hm)) for g in gs]
    ybar = [dot(a_rb[g], _stack_lanes(ubar_b[g], hm)) + dot(a_rk[g], v_s[g]) for g in gs]
    mt = [eye * grp(w_end, g) + same * dot(bh_b[g], abar_b[g], _TN) for g in gs]
    dt = [same * dot(jnp.concatenate([bh_b[g], bf(grp(kh, g))], axis=0),
                     jnp.concatenate([ubar_b[g], v_b[g]], axis=0), _TN) for g in gs]

    stp = [_parts(st_ref[g], _P_STATE) for g in gs]
    y = [_mm(rbar[g], stp[g], na=_P_STATE) + ybar[g] for g in gs]
    for g in gs:
        st_ref[g] = _mm(mt[g], stp[g], na=_P_STATE, nb=_P_STATE) + dt[g]
    return jnp.concatenate(y, axis=1)


def _rwkv_kernel(pr_ref, pk_ref, pv_ref, pg_ref, pwa_ref,
                 mur_ref, muk_ref, muv_ref, muwa_ref,
                 w0_ref, w2_ref, a0_ref, a2_ref, kk_ref, ka_ref, rk_ref, lnw_ref, lnb_ref,
                 strict_ref, incl_ref, eyec_ref, eye_ref, same_ref, hones_ref, tri_ref,
                 o_ref,
                 s_ref, prev_r, prev_k, prev_v, prev_wa):
    c = _C
    nb = pr_ref.shape[0]

    @pl.when(pl.program_id(1) == 0)
    def _():
        s_ref[...] = jnp.zeros_like(s_ref)
        prev_r[...] = jnp.zeros_like(prev_r)
        prev_k[...] = jnp.zeros_like(prev_k)
        prev_v[...] = jnp.zeros_like(prev_v)
        prev_wa[...] = jnp.zeros_like(prev_wa)

    head_ones = hones_ref[...]

    def lanes(ref):
        return jnp.concatenate([ref[b] for b in range(nb)], axis=1)

    def rep(ref):
        return jnp.concatenate([ref[...]] * nb, axis=1)

    def unrows(x):
        return jnp.concatenate([x[b * c:(b + 1) * c] for b in range(nb)], axis=1)

    r = _token_shift(lanes(pr_ref), prev_r, rep(mur_ref))
    k = _token_shift(lanes(pk_ref), prev_k, rep(muk_ref))
    v = _token_shift(lanes(pv_ref), prev_v, rep(muv_ref))
    wa = _token_shift(lanes(pwa_ref), prev_wa, rep(muwa_ref))
    wa = jnp.concatenate([wa[:, b * 128:(b + 1) * 128] for b in range(nb)], axis=0)

    wdec = rep(w0_ref) + unrows(_mm(jnp.tanh(wa), w2_ref[...], na=2))
    wdec = -_softplus(-wdec) - 0.5
    lw = -jnp.exp(wdec)
    a_lr = _sigmoid(rep(a0_ref) + unrows(_mm(wa, a2_ref[...], na=2)))
    kk = k * rep(kk_ref)
    kk = kk * lax.rsqrt(jnp.maximum(_head_sums(kk * kk, head_ones), 1e-24))
    k2 = k * (1.0 + (a_lr - 1.0) * rep(ka_ref))
    bb = kk * a_lr
    cl = _mm(tri_ref[...], lw, nb=_P_SUM)
    cl_end = cl[c - 1:c, :]
    e_in = jnp.exp(cl)
    e_neg = jnp.exp(-cl)
    e_end = jnp.exp(cl_end - cl)
    w_end = jnp.exp(cl_end)
    at = -kk * jnp.exp(cl - lw)
    rt = r * e_in
    bt = bb * e_neg
    kt = k2 * e_neg
    bh = bb * e_end
    kh = k2 * e_end

    y = _rwkv_groups(at, rt, bt, kt, bh, kh, v, w_end, s_ref,
                     strict_ref[...], incl_ref[...], eyec_ref[...], eye_ref[...], same_ref[...])

    inv_n = 1.0 / _RN
    mean = _head_sums(y, head_ones) * inv_n
    yc = y - mean
    var = _head_sums(yc * yc, head_ones) * inv_n
    yn = yc * lax.rsqrt(var + _GN_EPS) * rep(lnw_ref) + rep(lnb_ref)
    bonus = _head_sums(r * k2 * rep(rk_ref), head_ones) * v
    out = yn + bonus
    for b in range(nb):
        o_ref[b] = (out[:, b * _RW:(b + 1) * _RW] * _silu(pg_ref[b])).astype(o_ref.dtype)


def _rwkv_consts():
    n = _GRP
    t = lax.broadcasted_iota(jnp.int32, (_C, n // _RN * _C), 0)
    s = lax.broadcasted_iota(jnp.int32, (_C, n // _RN * _C), 1) % _C
    strict = (s < t).astype(F32)
    incl = (s <= t).astype(F32)
    eye_c = (s == t).astype(F32)
    ri = lax.broadcasted_iota(jnp.int32, (n, n), 0)
    ci = lax.broadcasted_iota(jnp.int32, (n, n), 1)
    eye = (ri == ci).astype(F32)
    same = (ri // _RN) == (ci // _RN)
    ti = lax.broadcasted_iota(jnp.int32, (_C, _C), 0)
    tj = lax.broadcasted_iota(jnp.int32, (_C, _C), 1)
    tri = (ti >= tj).astype(BF16)
    return strict, incl, eye_c, eye, same.astype(F32), same.astype(BF16), tri


def _rwkv_mix(p3, mu_r, mu_k, mu_v, mu_wa, w0, w2p, a0, a2p, k_k, k_a, r_k, ln_w, ln_b, bb=4):
    b, s, _ = p3.shape
    nc = s // _C
    bb = min(bb, b)
    assert b % bb == 0
    ng = bb * _RW // _GRP

    def pspec(off):
        return pl.BlockSpec((bb, _C, _RW), lambda bi, c, o=off // _RW: (bi, c, o))

    def full(shape):
        return pl.BlockSpec(shape, lambda bi, c: (0,) * len(shape))

    vec = full((1, _RW))
    lora = full((2 * _RLORA, _RW))
    sq = full((_GRP, _GRP))
    cat = full((_C, _GRP // _RN * _C))
    return pl.pallas_call(
        _rwkv_kernel,
        out_shape=jax.ShapeDtypeStruct((b, s, _RW), BF16),
        grid=(b // bb, nc),
        in_specs=[
            pspec(_O_R), pspec(_O_K), pspec(_O_V), pspec(_O_GA),
            pl.BlockSpec((bb, _C, 128), lambda bi, c: (bi, c, _O_WA // 128)),
            vec, vec, vec, full((1, 2 * _RLORA)),
            vec, lora, vec, lora, vec, vec, vec, vec, vec,
            cat, cat, cat, sq, sq, sq, full((_C, _C)),
        ],
        out_specs=pl.BlockSpec((bb, _C, _RW), lambda bi, c: (bi, c, 0)),
        scratch_shapes=[
            pltpu.VMEM((ng, _GRP, _GRP), F32),
            pltpu.VMEM((8, bb * _RW), F32), pltpu.VMEM((8, bb * _RW), F32),
            pltpu.VMEM((8, bb * _RW), F32), pltpu.VMEM((8, bb * 2 * _RLORA), F32),
        ],
        compiler_params=pltpu.CompilerParams(
            dimension_semantics=("parallel", "arbitrary"),
            vmem_limit_bytes=_VMEM_LIMIT),
        name="rwkv_mix",
    )(p3, p3, p3, p3, p3, mu_r, mu_k, mu_v, mu_wa, w0, w2p, a0, a2p, k_k, k_a, r_k, ln_w, ln_b,
      *_rwkv_consts())


def _moba_block(ii, q_ref, k_ref, v_ref, g_ref, o_ref, kmean_ref):
    blk = _MBLK
    nh = _MW // _MD
    hs = range(nh)
    scale = _MD ** -0.5
    neg = -1e30
    npast = ii * blk

    def hsl(h):
        return slice(h * _MD, (h + 1) * _MD)

    def dot(a, b, dims):
        return lax.dot_general(a, b, dims, preferred_element_type=F32)

    q = [q_ref[:, hsl(h)] for h in hs]
    qs = [(q[h] * scale).astype(BF16) for h in hs]
    own = slice(npast, npast + blk)
    kpos = _iota((blk, blk), 0)
    qpos = _iota((blk, blk), 1)
    s_own = [jnp.where(kpos <= qpos, dot(k_ref[own, hsl(h)].astype(BF16), qs[h], _NT), neg) for h in hs]
    m = [jnp.max(s_own[h], axis=0, keepdims=True) for h in hs]

    if ii > 0:
        npad = kmean_ref.shape[0]
        bid = _iota((npad, blk), 0)
        gate = [jnp.where(bid < ii, _mm(kmean_ref[:, hsl(h)], q[h], _NT, na=3, nb=3), -jnp.inf)
                for h in hs]
        s_past = [dot(k_ref[0:npast, hsl(h)].astype(BF16), qs[h], _NT) for h in hs]
        masked = []
        for h in hs:
            rows = []
            for j in range(ii):
                gj = gate[h][j:j + 1, :]
                ahead = (gate[h] > gj) | ((gate[h] == gj) & (bid < j))
                rank = jnp.sum(jnp.where(ahead, 1.0, 0.0), axis=0, keepdims=True)
                rows.append(jnp.where(rank < float(_MTOPK), s_past[h][j * blk:(j + 1) * blk], neg))
            masked.append(rows[0] if ii == 1 else jnp.concatenate(rows, axis=0))
        m = [jnp.maximum(m[h], jnp.max(masked[h], axis=0, keepdims=True)) for h in hs]
        p_past = [jnp.exp(masked[h] - m[h]) for h in hs]

    p_own = [jnp.exp(s_own[h] - m[h]) for h in hs]
    l = [jnp.sum(p_own[h], axis=0, keepdims=True) for h in hs]
    acc = [dot(v_ref[own, hsl(h)].astype(BF16), p_own[h].astype(BF16), _TN) for h in hs]
    if ii > 0:
        l = [l[h] + jnp.sum(p_past[h], axis=0, keepdims=True) for h in hs]
        acc = [acc[h] + dot(v_ref[0:npast, hsl(h)].astype(BF16), p_past[h].astype(BF16), _TN) for h in hs]

    o = jnp.concatenate([(acc[h] / l[h]).T for h in hs], axis=1)
    o_ref[...] = (o * _silu(g_ref[...])).astype(o_ref.dtype)


def _moba_kernel(q_ref, k_ref, v_ref, g_ref, o_ref, kmean_ref):
    i = pl.program_id(1)
    blk = _MBLK
    nb = k_ref.shape[0] // blk

    @pl.when(i == 0)
    def _():
        kmean_ref[...] = jnp.zeros_like(kmean_ref)
        for j in range(nb):
            kmean_ref[j:j + 1, :] = jnp.mean(k_ref[j * blk:(j + 1) * blk, :], axis=0, keepdims=True)

    for ii in range(nb):
        pl.when(i == ii)(functools.partial(_moba_block, ii, q_ref, k_ref, v_ref, g_ref, o_ref, kmean_ref))


def _moba_mix(p3):
    b, s, _ = p3.shape
    nq = s // _MBLK
    npad = -(-nq // 16) * 16
    return pl.pallas_call(
        _moba_kernel,
        out_shape=jax.ShapeDtypeStruct((b, s, _MW), BF16),
        grid=(b, nq),
        in_specs=[
            pl.BlockSpec((None, _MBLK, _MW), lambda bi, i: (bi, i, _O_MQ // _MW)),
            pl.BlockSpec((None, s, _MW), lambda bi, i: (bi, 0, _O_MK // _MW)),
            pl.BlockSpec((None, s, _MW), lambda bi, i: (bi, 0, _O_MV // _MW)),
            pl.BlockSpec((None, _MBLK, _MW), lambda bi, i: (bi, i, _O_MG // _MW)),
        ],
        out_specs=pl.BlockSpec((None, _MBLK, _MW), lambda bi, i: (bi, i, 0)),
        scratch_shapes=[pltpu.VMEM((npad, _MW), F32)],
        compiler_params=pltpu.CompilerParams(
            dimension_semantics=("parallel", "arbitrary"),
            vmem_limit_bytes=_VMEM_LIMIT),
        name="moba_mix",
    )(p3, p3, p3, p3)


def _gla_kernel(q_ref, k_ref, v_ref, g_ref, low_ref, aup_ref, ab_ref, nw_ref, incl_ref, tri_ref,
                o_ref, st_ref):
    c = _C
    nh = _GKW // _GDK
    bs = range(q_ref.shape[0])

    @pl.when(pl.program_id(1) == 0)
    def _():
        st_ref[...] = jnp.zeros_like(st_ref)

    incl, tri = incl_ref[...], tri_ref[...]
    z = [_mm(low_ref[b], aup_ref[...], na=2, nb=2) + ab_ref[...] for b in bs]
    la = [-_softplus(-z[b]) * (1.0 / _GTEMP) for b in bs]
    bcum = [_mm(tri, la[b], nb=_P_SUM) for b in bs]
    b_last = [bcum[b][c - 1:c, :] for b in bs]
    k = [k_ref[b] for b in bs]
    qe = [_parts(_stack_heads(q_ref[b] * (_GDK ** -0.5) * jnp.exp(bcum[b]), _GDK, nh), 1) for b in bs]
    ke = [_stack_heads(k[b] * jnp.exp(-bcum[b]), _GDK, nh) for b in bs]
    kd = [_stack_heads(k[b] * jnp.exp(b_last[b] - bcum[b]), _GDK, nh) for b in bs]
    v_s = [_parts(_stack_heads(v_ref[b], _GDV, nh), 1) for b in bs]

    att = [_mm(qe[b], ke[b], _NT) * incl for b in bs]
    st = [st_ref[b] for b in bs]
    o_s = [_mm(att[b], v_s[b]) + _mm(qe[b], st[b], _NT) for b in bs]
    for b in bs:
        st_ref[b] = st[b] * jnp.exp(b_last[b]) + _mm(v_s[b], kd[b], _TN)

    for b in bs:
        o = _unstack_heads(o_s[b], c, nh)
        outs = []
        for h in range(nh):
            oh = o[:, h * _GDV:(h + 1) * _GDV]
            ms = jnp.mean(oh * oh, axis=-1, keepdims=True)
            outs.append(oh * lax.rsqrt(ms + _EPS) * nw_ref[...])
        o_ref[b] = (jnp.concatenate(outs, axis=1) * _silu(g_ref[b])).astype(o_ref.dtype)


def _gla_mix(p3, a_up_p, a_b, norm_w, bb=4):
    b, s, _ = p3.shape
    nc = s // _C
    bb = min(bb, b)
    assert b % bb == 0
    n = _GKW // _GDK * _C
    ri = lax.broadcasted_iota(jnp.int32, (n, n), 0)
    ci = lax.broadcasted_iota(jnp.int32, (n, n), 1)
    incl = (((ri // _C) == (ci // _C)) & (ci <= ri)).astype(F32)
    tri = (lax.broadcasted_iota(jnp.int32, (_C, _C), 0) >= lax.broadcasted_iota(jnp.int32, (_C, _C), 1)).astype(BF16)

    def full(shape):
        return pl.BlockSpec(shape, lambda bi, c: (0,) * len(shape))

    return pl.pallas_call(
        _gla_kernel,
        out_shape=jax.ShapeDtypeStruct((b, s, _GW), BF16),
        grid=(b // bb, nc),
        in_specs=[
            pl.BlockSpec((bb, _C, _GKW), lambda bi, c: (bi, c, _O_GQ // _GKW)),
            pl.BlockSpec((bb, _C, _GKW), lambda bi, c: (bi, c, _O_GK // _GKW)),
            pl.BlockSpec((bb, _C, _GW), lambda bi, c: (bi, c, _O_GV // _GW)),
            pl.BlockSpec((bb, _C, _GW), lambda bi, c: (bi, c, _O_GG // _GW)),
            pl.BlockSpec((bb, _C, 128), lambda bi, c: (bi, c, _O_GL // 128)),
            full((128, _GKW)), full((1, _GKW)), full((1, _GDV)), full((n, n)), full((_C, _C)),
        ],
        out_specs=pl.BlockSpec((bb, _C, _GW), lambda bi, c: (bi, c, 0)),
        scratch_shapes=[pltpu.VMEM((bb, _GW, _GKW), F32)],
        compiler_params=pltpu.CompilerParams(
            dimension_semantics=("parallel", "arbitrary"),
            vmem_limit_bytes=_VMEM_LIMIT),
        name="gla_mix",
    )(p3, p3, p3, p3, p3, a_up_p, a_b, norm_w, incl, tri)


def _outproj_kernel(x_ref, ya_ref, yb_ref, yc_ref, w_ref, fw_ref, o_ref, *, final_norm):
    acc = x_ref[...]
    acc = acc + jnp.dot(ya_ref[...], w_ref[0:_RW, :], preferred_element_type=F32)
    acc = acc + jnp.dot(yb_ref[...], w_ref[_RW:_RW + _MW, :], preferred_element_type=F32)
    acc = acc + jnp.dot(yc_ref[...], w_ref[_RW + _MW:, :], preferred_element_type=F32)
    if final_norm:
        ms = jnp.mean(acc * acc, axis=-1, keepdims=True)
        acc = acc * lax.rsqrt(ms + _EPS) * fw_ref[...]
    o_ref[...] = acc


def _out_proj(x2, ya, yb, yc, w_bf16, fw, final_norm, tm=512):
    t, d = x2.shape
    return pl.pallas_call(
        functools.partial(_outproj_kernel, final_norm=final_norm),
        out_shape=jax.ShapeDtypeStruct((t, d), F32),
        grid=(t // tm,),
        in_specs=[
            pl.BlockSpec((tm, d), lambda i: (i, 0)),
            pl.BlockSpec((tm, _RW), lambda i: (i, 0)),
            pl.BlockSpec((tm, _MW), lambda i: (i, 0)),
            pl.BlockSpec((tm, _GW), lambda i: (i, 0)),
            pl.BlockSpec((d, d), lambda i: (0, 0)),
            pl.BlockSpec((1, d), lambda i: (0, 0)),
        ],
        out_specs=pl.BlockSpec((tm, d), lambda i: (i, 0)),
        compiler_params=pltpu.CompilerParams(
            dimension_semantics=("parallel",),
            vmem_limit_bytes=_VMEM_LIMIT),
        name="out_proj",
    )(x2, ya, yb, yc, w_bf16, fw)


_WTILE = 128


def _wprep_src_tile(j):
    src = jnp.int32(_D_IN // _WTILE)
    for a, b, o in _RUNS:
        lo = o // _WTILE
        hi = lo + -(-(b - a) // _WTILE)
        src = jnp.where((j >= lo) & (j < hi), j + (a - o) // _WTILE, src)
    return src


def _wprep_kernel(w_ref, o_ref):
    j = pl.program_id(1)
    last_dst = _RUNS[-1][2] + _RUNS[-1][1] - _RUNS[-1][0]
    valid = jnp.clip(last_dst - j * _WTILE, 0, _WTILE)
    x = w_ref[...]
    o_ref[...] = jnp.where(_iota(x.shape, 0) < valid, x, 0.0).astype(BF16)


def _prep_w_in(w_t):
    depth, n, d = w_t.shape
    assert n == _D_IN and all(a % _WTILE == 0 and o % _WTILE == 0 for a, _, o in _RUNS)
    return pl.pallas_call(
        _wprep_kernel,
        out_shape=jax.ShapeDtypeStruct((depth, _NP, d), BF16),
        grid=(depth, _NP // _WTILE),
        in_specs=[pl.BlockSpec((None, _WTILE, d), lambda l, j: (l, _wprep_src_tile(j), 0))],
        out_specs=pl.BlockSpec((None, _WTILE, d), lambda l, j: (l, j, 0)),
        compiler_params=pltpu.CompilerParams(
            dimension_semantics=("parallel", "parallel"),
            vmem_limit_bytes=_VMEM_LIMIT),
        name="w_prep",
    )(w_t)


def kernel(x, norm_w, w_in, w_out, rwkv_mu, rwkv_w0, rwkv_w2, rwkv_a0, rwkv_a2, rwkv_k_k, rwkv_k_a,
           rwkv_r_k, rwkv_ln_w, rwkv_ln_b, gla_a_up, gla_a_b, gla_norm_w, final_norm_w):
    b, s, d = x.shape
    depth = norm_w.shape[0]
    x2 = x.reshape(b * s, d)
    zero_lora = jnp.zeros((_RLORA, _RW), F32)
    fw = final_norm_w.reshape(1, d)
    w_in_p = _prep_w_in(jnp.swapaxes(w_in, 1, 2))
    for l in range(depth):
        w_out_l = w_out[l].astype(BF16)
        mu = rwkv_mu[l]
        mu_r, mu_k, mu_v = (mu[i * _RW:(i + 1) * _RW].reshape(1, _RW) for i in range(3))
        mu_wa = mu[3 * _RW:].reshape(1, 2 * _RLORA)
        w2p = jnp.concatenate([rwkv_w2[l], zero_lora], axis=0)
        a2p = jnp.concatenate([zero_lora, rwkv_a2[l]], axis=0)
        a_up_p = jnp.concatenate([gla_a_up[l], jnp.zeros((128 - _GRANK, _GKW), F32)], axis=0)

        p = _in_proj(x2, norm_w[l].reshape(1, d), w_in_p, l)
        p3 = p.reshape(b, s, _NP)
        row = lambda a: a.reshape(1, _RW)
        y_a = _rwkv_mix(p3, mu_r, mu_k, mu_v, mu_wa, row(rwkv_w0[l]), w2p, row(rwkv_a0[l]), a2p,
                        row(rwkv_k_k[l]), row(rwkv_k_a[l]), row(rwkv_r_k[l]),
                        row(rwkv_ln_w[l]), row(rwkv_ln_b[l]))
        y_b = _moba_mix(p3)
        y_c = _gla_mix(p3, a_up_p, gla_a_b[l].reshape(1, _GKW), gla_norm_w[l].reshape(1, _GDV))
        x2 = _out_proj(x2, y_a.reshape(b * s, _RW), y_b.reshape(b * s, _MW), y_c.reshape(b * s, _GW),
                       w_out_l, fw, final_norm=(l == depth - 1))
    return x2.reshape(b, s, d)
```

```python
import functools

import jax
import jax.numpy as jnp
from jax import lax
from jax.experimental import pallas as pl
from jax.experimental.pallas import tpu as pltpu

F32 = jnp.float32
BF16 = jnp.bfloat16

_D = 2048
_RW = 1024
_RN = 64
_RLORA = 64
_GN_EPS = 64e-5
_MW = 512
_MD = 128
_MBLK = 256
_MTOPK = 3
_GW = 512
_GKW = 256
_GDK = 64
_GDV = 128
_GRANK = 16
_GTEMP = 16.0
_EPS = 1e-6
_C = 64
_GRP = 256

_D_IN = 7824
_RUNS = ((0, 3072, 0),
         (3200, 4224, 3072),
         (4224, 6272, 4096),
         (6784, 7808, 6144),
         (6272, 6784, 7168),
         (3072, 3200, 7680),
         (7808, 7824, 7808))
_O_R, _O_K, _O_V, _O_GA = 0, 1024, 2048, 3072
_O_MQ, _O_MK, _O_MV, _O_MG = 4096, 4608, 5120, 5632
_O_GV, _O_GG = 6144, 6656
_O_GQ, _O_GK = 7168, 7424
_O_WA, _O_GL = 7680, 7808
_NP = 7936
assert all(o % _GW == 0 for o in (_O_GV, _O_GG))
assert all(o % _RW == 0 for o in (_O_R, _O_K, _O_V, _O_GA))
assert all(o % _GKW == 0 for o in (_O_GQ, _O_GK))
assert all(o % _MW == 0 for o in (_O_MQ, _O_MK, _O_MV, _O_MG))
assert all(o % 128 == 0 for o in (_O_WA, _O_GL))

_NN = (((1,), (0,)), ((), ()))
_NT = (((1,), (1,)), ((), ()))
_TN = (((0,), (0,)), ((), ()))

_VMEM_LIMIT = 56 * 1024 * 1024

_P_INV = 1
_P_MAIN = 1
_P_STATE = 1
_P_SUM = 3
_P_HEADSUM = 2


def _parts(x, n):
    if x.dtype == BF16:
        return [x]
    out, rem = [], x
    for i in range(n):
        p = rem.astype(BF16)
        out.append(p)
        if i + 1 < n:
            rem = rem - p.astype(F32)
    return out


def _mm(a, b, dims=_NN, na=1, nb=1):
    pa = a if isinstance(a, list) else _parts(a, na)
    pb = b if isinstance(b, list) else _parts(b, nb)
    order = max(len(pa), len(pb))
    free_axis = 1 if dims[0][0] == (0,) else 0
    m = pa[0].shape[free_axis]
    acc = None
    for j, y in enumerate(pb):
        xs = pa[:order - j]
        if not xs:
            continue
        x = xs[0] if len(xs) == 1 else jnp.concatenate(xs, axis=free_axis)
        t = lax.dot_general(x, y, dims, preferred_element_type=F32)
        for i in range(len(xs)):
            blk = t[i * m:(i + 1) * m]
            acc = blk if acc is None else acc + blk
    return acc


def _softplus(x):
    return jnp.maximum(x, 0.0) + jnp.log(1.0 + jnp.exp(-jnp.abs(x)))


def _sigmoid(x):
    return 1.0 / (1.0 + jnp.exp(-x))


def _silu(x):
    return x * _sigmoid(x)


def _iota(shape, dim):
    return lax.broadcasted_iota(jnp.int32, shape, dim)


def _idiv(x, n):
    assert n & (n - 1) == 0
    return jnp.right_shift(x, n.bit_length() - 1)


def _stack_heads(x, head_w, n_heads):
    hid = _idiv(_iota(x.shape, 1), head_w)
    return jnp.concatenate([jnp.where(hid == g, x, 0.0) for g in range(n_heads)], axis=0)


def _unstack_heads(xs, c, n_heads):
    out = xs[0:c]
    for g in range(1, n_heads):
        out = out + xs[g * c:(g + 1) * c]
    return out


def _inproj_kernel(x_ref, nw_ref, w_ref, o_ref):
    x = x_ref[...]
    ms = jnp.mean(x * x, axis=-1, keepdims=True)
    h = (x * lax.rsqrt(ms + _EPS) * nw_ref[...]).astype(BF16)
    o_ref[...] = lax.dot_general(h, w_ref[...], _NT, preferred_element_type=F32)


def _in_proj(x2, nw, w_bf16, layer, tm=256):
    t, d = x2.shape
    n = w_bf16.shape[1]
    return pl.pallas_call(
        _inproj_kernel,
        out_shape=jax.ShapeDtypeStruct((t, n), F32),
        grid=(t // tm,),
        in_specs=[
            pl.BlockSpec((tm, d), lambda i: (i, 0)),
            pl.BlockSpec((1, d), lambda i: (0, 0)),
            pl.BlockSpec((None, n, d), lambda i: (layer, 0, 0), pipeline_mode=pl.Buffered(1)),
        ],
        out_specs=pl.BlockSpec((tm, n), lambda i: (i, 0)),
        compiler_params=pltpu.CompilerParams(
            dimension_semantics=("parallel",),
            vmem_limit_bytes=_VMEM_LIMIT),
        name="in_proj",
    )(x2, nw, w_bf16)


def _token_shift(x, prev_ref, mu):
    c = x.shape[0]
    rows = _iota(x.shape, 0)
    xp = jnp.where(rows == 0, prev_ref[7:8, :], pltpu.roll(x, 1, 0))
    prev_ref[...] = x[c - 8:c, :]
    return x + mu * (xp - x)


def _head_sums(x, head_ones):
    c = x.shape[0]
    ng = x.shape[1] // _GRP
    rows = [p[:, g * _GRP:(g + 1) * _GRP] for p in _parts(x, _P_HEADSUM) for g in range(ng)]
    t = lax.dot_general(jnp.concatenate(rows, axis=0), head_ones, _NN, preferred_element_type=F32)
    cols = []
    for g in range(ng):
        acc = t[g * c:(g + 1) * c]
        for i in range(1, _P_HEADSUM):
            acc = acc + t[(i * ng + g) * c:(i * ng + g + 1) * c]
        cols.append(acc)
    return jnp.concatenate(cols, axis=1)


def _stack_lanes(x, lane_masks):
    return jnp.concatenate([x * m for m in lane_masks], axis=0)


def _rwkv_groups(at, rt, bt, kt, bh, kh, v, w_end, st_ref, strict, incl, eye_c, eye, same):
    c = _C
    nh = _GRP // _RN
    gs = range(at.shape[1] // _GRP)
    lane_head = _idiv(_iota((1, _GRP), 1), _RN)
    hm = [jnp.where(lane_head == h, 1.0, 0.0).astype(BF16) for h in range(nh)]

    def grp(x, g):
        return x[:, g * _GRP:(g + 1) * _GRP]

    def bf(x):
        return x.astype(BF16)

    def dot(a, b, dims=_NN):
        return lax.dot_general(a, b, dims, preferred_element_type=F32)

    def stack(x):
        return _stack_lanes(bf(x), hm)

    at_b = [bf(grp(at, g)) for g in gs]
    v_b = [bf(grp(v, g)) for g in gs]
    bh_b = [bf(grp(bh, g)) for g in gs]
    lhs = [jnp.concatenate([at_b[g], bf(grp(rt, g))], axis=0) for g in gs]
    rhs = [jnp.concatenate([stack(grp(bt, g)), stack(grp(kt, g))], axis=0) for g in gs]
    a_all = [dot(lhs[g], rhs[g], _NT) for g in gs]
    a_ab = [a_all[g][:c, :nh * c] * strict for g in gs]
    a_ak = [bf(a_all[g][:c, nh * c:] * strict) for g in gs]
    a_rb = [bf(a_all[g][c:, :nh * c] * incl) for g in gs]
    a_rk = [bf(a_all[g][c:, nh * c:] * incl) for g in gs]

    x = [eye_c + a_ab[g] for g in gs]
    p_b = [bf(a_ab[g]) for g in gs]
    p = [dot(p_b[g], _stack_lanes(p_b[g], hm)) for g in gs]
    n_sq = c.bit_length() - 2
    for it in range(n_sq):
        p_bd = [stack(p[g]) for g in gs]
        if it + 1 < n_sq:
            xp = [dot(jnp.concatenate([bf(x[g]), bf(p[g])], axis=0), p_bd[g]) for g in gs]
            x = [x[g] + xp[g][:c] for g in gs]
            p = [xp[g][c:] for g in gs]
        else:
            x = [x[g] + dot(bf(x[g]), p_bd[g]) for g in gs]
    t = [bf(x[g]) for g in gs]

    v_s = [_stack_lanes(v_b[g], hm) for g in gs]
    abar = [dot(t[g], _stack_lanes(at_b[g], hm)) for g in gs]
    uv = [dot(a_ak[g], v_s[g]) for g in gs]
    ubar = [dot(t[g], stack(uv[g])) for g in gs]
    abar_b = [bf(abar[g]) for g in gs]
    ubar_b = [bf(ubar[g]) for g in gs]
    rbar = [grp(rt, g) + dot(a_rb[g], _stack_lanes(abar_b[g], hm)) for g in gs]
    ybar = [dot(a_rb[g], _stack_lanes(ubar_b[g], hm)) + dot(a_rk[g], v_s[g]) for g in gs]
    mt = [eye * grp(w_end, g) + same * dot(bh_b[g], abar_b[g], _TN) for g in gs]
    dt = [same * dot(jnp.concatenate([bh_b[g], bf(grp(kh, g))], axis=0),
                     jnp.concatenate([ubar_b[g], v_b[g]], axis=0), _TN) for g in gs]

    stp = [_parts(st_ref[g], _P_STATE) for g in gs]
    y = [_mm(rbar[g], stp[g], na=_P_STATE) + ybar[g] for g in gs]
    for g in gs:
        st_ref[g] = _mm(mt[g], stp[g], na=_P_STATE, nb=_P_STATE) + dt[g]
    return jnp.concatenate(y, axis=1)


def _rwkv_kernel(pr_ref, pk_ref, pv_ref, pg_ref, pwa_ref,
                 mur_ref, muk_ref, muv_ref, muwa_ref,
                 w0_ref, w2_ref, a0_ref, a2_ref, kk_ref, ka_ref, rk_ref, lnw_ref, lnb_ref,
                 strict_ref, incl_ref, eyec_ref, eye_ref, same_ref, hones_ref, tri_ref,
                 o_ref,
                 s_ref, prev_r, prev_k, prev_v, prev_wa):
    c = _C
    nb = pr_ref.shape[0]

    @pl.when(pl.program_id(1) == 0)
    def _():
        s_ref[...] = jnp.zeros_like(s_ref)
        prev_r[...] = jnp.zeros_like(prev_r)
        prev_k[...] = jnp.zeros_like(prev_k)
        prev_v[...] = jnp.zeros_like(prev_v)
        prev_wa[...] = jnp.zeros_like(prev_wa)

    head_ones = hones_ref[...]

    def lanes(ref):
        return jnp.concatenate([ref[b] for b in range(nb)], axis=1)

    def rep(ref):
        return jnp.concatenate([ref[...]] * nb, axis=1)

    def unrows(x):
        return jnp.concatenate([x[b * c:(b + 1) * c] for b in range(nb)], axis=1)

    r = _token_shift(lanes(pr_ref), prev_r, rep(mur_ref))
    k = _token_shift(lanes(pk_ref), prev_k, rep(muk_ref))
    v = _token_shift(lanes(pv_ref), prev_v, rep(muv_ref))
    wa = _token_shift(lanes(pwa_ref), prev_wa, rep(muwa_ref))
    wa = jnp.concatenate([wa[:, b * 128:(b + 1) * 128] for b in range(nb)], axis=0)

    wdec = rep(w0_ref) + unrows(_mm(jnp.tanh(wa), w2_ref[...], na=2))
    wdec = -_softplus(-wdec) - 0.5
    lw = -jnp.exp(wdec)
    a_lr = _sigmoid(rep(a0_ref) + unrows(_mm(wa, a2_ref[...], na=2)))
    kk = k * rep(kk_ref)
    kk = kk * lax.rsqrt(jnp.maximum(_head_sums(kk * kk, head_ones), 1e-24))
    k2 = k * (1.0 + (a_lr - 1.0) * rep(ka_ref))
    bb = kk * a_lr
    cl = _mm(tri_ref[...], lw, nb=_P_SUM)
    cl_end = cl[c - 1:c, :]
    e_in = jnp.exp(cl)
    e_neg = jnp.exp(-cl)
    e_end = jnp.exp(cl_end - cl)
    w_end = jnp.exp(cl_end)
    at = -kk * jnp.exp(cl - lw)
    rt = r * e_in
    bt = bb * e_neg
    kt = k2 * e_neg
    bh = bb * e_end
    kh = k2 * e_end

    y = _rwkv_groups(at, rt, bt, kt, bh, kh, v, w_end, s_ref,
                     strict_ref[...], incl_ref[...], eyec_ref[...], eye_ref[...], same_ref[...])

    inv_n = 1.0 / _RN
    mean = _head_sums(y, head_ones) * inv_n
    yc = y - mean
    var = _head_sums(yc * yc, head_ones) * inv_n
    yn = yc * lax.rsqrt(var + _GN_EPS) * rep(lnw_ref) + rep(lnb_ref)
    bonus = _head_sums(r * k2 * rep(rk_ref), head_ones) * v
    out = yn + bonus
    for b in range(nb):
        o_ref[b] = (out[:, b * _RW:(b + 1) * _RW] * _silu(pg_ref[b])).astype(o_ref.dtype)


def _rwkv_consts():
    n = _GRP
    t = lax.broadcasted_iota(jnp.int32, (_C, n // _RN * _C), 0)
    s = lax.broadcasted_iota(jnp.int32, (_C, n // _RN * _C), 1) % _C
    strict = (s < t).astype(F32)
    incl = (s <= t).astype(F32)
    eye_c = (s == t).astype(F32)
    ri = lax.broadcasted_iota(jnp.int32, (n, n), 0)
    ci = lax.broadcasted_iota(jnp.int32, (n, n), 1)
    eye = (ri == ci).astype(F32)
    same = (ri // _RN) == (ci // _RN)
    ti = lax.broadcasted_iota(jnp.int32, (_C, _C), 0)
    tj = lax.broadcasted_iota(jnp.int32, (_C, _C), 1)
    tri = (ti >= tj).astype(BF16)
    return strict, incl, eye_c, eye, same.astype(F32), same.astype(BF16), tri


def _rwkv_mix(p3, mu_r, mu_k, mu_v, mu_wa, w0, w2p, a0, a2p, k_k, k_a, r_k, ln_w, ln_b, bb=4):
    b, s, _ = p3.shape
    nc = s // _C
    bb = min(bb, b)
    assert b % bb == 0
    ng = bb * _RW // _GRP

    def pspec(off):
        return pl.BlockSpec((bb, _C, _RW), lambda bi, c, o=off // _RW: (bi, c, o))

    def full(shape):
        return pl.BlockSpec(shape, lambda bi, c: (0,) * len(shape))

    vec = full((1, _RW))
    lora = full((2 * _RLORA, _RW))
    sq = full((_GRP, _GRP))
    cat = full((_C, _GRP // _RN * _C))
    return pl.pallas_call(
        _rwkv_kernel,
        out_shape=jax.ShapeDtypeStruct((b, s, _RW), BF16),
        grid=(b // bb, nc),
        in_specs=[
            pspec(_O_R), pspec(_O_K), pspec(_O_V), pspec(_O_GA),
            pl.BlockSpec((bb, _C, 128), lambda bi, c: (bi, c, _O_WA // 128)),
            vec, vec, vec, full((1, 2 * _RLORA)),
            vec, lora, vec, lora, vec, vec, vec, vec, vec,
            cat, cat, cat, sq, sq, sq, full((_C, _C)),
        ],
        out_specs=pl.BlockSpec((bb, _C, _RW), lambda bi, c: (bi, c, 0)),
        scratch_shapes=[
            pltpu.VMEM((ng, _GRP, _GRP), F32),
            pltpu.VMEM((8, bb * _RW), F32), pltpu.VMEM((8, bb * _RW), F32),
            pltpu.VMEM((8, bb * _RW), F32), pltpu.VMEM((8, bb * 2 * _RLORA), F32),
        ],
        compiler_params=pltpu.CompilerParams(
            dimension_semantics=("parallel", "arbitrary"),
            vmem_limit_bytes=_VMEM_LIMIT),
        name="rwkv_mix",
    )(p3, p3, p3, p3, p3, mu_r, mu_k, mu_v, mu_wa, w0, w2p, a0, a2p, k_k, k_a, r_k, ln_w, ln_b,
      *_rwkv_consts())


def _moba_block(ii, q_ref, k_ref, v_ref, g_ref, o_ref, kmean_ref):
    blk = _MBLK
    nh = _MW // _MD
    hs = range(nh)
    scale = _MD ** -0.5
    neg = -1e30
    npast = ii * blk

    def hsl(h):
        return slice(h * _MD, (h + 1) * _MD)

    def dot(a, b, dims):
        return lax.dot_general(a, b, dims, preferred_element_type=F32)

    q = [q_ref[:, hsl(h)] for h in hs]
    qs = [(q[h] * scale).astype(BF16) for h in hs]
    own = slice(npast, npast + blk)
    kpos = _iota((blk, blk), 0)
    qpos = _iota((blk, blk), 1)
    s_own = [jnp.where(kpos <= qpos, dot(k_ref[own, hsl(h)].astype(BF16), qs[h], _NT), neg) for h in hs]
    m = [jnp.max(s_own[h], axis=0, keepdims=True) for h in hs]

    if ii > 0:
        npad = kmean_ref.shape[0]
        bid = _iota((npad, blk), 0)
        gate = [jnp.where(bid < ii, _mm(kmean_ref[:, hsl(h)], q[h], _NT, na=3, nb=3), -jnp.inf)
                for h in hs]
        s_past = [dot(k_ref[0:npast, hsl(h)].astype(BF16), qs[h], _NT) for h in hs]
        masked = []
        for h in hs:
            rows = []
            for j in range(ii):
                gj = gate[h][j:j + 1, :]
                ahead = (gate[h] > gj) | ((gate[h] == gj) & (bid < j))
                rank = jnp.sum(jnp.where(ahead, 1.0, 0.0), axis=0, keepdims=True)
                rows.append(jnp.where(rank < float(_MTOPK), s_past[h][j * blk:(j + 1) * blk], neg))
            masked.append(rows[0] if ii == 1 else jnp.concatenate(rows, axis=0))
        m = [jnp.maximum(m[h], jnp.max(masked[h], axis=0, keepdims=True)) for h in hs]
        p_past = [jnp.exp(masked[h] - m[h]) for h in hs]

    p_own = [jnp.exp(s_own[h] - m[h]) for h in hs]
    l = [jnp.sum(p_own[h], axis=0, keepdims=True) for h in hs]
    acc = [dot(v_ref[own, hsl(h)].astype(BF16), p_own[h].astype(BF16), _TN) for h in hs]
    if ii > 0:
        l = [l[h] + jnp.sum(p_past[h], axis=0, keepdims=True) for h in hs]
        acc = [acc[h] + dot(v_ref[0:npast, hsl(h)].astype(BF16), p_past[h].astype(BF16), _TN) for h in hs]

    o = jnp.concatenate([(acc[h] / l[h]).T for h in hs], axis=1)
    o_ref[...] = (o * _silu(g_ref[...])).astype(o_ref.dtype)


def _moba_kernel(q_ref, k_ref, v_ref, g_ref, o_ref, kmean_ref):
    i = pl.program_id(1)
    blk = _MBLK
    nb = k_ref.shape[0] // blk

    @pl.when(i == 0)
    def _():
        kmean_ref[...] = jnp.zeros_like(kmean_ref)
        for j in range(nb):
            kmean_ref[j:j + 1, :] = jnp.mean(k_ref[j * blk:(j + 1) * blk, :], axis=0, keepdims=True)

    for ii in range(nb):
        pl.when(i == ii)(functools.partial(_moba_block, ii, q_ref, k_ref, v_ref, g_ref, o_ref, kmean_ref))


def _moba_mix(p3):
    b, s, _ = p3.shape
    nq = s // _MBLK
    npad = -(-nq // 16) * 16
    return pl.pallas_call(
        _moba_kernel,
        out_shape=jax.ShapeDtypeStruct((b, s, _MW), BF16),
        grid=(b, nq),
        in_specs=[
            pl.BlockSpec((None, _MBLK, _MW), lambda bi, i: (bi, i, _O_MQ // _MW)),
            pl.BlockSpec((None, s, _MW), lambda bi, i: (bi, 0, _O_MK // _MW)),
            pl.BlockSpec((None, s, _MW), lambda bi, i: (bi, 0, _O_MV // _MW)),
            pl.BlockSpec((None, _MBLK, _MW), lambda bi, i: (bi, i, _O_MG // _MW)),
        ],
        out_specs=pl.BlockSpec((None, _MBLK, _MW), lambda bi, i: (bi, i, 0)),
        scratch_shapes=[pltpu.VMEM((npad, _MW), F32)],
        compiler_params=pltpu.CompilerParams(
            dimension_semantics=("parallel", "arbitrary"),
            vmem_limit_bytes=_VMEM_LIMIT),
        name="moba_mix",
    )(p3, p3, p3, p3)


def _gla_kernel(q_ref, k_ref, v_ref, g_ref, low_ref, aup_ref, ab_ref, nw_ref, incl_ref, tri_ref,
                o_ref, st_ref):
    c = _C
    nh = _GKW // _GDK
    bs = range(q_ref.shape[0])

    @pl.when(pl.program_id(1) == 0)
    def _():
        st_ref[...] = jnp.zeros_like(st_ref)

    incl, tri = incl_ref[...], tri_ref[...]
    z = [_mm(low_ref[b], aup_ref[...], na=2, nb=2) + ab_ref[...] for b in bs]
    la = [-_softplus(-z[b]) * (1.0 / _GTEMP) for b in bs]
    bcum = [_mm(tri, la[b], nb=_P_SUM) for b in bs]
    b_last = [bcum[b][c - 1:c, :] for b in bs]
    k = [k_ref[b] for b in bs]
    qe = [_parts(_stack_heads(q_ref[b] * (_GDK ** -0.5) * jnp.exp(bcum[b]), _GDK, nh), 1) for b in bs]
    ke = [_stack_heads(k[b] * jnp.exp(-bcum[b]), _GDK, nh) for b in bs]
    kd = [_stack_heads(k[b] * jnp.exp(b_last[b] - bcum[b]), _GDK, nh) for b in bs]
    v_s = [_parts(_stack_heads(v_ref[b], _GDV, nh), 1) for b in bs]

    att = [_mm(qe[b], ke[b], _NT) * incl for b in bs]
    st = [st_ref[b] for b in bs]
    o_s = [_mm(att[b], v_s[b]) + _mm(qe[b], st[b], _NT) for b in bs]
    for b in bs:
        st_ref[b] = st[b] * jnp.exp(b_last[b]) + _mm(v_s[b], kd[b], _TN)

    for b in bs:
        o = _unstack_heads(o_s[b], c, nh)
        outs = []
        for h in range(nh):
            oh = o[:, h * _GDV:(h + 1) * _GDV]
            ms = jnp.mean(oh * oh, axis=-1, keepdims=True)
            outs.append(oh * lax.rsqrt(ms + _EPS) * nw_ref[...])
        o_ref[b] = (jnp.concatenate(outs, axis=1) * _silu(g_ref[b])).astype(o_ref.dtype)


def _gla_mix(p3, a_up_p, a_b, norm_w, bb=8):
    b, s, _ = p3.shape
    nc = s // _C
    bb = min(bb, b)
    assert b % bb == 0
    n = _GKW // _GDK * _C
    ri = lax.broadcasted_iota(jnp.int32, (n, n), 0)
    ci = lax.broadcasted_iota(jnp.int32, (n, n), 1)
    incl = (((ri // _C) == (ci // _C)) & (ci <= ri)).astype(F32)
    tri = (lax.broadcasted_iota(jnp.int32, (_C, _C), 0) >= lax.broadcasted_iota(jnp.int32, (_C, _C), 1)).astype(BF16)

    def full(shape):
        return pl.BlockSpec(shape, lambda bi, c: (0,) * len(shape))

    return pl.pallas_call(
        _gla_kernel,
        out_shape=jax.ShapeDtypeStruct((b, s, _GW), BF16),
        grid=(b // bb, nc),
        in_specs=[
            pl.BlockSpec((bb, _C, _GKW), lambda bi, c: (bi, c, _O_GQ // _GKW)),
            pl.BlockSpec((bb, _C, _GKW), lambda bi, c: (bi, c, _O_GK // _GKW)),
            pl.BlockSpec((bb, _C, _GW), lambda bi, c: (bi, c, _O_GV // _GW)),
            pl.BlockSpec((bb, _C, _GW), lambda bi, c: (bi, c, _O_GG // _GW)),
            pl.BlockSpec((bb, _C, 128), lambda bi, c: (bi, c, _O_GL // 128)),
            full((128, _GKW)), full((1, _GKW)), full((1, _GDV)), full((n, n)), full((_C, _C)),
        ],
        out_specs=pl.BlockSpec((bb, _C, _GW), lambda bi, c: (bi, c, 0)),
        scratch_shapes=[pltpu.VMEM((bb, _GW, _GKW), F32)],
        compiler_params=pltpu.CompilerParams(
            dimension_semantics=("parallel", "arbitrary"),
            vmem_limit_bytes=_VMEM_LIMIT),
        name="gla_mix",
    )(p3, p3, p3, p3, p3, a_up_p, a_b, norm_w, incl, tri)


def _outproj_kernel(x_ref, ya_ref, yb_ref, yc_ref, w_ref, fw_ref, o_ref, *, final_norm):
    acc = x_ref[...]
    acc = acc + jnp.dot(ya_ref[...], w_ref[0:_RW, :], preferred_element_type=F32)
    acc = acc + jnp.dot(yb_ref[...], w_ref[_RW:_RW + _MW, :], preferred_element_type=F32)
    acc = acc + jnp.dot(yc_ref[...], w_ref[_RW + _MW:, :], preferred_element_type=F32)
    if final_norm:
        ms = jnp.mean(acc * acc, axis=-1, keepdims=True)
        acc = acc * lax.rsqrt(ms + _EPS) * fw_ref[...]
    o_ref[...] = acc


def _out_proj(x2, ya, yb, yc, w_bf16, fw, final_norm, tm=512):
    t, d = x2.shape
    return pl.pallas_call(
        functools.partial(_outproj_kernel, final_norm=final_norm),
        out_shape=jax.ShapeDtypeStruct((t, d), F32),
        grid=(t // tm,),
        in_specs=[
            pl.BlockSpec((tm, d), lambda i: (i, 0)),
            pl.BlockSpec((tm, _RW), lambda i: (i, 0)),
            pl.BlockSpec((tm, _MW), lambda i: (i, 0)),
            pl.BlockSpec((tm, _GW), lambda i: (i, 0)),
            pl.BlockSpec((d, d), lambda i: (0, 0)),
            pl.BlockSpec((1, d), lambda i: (0, 0)),
        ],
        out_specs=pl.BlockSpec((tm, d), lambda i: (i, 0)),
        compiler_params=pltpu.CompilerParams(
            dimension_semantics=("parallel",),
            vmem_limit_bytes=_VMEM_LIMIT),
        name="out_proj",
    )(x2, ya, yb, yc, w_bf16, fw)


_WTILE = 128


def _wprep_src_tile(j):
    src = jnp.int32(_D_IN // _WTILE)
    for a, b, o in _RUNS:
        lo = o // _WTILE
        hi = lo + -(-(b - a) // _WTILE)
        src = jnp.where((j >= lo) & (j < hi), j + (a - o) // _WTILE, src)
    return src


def _wprep_kernel(w_ref, o_ref):
    j = pl.program_id(1)
    last_dst = _RUNS[-1][2] + _RUNS[-1][1] - _RUNS[-1][0]
    valid = jnp.clip(last_dst - j * _WTILE, 0, _WTILE)
    x = w_ref[...]
    o_ref[...] = jnp.where(_iota(x.shape, 0) < valid, x, 0.0).astype(BF16)


def _prep_w_in(w_t):
    depth, n, d = w_t.shape
    assert n == _D_IN and all(a % _WTILE == 0 and o % _WTILE == 0 for a, _, o in _RUNS)
    return pl.pallas_call(
        _wprep_kernel,
        out_shape=jax.ShapeDtypeStruct((depth, _NP, d), BF16),
        grid=(depth, _NP // _WTILE),
        in_specs=[pl.BlockSpec((None, _WTILE, d), lambda l, j: (l, _wprep_src_tile(j), 0))],
        out_specs=pl.BlockSpec((None, _WTILE, d), lambda l, j: (l, j, 0)),
        compiler_params=pltpu.CompilerParams(
            dimension_semantics=("parallel", "parallel"),
            vmem_limit_bytes=_VMEM_LIMIT),
        name="w_prep",
    )(w_t)


def kernel(x, norm_w, w_in, w_out, rwkv_mu, rwkv_w0, rwkv_w2, rwkv_a0, rwkv_a2, rwkv_k_k, rwkv_k_a,
           rwkv_r_k, rwkv_ln_w, rwkv_ln_b, gla_a_up, gla_a_b, gla_norm_w, final_norm_w):
    b, s, d = x.shape
    depth = norm_w.shape[0]
    x2 = x.reshape(b * s, d)
    zero_lora = jnp.zeros((_RLORA, _RW), F32)
    fw = final_norm_w.reshape(1, d)
    w_in_p = _prep_w_in(jnp.swapaxes(w_in, 1, 2))
    for l in range(depth):
        w_out_l = w_out[l].astype(BF16)
        mu = rwkv_mu[l]
        mu_r, mu_k, mu_v = (mu[i * _RW:(i + 1) * _RW].reshape(1, _RW) for i in range(3))
        mu_wa = mu[3 * _RW:].reshape(1, 2 * _RLORA)
        w2p = jnp.concatenate([rwkv_w2[l], zero_lora], axis=0)
        a2p = jnp.concatenate([zero_lora, rwkv_a2[l]], axis=0)
        a_up_p = jnp.concatenate([gla_a_up[l], jnp.zeros((128 - _GRANK, _GKW), F32)], axis=0)

        p = _in_proj(x2, norm_w[l].reshape(1, d), w_in_p, l)
        p3 = p.reshape(b, s, _NP)
        row = lambda a: a.reshape(1, _RW)
        y_a = _rwkv_mix(p3, mu_r, mu_k, mu_v, mu_wa, row(rwkv_w0[l]), w2p, row(rwkv_a0[l]), a2p,
                        row(rwkv_k_k[l]), row(rwkv_k_a[l]), row(rwkv_r_k[l]),
                        row(rwkv_ln_w[l]), row(rwkv_ln_b[l]))
        y_b = _moba_mix(p3)
        y_c = _gla_mix(p3, a_up_p, gla_a_b[l].reshape(1, _GKW), gla_norm_w[l].reshape(1, _GDV))
        x2 = _out_proj(x2, y_a.reshape(b * s, _RW), y_b.reshape(b * s, _MW), y_c.reshape(b * s, _GW),
                       w_out_l, fw, final_norm=(l == depth - 1))
    return x2.reshape(b, s, d)
```

```python
import functools

import jax
import jax.numpy as jnp
from jax import lax
from jax.experimental import pallas as pl
from jax.experimental.pallas import tpu as pltpu

F32 = jnp.float32
BF16 = jnp.bfloat16

_D = 2048
_RW = 1024
_RN = 64
_RLORA = 64
_GN_EPS = 64e-5
_MW = 512
_MD = 128
_MBLK = 256
_MTOPK = 3
_GW = 512
_GKW = 256
_GDK = 64
_GDV = 128
_GRANK = 16
_GTEMP = 16.0
_EPS = 1e-6
_C = 64
_GRP = 256

_D_IN = 7824
_RUNS = ((0, 3072, 0),
         (3200, 4224, 3072),
         (4224, 6272, 4096),
         (6784, 7808, 6144),
         (6272, 6784, 7168),
         (3072, 3200, 7680),
         (7808, 7824, 7808))
_O_R, _O_K, _O_V, _O_GA = 0, 1024, 2048, 3072
_O_MQ, _O_MK, _O_MV, _O_MG = 4096, 4608, 5120, 5632
_O_GV, _O_GG = 6144, 6656
_O_GQ, _O_GK = 7168, 7424
_O_WA, _O_GL = 7680, 7808
_NP = 7936
assert all(o % _GW == 0 for o in (_O_GV, _O_GG))
assert all(o % _RW == 0 for o in (_O_R, _O_K, _O_V, _O_GA))
assert all(o % _GKW == 0 for o in (_O_GQ, _O_GK))
assert all(o % _MW == 0 for o in (_O_MQ, _O_MK, _O_MV, _O_MG))
assert all(o % 128 == 0 for o in (_O_WA, _O_GL))

_NN = (((1,), (0,)), ((), ()))
_NT = (((1,), (1,)), ((), ()))
_TN = (((0,), (0,)), ((), ()))

_VMEM_LIMIT = 56 * 1024 * 1024
_EXP_M05 = 0.6065306597126334

_P_INV = 1
_P_MAIN = 1
_P_STATE = 1
_P_SUM = 2
_P_HEADSUM = 1


def _parts(x, n):
    if x.dtype == BF16:
        return [x]
    out, rem = [], x
    for i in range(n):
        p = rem.astype(BF16)
        out.append(p)
        if i + 1 < n:
            rem = rem - p.astype(F32)
    return out


def _mm(a, b, dims=_NN, na=1, nb=1):
    pa = a if isinstance(a, list) else _parts(a, na)
    pb = b if isinstance(b, list) else _parts(b, nb)
    order = max(len(pa), len(pb))
    free_axis = 1 if dims[0][0] == (0,) else 0
    m = pa[0].shape[free_axis]
    acc = None
    for j, y in enumerate(pb):
        xs = pa[:order - j]
        if not xs:
            continue
        x = xs[0] if len(xs) == 1 else jnp.concatenate(xs, axis=free_axis)
        t = lax.dot_general(x, y, dims, preferred_element_type=F32)
        for i in range(len(xs)):
            blk = t[i * m:(i + 1) * m]
            acc = blk if acc is None else acc + blk
    return acc


def _softplus(x):
    return jnp.maximum(x, 0.0) + jnp.log(1.0 + jnp.exp(-jnp.abs(x)))


def _sigmoid(x):
    return 1.0 / (1.0 + jnp.exp(-x))


def _silu(x):
    return x * _sigmoid(x)


def _iota(shape, dim):
    return lax.broadcasted_iota(jnp.int32, shape, dim)


def _idiv(x, n):
    assert n & (n - 1) == 0
    return jnp.right_shift(x, n.bit_length() - 1)


def _stack_heads(x, head_w, n_heads):
    hid = _idiv(_iota(x.shape, 1), head_w)
    return jnp.concatenate([jnp.where(hid == g, x, 0.0) for g in range(n_heads)], axis=0)


def _unstack_heads(xs, c, n_heads):
    out = xs[0:c]
    for g in range(1, n_heads):
        out = out + xs[g * c:(g + 1) * c]
    return out


def _inproj_kernel(x_ref, nw_ref, w_ref, o_ref):
    x = x_ref[...]
    ms = jnp.mean(x * x, axis=-1, keepdims=True)
    h = (x * lax.rsqrt(ms + _EPS) * nw_ref[...]).astype(BF16)
    o_ref[...] = lax.dot_general(h, w_ref[...], _NT, preferred_element_type=F32)


def _in_proj(x2, nw, w_bf16, layer, tm=256):
    t, d = x2.shape
    n = w_bf16.shape[1]
    return pl.pallas_call(
        _inproj_kernel,
        out_shape=jax.ShapeDtypeStruct((t, n), F32),
        grid=(t // tm,),
        in_specs=[
            pl.BlockSpec((tm, d), lambda i: (i, 0)),
            pl.BlockSpec((1, d), lambda i: (0, 0)),
            pl.BlockSpec((None, n, d), lambda i: (layer, 0, 0), pipeline_mode=pl.Buffered(1)),
        ],
        out_specs=pl.BlockSpec((tm, n), lambda i: (i, 0)),
        compiler_params=pltpu.CompilerParams(
            dimension_semantics=("parallel",),
            vmem_limit_bytes=_VMEM_LIMIT),
        name="in_proj",
    )(x2, nw, w_bf16)


def _token_shift(x, prev_ref, mu):
    c = x.shape[0]
    rows = _iota(x.shape, 0)
    xp = jnp.where(rows == 0, prev_ref[7:8, :], pltpu.roll(x, 1, 0))
    prev_ref[...] = x[c - 8:c, :]
    return x + mu * (xp - x)


def _head_sums(x, head_ones):
    c = x.shape[0]
    ng = x.shape[1] // _GRP
    rows = [p[:, g * _GRP:(g + 1) * _GRP] for p in _parts(x, _P_HEADSUM) for g in range(ng)]
    t = lax.dot_general(jnp.concatenate(rows, axis=0), head_ones, _NN, preferred_element_type=F32)
    cols = []
    for g in range(ng):
        acc = t[g * c:(g + 1) * c]
        for i in range(1, _P_HEADSUM):
            acc = acc + t[(i * ng + g) * c:(i * ng + g + 1) * c]
        cols.append(acc)
    return jnp.concatenate(cols, axis=1)


def _stack_lanes(x, lane_masks):
    return jnp.concatenate([x * m for m in lane_masks], axis=0)


def _rwkv_groups(at, rt, bt, kt, bh, kh, v, w_end, st_ref, strict, incl, eye_c, eye, same):
    c = _C
    nh = _GRP // _RN
    gs = range(at.shape[1] // _GRP)
    lane_head = _idiv(_iota((1, _GRP), 1), _RN)
    hm = [jnp.where(lane_head == h, 1.0, 0.0).astype(BF16) for h in range(nh)]

    def grp(x, g):
        return x[:, g * _GRP:(g + 1) * _GRP]

    def bf(x):
        return x.astype(BF16)

    def dot(a, b, dims=_NN):
        return lax.dot_general(a, b, dims, preferred_element_type=F32)

    def stack(x):
        return _stack_lanes(bf(x), hm)

    at_b = [bf(grp(at, g)) for g in gs]
    v_b = [bf(grp(v, g)) for g in gs]
    bh_b = [bf(grp(bh, g)) for g in gs]
    lhs = [jnp.concatenate([at_b[g], bf(grp(rt, g))], axis=0) for g in gs]
    rhs = [jnp.concatenate([stack(grp(bt, g)), stack(grp(kt, g))], axis=0) for g in gs]
    a_all = [dot(lhs[g], rhs[g], _NT) for g in gs]
    a_ab = [a_all[g][:c, :nh * c] * strict for g in gs]
    a_ak = [bf(a_all[g][:c, nh * c:] * strict) for g in gs]
    a_rb = [bf(a_all[g][c:, :nh * c] * incl) for g in gs]
    a_rk = [bf(a_all[g][c:, nh * c:] * incl) for g in gs]

    x = [eye_c + a_ab[g] for g in gs]
    p_b = [bf(a_ab[g]) for g in gs]
    p = [dot(p_b[g], _stack_lanes(p_b[g], hm)) for g in gs]
    n_sq = c.bit_length() - 2
    for it in range(n_sq):
        p_bd = [stack(p[g]) for g in gs]
        if it + 1 < n_sq:
            xp = [dot(jnp.concatenate([bf(x[g]), bf(p[g])], axis=0), p_bd[g]) for g in gs]
            x = [x[g] + xp[g][:c] for g in gs]
            p = [xp[g][c:] for g in gs]
        else:
            x = [x[g] + dot(bf(x[g]), p_bd[g]) for g in gs]
    t = [bf(x[g]) for g in gs]

    v_s = [_stack_lanes(v_b[g], hm) for g in gs]
    abar = [dot(t[g], _stack_lanes(at_b[g], hm)) for g in gs]
    uv = [dot(a_ak[g], v_s[g]) for g in gs]
    ubar = [dot(t[g], stack(uv[g])) for g in gs]
    abar_b = [bf(abar[g]) for g in gs]
    ubar_b = [bf(ubar[g]) for g in gs]
    rbar = [grp(rt, g) + dot(a_rb[g], _stack_lanes(abar_b[g], hm)) for g in gs]
    ybar = [dot(a_rb[g], _stack_lanes(ubar_b[g], hm)) + dot(a_rk[g], v_s[g]) for g in gs]
    mt = [eye * grp(w_end, g) + same * dot(bh_b[g], abar_b[g], _TN) for g in gs]
    dt = [same * dot(jnp.concatenate([bh_b[g], bf(grp(kh, g))], axis=0),
                     jnp.concatenate([ubar_b[g], v_b[g]], axis=0), _TN) for g in gs]

    stp = [_parts(st_ref[g], _P_STATE) for g in gs]
    y = [_mm(rbar[g], stp[g], na=_P_STATE) + ybar[g] for g in gs]
    for g in gs:
        st_ref[g] = _mm(mt[g], stp[g], na=_P_STATE, nb=_P_STATE) + dt[g]
    return jnp.concatenate(y, axis=1)


def _rwkv_kernel(pr_ref, pk_ref, pv_ref, pg_ref, pwa_ref,
                 mur_ref, muk_ref, muv_ref, muwa_ref,
                 w0_ref, w2_ref, a0_ref, a2_ref, kk_ref, ka_ref, rk_ref, lnw_ref, lnb_ref,
                 strict_ref, incl_ref, eyec_ref, eye_ref, same_ref, hones_ref, tri_ref,
                 o_ref,
                 s_ref, prev_r, prev_k, prev_v, prev_wa):
    c = _C
    nb = pr_ref.shape[0]

    @pl.when(pl.program_id(1) == 0)
    def _():
        s_ref[...] = jnp.zeros_like(s_ref)
        prev_r[...] = jnp.zeros_like(prev_r)
        prev_k[...] = jnp.zeros_like(prev_k)
        prev_v[...] = jnp.zeros_like(prev_v)
        prev_wa[...] = jnp.zeros_like(prev_wa)

    head_ones = hones_ref[...]

    def lanes(ref):
        return jnp.concatenate([ref[b] for b in range(nb)], axis=1)

    def rep(ref):
        return jnp.concatenate([ref[...]] * nb, axis=1)

    def unrows(x):
        return jnp.concatenate([x[b * c:(b + 1) * c] for b in range(nb)], axis=1)

    r = _token_shift(lanes(pr_ref), prev_r, rep(mur_ref))
    k = _token_shift(lanes(pk_ref), prev_k, rep(muk_ref))
    v = _token_shift(lanes(pv_ref), prev_v, rep(muv_ref))
    wa = _token_shift(lanes(pwa_ref), prev_wa, rep(muwa_ref))
    wa = jnp.concatenate([wa[:, b * 128:(b + 1) * 128] for b in range(nb)], axis=0)

    wdec = rep(w0_ref) + unrows(_mm(jnp.tanh(wa), w2_ref[...], na=2))
    lw = -_EXP_M05 * _sigmoid(wdec)
    a_lr = _sigmoid(rep(a0_ref) + unrows(_mm(wa, a2_ref[...], na=2)))
    kk = k * rep(kk_ref)
    kk = kk * lax.rsqrt(jnp.maximum(_head_sums(kk * kk, head_ones), 1e-24))
    k2 = k * (1.0 + (a_lr - 1.0) * rep(ka_ref))
    bb = kk * a_lr
    cl = _mm(tri_ref[...], lw, nb=_P_SUM)
    cl_end = cl[c - 1:c, :]
    e_in = jnp.exp(cl)
    e_neg = jnp.exp(-cl)
    w_end = jnp.exp(cl_end)
    e_end = w_end * e_neg
    at = -kk * jnp.exp(cl - lw)
    rt = r * e_in
    bt = bb * e_neg
    kt = k2 * e_neg
    bh = bb * e_end
    kh = k2 * e_end

    y = _rwkv_groups(at, rt, bt, kt, bh, kh, v, w_end, s_ref,
                     strict_ref[...], incl_ref[...], eyec_ref[...], eye_ref[...], same_ref[...])

    inv_n = 1.0 / _RN
    mean = _head_sums(y, head_ones) * inv_n
    yc = y - mean
    var = _head_sums(yc * yc, head_ones) * inv_n
    yn = yc * lax.rsqrt(var + _GN_EPS) * rep(lnw_ref) + rep(lnb_ref)
    bonus = _head_sums(r * k2 * rep(rk_ref), head_ones) * v
    out = yn + bonus
    for b in range(nb):
        o_ref[b] = (out[:, b * _RW:(b + 1) * _RW] * _silu(pg_ref[b])).astype(o_ref.dtype)


def _rwkv_consts():
    n = _GRP
    t = lax.broadcasted_iota(jnp.int32, (_C, n // _RN * _C), 0)
    s = lax.broadcasted_iota(jnp.int32, (_C, n // _RN * _C), 1) % _C
    strict = (s < t).astype(F32)
    incl = (s <= t).astype(F32)
    eye_c = (s == t).astype(F32)
    ri = lax.broadcasted_iota(jnp.int32, (n, n), 0)
    ci = lax.broadcasted_iota(jnp.int32, (n, n), 1)
    eye = (ri == ci).astype(F32)
    same = (ri // _RN) == (ci // _RN)
    ti = lax.broadcasted_iota(jnp.int32, (_C, _C), 0)
    tj = lax.broadcasted_iota(jnp.int32, (_C, _C), 1)
    tri = (ti >= tj).astype(BF16)
    return strict, incl, eye_c, eye, same.astype(F32), same.astype(BF16), tri


def _rwkv_mix(p3, mu_r, mu_k, mu_v, mu_wa, w0, w2p, a0, a2p, k_k, k_a, r_k, ln_w, ln_b, bb=4):
    b, s, _ = p3.shape
    nc = s // _C
    bb = min(bb, b)
    assert b % bb == 0
    ng = bb * _RW // _GRP

    def pspec(off):
        return pl.BlockSpec((bb, _C, _RW), lambda bi, c, o=off // _RW: (bi, c, o))

    def full(shape):
        return pl.BlockSpec(shape, lambda bi, c: (0,) * len(shape))

    vec = full((1, _RW))
    lora = full((2 * _RLORA, _RW))
    sq = full((_GRP, _GRP))
    cat = full((_C, _GRP // _RN * _C))
    return pl.pallas_call(
        _rwkv_kernel,
        out_shape=jax.ShapeDtypeStruct((b, s, _RW), BF16),
        grid=(b // bb, nc),
        in_specs=[
            pspec(_O_R), pspec(_O_K), pspec(_O_V), pspec(_O_GA),
            pl.BlockSpec((bb, _C, 128), lambda bi, c: (bi, c, _O_WA // 128)),
            vec, vec, vec, full((1, 2 * _RLORA)),
            vec, lora, vec, lora, vec, vec, vec, vec, vec,
            cat, cat, cat, sq, sq, sq, full((_C, _C)),
        ],
        out_specs=pl.BlockSpec((bb, _C, _RW), lambda bi, c: (bi, c, 0)),
        scratch_shapes=[
            pltpu.VMEM((ng, _GRP, _GRP), F32),
            pltpu.VMEM((8, bb * _RW), F32), pltpu.VMEM((8, bb * _RW), F32),
            pltpu.VMEM((8, bb * _RW), F32), pltpu.VMEM((8, bb * 2 * _RLORA), F32),
        ],
        compiler_params=pltpu.CompilerParams(
            dimension_semantics=("parallel", "arbitrary"),
            vmem_limit_bytes=_VMEM_LIMIT),
        name="rwkv_mix",
    )(p3, p3, p3, p3, p3, mu_r, mu_k, mu_v, mu_wa, w0, w2p, a0, a2p, k_k, k_a, r_k, ln_w, ln_b,
      *_rwkv_consts())


def _moba_block(ii, q_ref, k_ref, v_ref, g_ref, o_ref, kmean_ref):
    blk = _MBLK
    nh = _MW // _MD
    hs = range(nh)
    scale = _MD ** -0.5
    neg = -1e30
    npast = ii * blk

    def hsl(h):
        return slice(h * _MD, (h + 1) * _MD)

    def dot(a, b, dims):
        return lax.dot_general(a, b, dims, preferred_element_type=F32)

    q = [q_ref[:, hsl(h)] for h in hs]
    qs = [(q[h] * scale).astype(BF16) for h in hs]
    own = slice(npast, npast + blk)
    kpos = _iota((blk, blk), 0)
    qpos = _iota((blk, blk), 1)
    s_own = [jnp.where(kpos <= qpos, dot(k_ref[own, hsl(h)].astype(BF16), qs[h], _NT), neg) for h in hs]
    m = [jnp.max(s_own[h], axis=0, keepdims=True) for h in hs]

    if ii > 0:
        npad = kmean_ref.shape[0]
        bid = _iota((npad, blk), 0)
        gate = [jnp.where(bid < ii, _mm(kmean_ref[:, hsl(h)], q[h], _NT, na=2, nb=2), -jnp.inf)
                for h in hs]
        s_past = [dot(k_ref[0:npast, hsl(h)].astype(BF16), qs[h], _NT) for h in hs]
        masked = []
        for h in hs:
            rows = []
            for j in range(ii):
                gj = gate[h][j:j + 1, :]
                ahead = (gate[h] > gj) | ((gate[h] == gj) & (bid < j))
                rank = jnp.sum(jnp.where(ahead, 1.0, 0.0), axis=0, keepdims=True)
                rows.append(jnp.where(rank < float(_MTOPK), s_past[h][j * blk:(j + 1) * blk], neg))
            masked.append(rows[0] if ii == 1 else jnp.concatenate(rows, axis=0))
        m = [jnp.maximum(m[h], jnp.max(masked[h], axis=0, keepdims=True)) for h in hs]
        p_past = [jnp.exp(masked[h] - m[h]) for h in hs]

    p_own = [jnp.exp(s_own[h] - m[h]) for h in hs]
    l = [jnp.sum(p_own[h], axis=0, keepdims=True) for h in hs]
    acc = [dot(v_ref[own, hsl(h)].astype(BF16), p_own[h].astype(BF16), _TN) for h in hs]
    if ii > 0:
        l = [l[h] + jnp.sum(p_past[h], axis=0, keepdims=True) for h in hs]
        acc = [acc[h] + dot(v_ref[0:npast, hsl(h)].astype(BF16), p_past[h].astype(BF16), _TN) for h in hs]

    o = jnp.concatenate([(acc[h] / l[h]).T for h in hs], axis=1)
    o_ref[...] = (o * _silu(g_ref[...])).astype(o_ref.dtype)


def _moba_kernel(q_ref, k_ref, v_ref, g_ref, o_ref, kmean_ref):
    i = pl.program_id(1)
    blk = _MBLK
    nb = k_ref.shape[0] // blk

    @pl.when(i == 0)
    def _():
        kmean_ref[...] = jnp.zeros_like(kmean_ref)
        for j in range(nb):
            kmean_ref[j:j + 1, :] = jnp.mean(k_ref[j * blk:(j + 1) * blk, :], axis=0, keepdims=True)

    for ii in range(nb):
        pl.when(i == ii)(functools.partial(_moba_block, ii, q_ref, k_ref, v_ref, g_ref, o_ref, kmean_ref))


def _moba_mix(p3):
    b, s, _ = p3.shape
    nq = s // _MBLK
    npad = -(-nq // 16) * 16
    return pl.pallas_call(
        _moba_kernel,
        out_shape=jax.ShapeDtypeStruct((b, s, _MW), BF16),
        grid=(b, nq),
        in_specs=[
            pl.BlockSpec((None, _MBLK, _MW), lambda bi, i: (bi, i, _O_MQ // _MW)),
            pl.BlockSpec((None, s, _MW), lambda bi, i: (bi, 0, _O_MK // _MW)),
            pl.BlockSpec((None, s, _MW), lambda bi, i: (bi, 0, _O_MV // _MW)),
            pl.BlockSpec((None, _MBLK, _MW), lambda bi, i: (bi, i, _O_MG // _MW)),
        ],
        out_specs=pl.BlockSpec((None, _MBLK, _MW), lambda bi, i: (bi, i, 0)),
        scratch_shapes=[pltpu.VMEM((npad, _MW), F32)],
        compiler_params=pltpu.CompilerParams(
            dimension_semantics=("parallel", "arbitrary"),
            vmem_limit_bytes=_VMEM_LIMIT),
        name="moba_mix",
    )(p3, p3, p3, p3)


def _gla_kernel(q_ref, k_ref, v_ref, g_ref, low_ref, aup_ref, ab_ref, nw_ref, incl_ref, tri_ref,
                o_ref, st_ref):
    c = _C
    nh = _GKW // _GDK
    bs = range(q_ref.shape[0])

    @pl.when(pl.program_id(1) == 0)
    def _():
        st_ref[...] = jnp.zeros_like(st_ref)

    incl, tri = incl_ref[...], tri_ref[...]
    z = [_mm(low_ref[b], aup_ref[...], na=2, nb=2) + ab_ref[...] for b in bs]
    la = [-_softplus(-z[b]) * (1.0 / _GTEMP) for b in bs]
    bcum = [_mm(tri, la[b], nb=_P_SUM) for b in bs]
    b_last = [bcum[b][c - 1:c, :] for b in bs]
    k = [k_ref[b] for b in bs]
    qe = [_parts(_stack_heads(q_ref[b] * (_GDK ** -0.5) * jnp.exp(bcum[b]), _GDK, nh), 1) for b in bs]
    ke = [_stack_heads(k[b] * jnp.exp(-bcum[b]), _GDK, nh) for b in bs]
    kd = [_stack_heads(k[b] * jnp.exp(b_last[b] - bcum[b]), _GDK, nh) for b in bs]
    v_s = [_parts(_stack_heads(v_ref[b], _GDV, nh), 1) for b in bs]

    att = [_mm(qe[b], ke[b], _NT) * incl for b in bs]
    st = [st_ref[b] for b in bs]
    o_s = [_mm(att[b], v_s[b]) + _mm(qe[b], st[b], _NT) for b in bs]
    for b in bs:
        st_ref[b] = st[b] * jnp.exp(b_last[b]) + _mm(v_s[b], kd[b], _TN)

    for b in bs:
        o = _unstack_heads(o_s[b], c, nh)
        outs = []
        for h in range(nh):
            oh = o[:, h * _GDV:(h + 1) * _GDV]
            ms = jnp.mean(oh * oh, axis=-1, keepdims=True)
            outs.append(oh * lax.rsqrt(ms + _EPS) * nw_ref[...])
        o_ref[b] = (jnp.concatenate(outs, axis=1) * _silu(g_ref[b])).astype(o_ref.dtype)


def _gla_mix(p3, a_up_p, a_b, norm_w, bb=8):
    b, s, _ = p3.shape
    nc = s // _C
    bb = min(bb, b)
    assert b % bb == 0
    n = _GKW // _GDK * _C
    ri = lax.broadcasted_iota(jnp.int32, (n, n), 0)
    ci = lax.broadcasted_iota(jnp.int32, (n, n), 1)
    incl = (((ri // _C) == (ci // _C)) & (ci <= ri)).astype(F32)
    tri = (lax.broadcasted_iota(jnp.int32, (_C, _C), 0) >= lax.broadcasted_iota(jnp.int32, (_C, _C), 1)).astype(BF16)

    def full(shape):
        return pl.BlockSpec(shape, lambda bi, c: (0,) * len(shape))

    return pl.pallas_call(
        _gla_kernel,
        out_shape=jax.ShapeDtypeStruct((b, s, _GW), BF16),
        grid=(b // bb, nc),
        in_specs=[
            pl.BlockSpec((bb, _C, _GKW), lambda bi, c: (bi, c, _O_GQ // _GKW)),
            pl.BlockSpec((bb, _C, _GKW), lambda bi, c: (bi, c, _O_GK // _GKW)),
            pl.BlockSpec((bb, _C, _GW), lambda bi, c: (bi, c, _O_GV // _GW)),
            pl.BlockSpec((bb, _C, _GW), lambda bi, c: (bi, c, _O_GG // _GW)),
            pl.BlockSpec((bb, _C, 128), lambda bi, c: (bi, c, _O_GL // 128)),
            full((128, _GKW)), full((1, _GKW)), full((1, _GDV)), full((n, n)), full((_C, _C)),
        ],
        out_specs=pl.BlockSpec((bb, _C, _GW), lambda bi, c: (bi, c, 0)),
        scratch_shapes=[pltpu.VMEM((bb, _GW, _GKW), F32)],
        compiler_params=pltpu.CompilerParams(
            dimension_semantics=("parallel", "arbitrary"),
            vmem_limit_bytes=_VMEM_LIMIT),
        name="gla_mix",
    )(p3, p3, p3, p3, p3, a_up_p, a_b, norm_w, incl, tri)


def _outproj_kernel(x_ref, ya_ref, yb_ref, yc_ref, w_ref, fw_ref, o_ref, *, final_norm):
    acc = x_ref[...]
    acc = acc + jnp.dot(ya_ref[...], w_ref[0:_RW, :], preferred_element_type=F32)
    acc = acc + jnp.dot(yb_ref[...], w_ref[_RW:_RW + _MW, :], preferred_element_type=F32)
    acc = acc + jnp.dot(yc_ref[...], w_ref[_RW + _MW:, :], preferred_element_type=F32)
    if final_norm:
        ms = jnp.mean(acc * acc, axis=-1, keepdims=True)
        acc = acc * lax.rsqrt(ms + _EPS) * fw_ref[...]
    o_ref[...] = acc


def _out_proj(x2, ya, yb, yc, w_bf16, fw, final_norm, tm=512):
    t, d = x2.shape
    return pl.pallas_call(
        functools.partial(_outproj_kernel, final_norm=final_norm),
        out_shape=jax.ShapeDtypeStruct((t, d), F32),
        grid=(t // tm,),
        in_specs=[
            pl.BlockSpec((tm, d), lambda i: (i, 0)),
            pl.BlockSpec((tm, _RW), lambda i: (i, 0)),
            pl.BlockSpec((tm, _MW), lambda i: (i, 0)),
            pl.BlockSpec((tm, _GW), lambda i: (i, 0)),
            pl.BlockSpec((d, d), lambda i: (0, 0)),
            pl.BlockSpec((1, d), lambda i: (0, 0)),
        ],
        out_specs=pl.BlockSpec((tm, d), lambda i: (i, 0)),
        compiler_params=pltpu.CompilerParams(
            dimension_semantics=("parallel",),
            vmem_limit_bytes=_VMEM_LIMIT),
        name="out_proj",
    )(x2, ya, yb, yc, w_bf16, fw)


_WTILE = 128


def _wprep_src_tile(j):
    src = jnp.int32(_D_IN // _WTILE)
    for a, b, o in _RUNS:
        lo = o // _WTILE
        hi = lo + -(-(b - a) // _WTILE)
        src = jnp.where((j >= lo) & (j < hi), j + (a - o) // _WTILE, src)
    return src


def _wprep_kernel(w_ref, o_ref):
    j = pl.program_id(1)
    last_dst = _RUNS[-1][2] + _RUNS[-1][1] - _RUNS[-1][0]
    valid = jnp.clip(last_dst - j * _WTILE, 0, _WTILE)
    x = w_ref[...]
    o_ref[...] = jnp.where(_iota(x.shape, 0) < valid, x, 0.0).astype(BF16)


def _prep_w_in(w_t):
    depth, n, d = w_t.shape
    assert n == _D_IN and all(a % _WTILE == 0 and o % _WTILE == 0 for a, _, o in _RUNS)
    return pl.pallas_call(
        _wprep_kernel,
        out_shape=jax.ShapeDtypeStruct((depth, _NP, d), BF16),
        grid=(depth, _NP // _WTILE),
        in_specs=[pl.BlockSpec((None, _WTILE, d), lambda l, j: (l, _wprep_src_tile(j), 0))],
        out_specs=pl.BlockSpec((None, _WTILE, d), lambda l, j: (l, j, 0)),
        compiler_params=pltpu.CompilerParams(
            dimension_semantics=("parallel", "parallel"),
            vmem_limit_bytes=_VMEM_LIMIT),
        name="w_prep",
    )(w_t)


def kernel(x, norm_w, w_in, w_out, rwkv_mu, rwkv_w0, rwkv_w2, rwkv_a0, rwkv_a2, rwkv_k_k, rwkv_k_a,
           rwkv_r_k, rwkv_ln_w, rwkv_ln_b, gla_a_up, gla_a_b, gla_norm_w, final_norm_w):
    b, s, d = x.shape
    depth = norm_w.shape[0]
    x2 = x.reshape(b * s, d)
    zero_lora = jnp.zeros((_RLORA, _RW), F32)
    fw = final_norm_w.reshape(1, d)
    w_in_p = _prep_w_in(jnp.swapaxes(w_in, 1, 2))
    for l in range(depth):
        w_out_l = w_out[l].astype(BF16)
        mu = rwkv_mu[l]
        mu_r, mu_k, mu_v = (mu[i * _RW:(i + 1) * _RW].reshape(1, _RW) for i in range(3))
        mu_wa = mu[3 * _RW:].reshape(1, 2 * _RLORA)
        w2p = jnp.concatenate([rwkv_w2[l], zero_lora], axis=0)
        a2p = jnp.concatenate([zero_lora, rwkv_a2[l]], axis=0)
        a_up_p = jnp.concatenate([gla_a_up[l], jnp.zeros((128 - _GRANK, _GKW), F32)], axis=0)

        p = _in_proj(x2, norm_w[l].reshape(1, d), w_in_p, l)
        p3 = p.reshape(b, s, _NP)
        row = lambda a: a.reshape(1, _RW)
        y_a = _rwkv_mix(p3, mu_r, mu_k, mu_v, mu_wa, row(rwkv_w0[l]), w2p, row(rwkv_a0[l]), a2p,
                        row(rwkv_k_k[l]), row(rwkv_k_a[l]), row(rwkv_r_k[l]),
                        row(rwkv_ln_w[l]), row(rwkv_ln_b[l]))
        y_b = _moba_mix(p3)
        y_c = _gla_mix(p3, a_up_p, gla_a_b[l].reshape(1, _GKW), gla_norm_w[l].reshape(1, _GDV))
        x2 = _out_proj(x2, y_a.reshape(b * s, _RW), y_b.reshape(b * s, _MW), y_c.reshape(b * s, _GW),
                       w_out_l, fw, final_norm=(l == depth - 1))
    return x2.reshape(b, s, d)
```

```python
import functools

import jax
import jax.numpy as jnp
from jax import lax
from jax.experimental import pallas as pl
from jax.experimental.pallas import tpu as pltpu

F32 = jnp.float32
BF16 = jnp.bfloat16

_D = 2048
_RW = 1024
_RN = 64
_RLORA = 64
_GN_EPS = 64e-5
_MW = 512
_MD = 128
_MBLK = 256
_MTOPK = 3
_GW = 512
_GKW = 256
_GDK = 64
_GDV = 128
_GRANK = 16
_GTEMP = 16.0
_EPS = 1e-6
_C = 64
_GRP = 256

_D_IN = 7824
_RUNS = ((0, 3072, 0),
         (3200, 4224, 3072),
         (4224, 6272, 4096),
         (6784, 7808, 6144),
         (6272, 6784, 7168),
         (3072, 3200, 7680),
         (7808, 7824, 7808))
_O_R, _O_K, _O_V, _O_GA = 0, 1024, 2048, 3072
_O_MQ, _O_MK, _O_MV, _O_MG = 4096, 4608, 5120, 5632
_O_GV, _O_GG = 6144, 6656
_O_GQ, _O_GK = 7168, 7424
_O_WA, _O_GL = 7680, 7808
_NP = 7936
assert all(o % _GW == 0 for o in (_O_GV, _O_GG))
assert all(o % _RW == 0 for o in (_O_R, _O_K, _O_V, _O_GA))
assert all(o % _GKW == 0 for o in (_O_GQ, _O_GK))
assert all(o % _MW == 0 for o in (_O_MQ, _O_MK, _O_MV, _O_MG))
assert all(o % 128 == 0 for o in (_O_WA, _O_GL))

_NN = (((1,), (0,)), ((), ()))
_NT = (((1,), (1,)), ((), ()))
_TN = (((0,), (0,)), ((), ()))

_VMEM_LIMIT = 56 * 1024 * 1024
_EXP_M05 = 0.6065306597126334

_P_INV = 1
_P_MAIN = 1
_P_SUM = 2
_P_HEADSUM = 1


def _parts(x, n):
    if x.dtype == BF16:
        return [x]
    out, rem = [], x
    for i in range(n):
        p = rem.astype(BF16)
        out.append(p)
        if i + 1 < n:
            rem = rem - p.astype(F32)
    return out


def _mm(a, b, dims=_NN, na=1, nb=1):
    pa = a if isinstance(a, list) else _parts(a, na)
    pb = b if isinstance(b, list) else _parts(b, nb)
    order = max(len(pa), len(pb))
    free_axis = 1 if dims[0][0] == (0,) else 0
    m = pa[0].shape[free_axis]
    acc = None
    for j, y in enumerate(pb):
        xs = pa[:order - j]
        if not xs:
            continue
        x = xs[0] if len(xs) == 1 else jnp.concatenate(xs, axis=free_axis)
        t = lax.dot_general(x, y, dims, preferred_element_type=F32)
        for i in range(len(xs)):
            blk = t[i * m:(i + 1) * m]
            acc = blk if acc is None else acc + blk
    return acc


def _softplus(x):
    return jnp.maximum(x, 0.0) + jnp.log(1.0 + jnp.exp(-jnp.abs(x)))


def _sigmoid(x):
    return 1.0 / (1.0 + jnp.exp(-x))


def _silu(x):
    return x * _sigmoid(x)


def _iota(shape, dim):
    return lax.broadcasted_iota(jnp.int32, shape, dim)


def _idiv(x, n):
    assert n & (n - 1) == 0
    return jnp.right_shift(x, n.bit_length() - 1)


def _inproj_kernel(x_ref, nw_ref, w_ref, o_ref):
    x = x_ref[...]
    ms = jnp.mean(x * x, axis=-1, keepdims=True)
    h = (x * lax.rsqrt(ms + _EPS) * nw_ref[...]).astype(BF16)
    o_ref[...] = lax.dot_general(h, w_ref[...], _NT, preferred_element_type=F32)


def _in_proj(x2, nw, w_bf16, layer, tm=256):
    t, d = x2.shape
    n = w_bf16.shape[1]
    return pl.pallas_call(
        _inproj_kernel,
        out_shape=jax.ShapeDtypeStruct((t, n), F32),
        grid=(t // tm,),
        in_specs=[
            pl.BlockSpec((tm, d), lambda i: (i, 0)),
            pl.BlockSpec((1, d), lambda i: (0, 0)),
            pl.BlockSpec((None, n, d), lambda i: (layer, 0, 0), pipeline_mode=pl.Buffered(1)),
        ],
        out_specs=pl.BlockSpec((tm, n), lambda i: (i, 0)),
        compiler_params=pltpu.CompilerParams(
            dimension_semantics=("parallel",),
            vmem_limit_bytes=_VMEM_LIMIT),
        name="in_proj",
    )(x2, nw, w_bf16)


def _token_shift(x, prev_ref, mu):
    c = x.shape[0]
    rows = _iota(x.shape, 0)
    xp = jnp.where(rows == 0, prev_ref[7:8, :], pltpu.roll(x, 1, 0))
    prev_ref[...] = x[c - 8:c, :]
    return x + mu * (xp - x)


def _head_sums(x, head_ones):
    c = x.shape[0]
    ng = x.shape[1] // _GRP
    rows = [p[:, g * _GRP:(g + 1) * _GRP] for p in _parts(x, _P_HEADSUM) for g in range(ng)]
    t = lax.dot_general(jnp.concatenate(rows, axis=0), head_ones, _NN, preferred_element_type=F32)
    cols = []
    for g in range(ng):
        acc = t[g * c:(g + 1) * c]
        for i in range(1, _P_HEADSUM):
            acc = acc + t[(i * ng + g) * c:(i * ng + g + 1) * c]
        cols.append(acc)
    return jnp.concatenate(cols, axis=1)


def _stack_lanes(x, lane_masks):
    return jnp.concatenate([x * m for m in lane_masks], axis=0)


def _rwkv_groups(at, rt, bt, kt, bh, kh, v, w_end, st_ref, strict, incl, eye_c, eye, same):
    c = _C
    nh = _GRP // _RN
    gs = range(at.shape[1] // _GRP)
    lane_head = _idiv(_iota((1, _GRP), 1), _RN)
    hm = [jnp.where(lane_head == h, 1.0, 0.0).astype(BF16) for h in range(nh)]

    def grp(x, g):
        return x[:, g * _GRP:(g + 1) * _GRP]

    def bf(x):
        return x.astype(BF16)

    def dot(a, b, dims=_NN):
        return lax.dot_general(a, b, dims, preferred_element_type=F32)

    def stack(x):
        return _stack_lanes(bf(x), hm)

    at_b = [bf(grp(at, g)) for g in gs]
    v_b = [bf(grp(v, g)) for g in gs]
    bh_b = [bf(grp(bh, g)) for g in gs]
    lhs = [jnp.concatenate([at_b[g], bf(grp(rt, g))], axis=0) for g in gs]
    rhs = [jnp.concatenate([stack(grp(bt, g)), stack(grp(kt, g))], axis=0) for g in gs]
    a_all = [dot(lhs[g], rhs[g], _NT) for g in gs]
    a_ab = [a_all[g][:c, :nh * c] * strict for g in gs]
    a_ak = [bf(a_all[g][:c, nh * c:] * strict) for g in gs]
    a_rb = [bf(a_all[g][c:, :nh * c] * incl) for g in gs]
    a_rk = [bf(a_all[g][c:, nh * c:] * incl) for g in gs]

    x = [eye_c + a_ab[g] for g in gs]
    p_b = [bf(a_ab[g]) for g in gs]
    p = [dot(p_b[g], _stack_lanes(p_b[g], hm)) for g in gs]
    n_sq = c.bit_length() - 2
    for it in range(n_sq):
        p_bd = [stack(p[g]) for g in gs]
        if it + 1 < n_sq:
            xp = [dot(jnp.concatenate([bf(x[g]), bf(p[g])], axis=0), p_bd[g]) for g in gs]
            x = [x[g] + xp[g][:c] for g in gs]
            p = [xp[g][c:] for g in gs]
        else:
            x = [x[g] + dot(bf(x[g]), p_bd[g]) for g in gs]
    t = [bf(x[g]) for g in gs]

    v_s = [_stack_lanes(v_b[g], hm) for g in gs]
    abar = [dot(t[g], _stack_lanes(at_b[g], hm)) for g in gs]
    akv = [dot(jnp.concatenate([a_ak[g], a_rk[g]], axis=0), v_s[g]) for g in gs]
    ubar = [dot(t[g], stack(akv[g][:c])) for g in gs]
    abar_b = [bf(abar[g]) for g in gs]
    ubar_b = [bf(ubar[g]) for g in gs]
    rbar = [grp(rt, g) + dot(a_rb[g], _stack_lanes(abar_b[g], hm)) for g in gs]
    ybar = [dot(a_rb[g], _stack_lanes(ubar_b[g], hm)) + akv[g][c:] for g in gs]
    mt = [eye * grp(w_end, g) + dot(bh_b[g], abar_b[g], _TN) for g in gs]
    dt = [dot(jnp.concatenate([bh_b[g], bf(grp(kh, g))], axis=0),
              jnp.concatenate([ubar_b[g], v_b[g]], axis=0), _TN) for g in gs]

    ys = [dot(jnp.concatenate([bf(rbar[g]), bf(mt[g])], axis=0), bf(st_ref[g])) for g in gs]
    for g in gs:
        st_ref[g] = same * (ys[g][c:] + dt[g])
    return jnp.concatenate([ys[g][:c] + ybar[g] for g in gs], axis=1)


def _rwkv_kernel(pr_ref, pk_ref, pv_ref, pg_ref, pwa_ref,
                 mur_ref, muk_ref, muv_ref, muwa_ref,
                 w0_ref, w2_ref, a0_ref, a2_ref, kk_ref, ka_ref, rk_ref, lnw_ref, lnb_ref,
                 strict_ref, incl_ref, eyec_ref, eye_ref, same_ref, hones_ref, tri_ref,
                 o_ref,
                 s_ref, prev_r, prev_k, prev_v, prev_wa):
    c = _C
    nb = pr_ref.shape[0]

    @pl.when(pl.program_id(1) == 0)
    def _():
        s_ref[...] = jnp.zeros_like(s_ref)
        prev_r[...] = jnp.zeros_like(prev_r)
        prev_k[...] = jnp.zeros_like(prev_k)
        prev_v[...] = jnp.zeros_like(prev_v)
        prev_wa[...] = jnp.zeros_like(prev_wa)

    head_ones = hones_ref[...]

    def lanes(ref):
        return jnp.concatenate([ref[b] for b in range(nb)], axis=1)

    def rep(ref):
        return jnp.concatenate([ref[...]] * nb, axis=1)

    def unrows(x):
        return jnp.concatenate([x[b * c:(b + 1) * c] for b in range(nb)], axis=1)

    r = _token_shift(lanes(pr_ref), prev_r, rep(mur_ref))
    k = _token_shift(lanes(pk_ref), prev_k, rep(muk_ref))
    v = _token_shift(lanes(pv_ref), prev_v, rep(muv_ref))
    wa = _token_shift(lanes(pwa_ref), prev_wa, rep(muwa_ref))
    wa = jnp.concatenate([wa[:, b * 128:(b + 1) * 128] for b in range(nb)], axis=0)

    wdec = rep(w0_ref) + unrows(_mm(jnp.tanh(wa), w2_ref[...], na=2))
    lw = -_EXP_M05 * _sigmoid(wdec)
    a_lr = _sigmoid(rep(a0_ref) + unrows(_mm(wa, a2_ref[...], na=2)))
    kk = k * rep(kk_ref)
    kk = kk * lax.rsqrt(jnp.maximum(_head_sums(kk * kk, head_ones), 1e-24))
    k2 = k * (1.0 + (a_lr - 1.0) * rep(ka_ref))
    bb = kk * a_lr
    cl = _mm(tri_ref[...], lw, nb=_P_SUM)
    cl_end = cl[c - 1:c, :]
    e_in = jnp.exp(cl)
    e_neg = jnp.exp(-cl)
    w_end = jnp.exp(cl_end)
    e_end = w_end * e_neg
    at = -kk * jnp.exp(cl - lw)
    rt = r * e_in
    bt = bb * e_neg
    kt = k2 * e_neg
    bh = bb * e_end
    kh = k2 * e_end

    y = _rwkv_groups(at, rt, bt, kt, bh, kh, v, w_end, s_ref,
                     strict_ref[...], incl_ref[...], eyec_ref[...], eye_ref[...], same_ref[...])

    inv_n = 1.0 / _RN
    mean = _head_sums(y, head_ones) * inv_n
    yc = y - mean
    var = _head_sums(yc * yc, head_ones) * inv_n
    yn = yc * lax.rsqrt(var + _GN_EPS) * rep(lnw_ref) + rep(lnb_ref)
    bonus = _head_sums(r * k2 * rep(rk_ref), head_ones) * v
    out = yn + bonus
    for b in range(nb):
        o_ref[b] = (out[:, b * _RW:(b + 1) * _RW] * _silu(pg_ref[b])).astype(o_ref.dtype)


def _rwkv_consts():
    n = _GRP
    t = lax.broadcasted_iota(jnp.int32, (_C, n // _RN * _C), 0)
    s = lax.broadcasted_iota(jnp.int32, (_C, n // _RN * _C), 1) % _C
    strict = (s < t).astype(F32)
    incl = (s <= t).astype(F32)
    eye_c = (s == t).astype(F32)
    ri = lax.broadcasted_iota(jnp.int32, (n, n), 0)
    ci = lax.broadcasted_iota(jnp.int32, (n, n), 1)
    eye = (ri == ci).astype(F32)
    same = (ri // _RN) == (ci // _RN)
    ti = lax.broadcasted_iota(jnp.int32, (_C, _C), 0)
    tj = lax.broadcasted_iota(jnp.int32, (_C, _C), 1)
    tri = (ti >= tj).astype(BF16)
    return strict, incl, eye_c, eye, same.astype(F32), same.astype(BF16), tri


def _rwkv_mix(p3, mu_r, mu_k, mu_v, mu_wa, w0, w2p, a0, a2p, k_k, k_a, r_k, ln_w, ln_b, bb=4):
    b, s, _ = p3.shape
    nc = s // _C
    bb = min(bb, b)
    assert b % bb == 0
    ng = bb * _RW // _GRP

    def pspec(off):
        return pl.BlockSpec((bb, _C, _RW), lambda bi, c, o=off // _RW: (bi, c, o))

    def full(shape):
        return pl.BlockSpec(shape, lambda bi, c: (0,) * len(shape))

    vec = full((1, _RW))
    lora = full((2 * _RLORA, _RW))
    sq = full((_GRP, _GRP))
    cat = full((_C, _GRP // _RN * _C))
    return pl.pallas_call(
        _rwkv_kernel,
        out_shape=jax.ShapeDtypeStruct((b, s, _RW), BF16),
        grid=(b // bb, nc),
        in_specs=[
            pspec(_O_R), pspec(_O_K), pspec(_O_V), pspec(_O_GA),
            pl.BlockSpec((bb, _C, 128), lambda bi, c: (bi, c, _O_WA // 128)),
            vec, vec, vec, full((1, 2 * _RLORA)),
            vec, lora, vec, lora, vec, vec, vec, vec, vec,
            cat, cat, cat, sq, sq, sq, full((_C, _C)),
        ],
        out_specs=pl.BlockSpec((bb, _C, _RW), lambda bi, c: (bi, c, 0)),
        scratch_shapes=[
            pltpu.VMEM((ng, _GRP, _GRP), F32),
            pltpu.VMEM((8, bb * _RW), F32), pltpu.VMEM((8, bb * _RW), F32),
            pltpu.VMEM((8, bb * _RW), F32), pltpu.VMEM((8, bb * 2 * _RLORA), F32),
        ],
        compiler_params=pltpu.CompilerParams(
            dimension_semantics=("parallel", "arbitrary"),
            vmem_limit_bytes=_VMEM_LIMIT),
        name="rwkv_mix",
    )(p3, p3, p3, p3, p3, mu_r, mu_k, mu_v, mu_wa, w0, w2p, a0, a2p, k_k, k_a, r_k, ln_w, ln_b,
      *_rwkv_consts())


def _moba_block(ii, q_ref, k_ref, v_ref, g_ref, o_ref, kmean_ref):
    blk = _MBLK
    nh = _MW // _MD
    hs = range(nh)
    scale = _MD ** -0.5
    neg = -1e30
    npast = ii * blk

    def hsl(h):
        return slice(h * _MD, (h + 1) * _MD)

    def dot(a, b, dims):
        return lax.dot_general(a, b, dims, preferred_element_type=F32)

    q = [q_ref[:, hsl(h)] for h in hs]
    qs = [(q[h] * scale).astype(BF16) for h in hs]
    own = slice(npast, npast + blk)
    kpos = _iota((blk, blk), 0)
    qpos = _iota((blk, blk), 1)
    s_own = [jnp.where(kpos <= qpos, dot(k_ref[own, hsl(h)].astype(BF16), qs[h], _NT), neg) for h in hs]
    m = [jnp.max(s_own[h], axis=0, keepdims=True) for h in hs]

    if ii > 0:
        npad = kmean_ref.shape[0]
        bid = _iota((npad, blk), 0)
        gate = [jnp.where(bid < ii, _mm(kmean_ref[:, hsl(h)], q[h], _NT, na=2, nb=2), -jnp.inf)
                for h in hs]
        s_past = [dot(k_ref[0:npast, hsl(h)].astype(BF16), qs[h], _NT) for h in hs]
        masked = []
        for h in hs:
            rows = []
            for j in range(ii):
                gj = gate[h][j:j + 1, :]
                ahead = (gate[h] > gj) | ((gate[h] == gj) & (bid < j))
                rank = jnp.sum(jnp.where(ahead, 1.0, 0.0), axis=0, keepdims=True)
                rows.append(jnp.where(rank < float(_MTOPK), s_past[h][j * blk:(j + 1) * blk], neg))
            masked.append(rows[0] if ii == 1 else jnp.concatenate(rows, axis=0))
        m = [jnp.maximum(m[h], jnp.max(masked[h], axis=0, keepdims=True)) for h in hs]
        p_past = [jnp.exp(masked[h] - m[h]) for h in hs]

    p_own = [jnp.exp(s_own[h] - m[h]) for h in hs]
    l = [jnp.sum(p_own[h], axis=0, keepdims=True) for h in hs]
    acc = [dot(v_ref[own, hsl(h)].astype(BF16), p_own[h].astype(BF16), _TN) for h in hs]
    if ii > 0:
        l = [l[h] + jnp.sum(p_past[h], axis=0, keepdims=True) for h in hs]
        acc = [acc[h] + dot(v_ref[0:npast, hsl(h)].astype(BF16), p_past[h].astype(BF16), _TN) for h in hs]

    o = jnp.concatenate([(acc[h] / l[h]).T for h in hs], axis=1)
    o_ref[...] = (o * _silu(g_ref[...])).astype(o_ref.dtype)


def _moba_kernel(q_ref, k_ref, v_ref, g_ref, o_ref, kmean_ref):
    i = pl.program_id(1)
    blk = _MBLK
    nb = k_ref.shape[0] // blk

    @pl.when(i == 0)
    def _():
        kmean_ref[...] = jnp.zeros_like(kmean_ref)
        for j in range(nb):
            kmean_ref[j:j + 1, :] = jnp.mean(k_ref[j * blk:(j + 1) * blk, :], axis=0, keepdims=True)

    for ii in range(nb):
        pl.when(i == ii)(functools.partial(_moba_block, ii, q_ref, k_ref, v_ref, g_ref, o_ref, kmean_ref))


def _moba_mix(p3):
    b, s, _ = p3.shape
    nq = s // _MBLK
    npad = -(-nq // 16) * 16
    return pl.pallas_call(
        _moba_kernel,
        out_shape=jax.ShapeDtypeStruct((b, s, _MW), BF16),
        grid=(b, nq),
        in_specs=[
            pl.BlockSpec((None, _MBLK, _MW), lambda bi, i: (bi, i, _O_MQ // _MW)),
            pl.BlockSpec((None, s, _MW), lambda bi, i: (bi, 0, _O_MK // _MW)),
            pl.BlockSpec((None, s, _MW), lambda bi, i: (bi, 0, _O_MV // _MW)),
            pl.BlockSpec((None, _MBLK, _MW), lambda bi, i: (bi, i, _O_MG // _MW)),
        ],
        out_specs=pl.BlockSpec((None, _MBLK, _MW), lambda bi, i: (bi, i, 0)),
        scratch_shapes=[pltpu.VMEM((npad, _MW), F32)],
        compiler_params=pltpu.CompilerParams(
            dimension_semantics=("parallel", "arbitrary"),
            vmem_limit_bytes=_VMEM_LIMIT),
        name="moba_mix",
    )(p3, p3, p3, p3)


def _gla_kernel(q_ref, k_ref, v_ref, g_ref, low_ref, aup_ref, ab_ref, nw_ref, incl_ref, tri_ref,
                o_ref, st_ref):
    c = _C
    nh = _GKW // _GDK
    bs = range(q_ref.shape[0])

    @pl.when(pl.program_id(1) == 0)
    def _():
        st_ref[...] = jnp.zeros_like(st_ref)

    def lane_masks(width, head_w):
        lane_head = _idiv(_iota((1, width), 1), head_w)
        return [jnp.where(lane_head == h, 1.0, 0.0).astype(BF16) for h in range(nh)]

    def dot(a, b, dims=_NN):
        return lax.dot_general(a, b, dims, preferred_element_type=F32)

    hm_k, hm_v = lane_masks(_GKW, _GDK), lane_masks(_GW, _GDV)
    incl, tri = incl_ref[...], tri_ref[...]
    z = [_mm(low_ref[b], aup_ref[...], na=2, nb=2) + ab_ref[...] for b in bs]
    la = [-_softplus(-z[b]) * (1.0 / _GTEMP) for b in bs]
    bcum = [_mm(tri, la[b], nb=_P_SUM) for b in bs]
    b_last = [bcum[b][c - 1:c, :] for b in bs]
    k = [k_ref[b] for b in bs]
    qe = [(q_ref[b] * (_GDK ** -0.5) * jnp.exp(bcum[b])).astype(BF16) for b in bs]
    ke_s = [_stack_lanes((k[b] * jnp.exp(-bcum[b])).astype(BF16), hm_k) for b in bs]
    kd_s = [_stack_lanes((k[b] * jnp.exp(b_last[b] - bcum[b])).astype(BF16), hm_k) for b in bs]
    v_s = [_stack_lanes(v_ref[b].astype(BF16), hm_v) for b in bs]

    att = [(dot(qe[b], ke_s[b], _NT) * incl).astype(BF16) for b in bs]
    st = [st_ref[b] for b in bs]
    o = [dot(att[b], v_s[b]) + dot(qe[b], st[b].astype(BF16), _NT) for b in bs]
    for b in bs:
        st_ref[b] = st[b] * jnp.exp(b_last[b]) + dot(v_s[b], kd_s[b], _TN)

    for b in bs:
        outs = []
        for h in range(nh):
            oh = o[b][:, h * _GDV:(h + 1) * _GDV]
            ms = jnp.mean(oh * oh, axis=-1, keepdims=True)
            outs.append(oh * lax.rsqrt(ms + _EPS) * nw_ref[...])
        o_ref[b] = (jnp.concatenate(outs, axis=1) * _silu(g_ref[b])).astype(o_ref.dtype)


def _gla_mix(p3, a_up_p, a_b, norm_w, bb=8):
    b, s, _ = p3.shape
    nc = s // _C
    bb = min(bb, b)
    assert b % bb == 0
    n = _GKW // _GDK * _C
    ti = lax.broadcasted_iota(jnp.int32, (_C, n), 0)
    si = lax.broadcasted_iota(jnp.int32, (_C, n), 1) % _C
    incl = (si <= ti).astype(F32)
    tri = (lax.broadcasted_iota(jnp.int32, (_C, _C), 0) >= lax.broadcasted_iota(jnp.int32, (_C, _C), 1)).astype(BF16)

    def full(shape):
        return pl.BlockSpec(shape, lambda bi, c: (0,) * len(shape))

    return pl.pallas_call(
        _gla_kernel,
        out_shape=jax.ShapeDtypeStruct((b, s, _GW), BF16),
        grid=(b // bb, nc),
        in_specs=[
            pl.BlockSpec((bb, _C, _GKW), lambda bi, c: (bi, c, _O_GQ // _GKW)),
            pl.BlockSpec((bb, _C, _GKW), lambda bi, c: (bi, c, _O_GK // _GKW)),
            pl.BlockSpec((bb, _C, _GW), lambda bi, c: (bi, c, _O_GV // _GW)),
            pl.BlockSpec((bb, _C, _GW), lambda bi, c: (bi, c, _O_GG // _GW)),
            pl.BlockSpec((bb, _C, 128), lambda bi, c: (bi, c, _O_GL // 128)),
            full((128, _GKW)), full((1, _GKW)), full((1, _GDV)), full((_C, n)), full((_C, _C)),
        ],
        out_specs=pl.BlockSpec((bb, _C, _GW), lambda bi, c: (bi, c, 0)),
        scratch_shapes=[pltpu.VMEM((bb, _GW, _GKW), F32)],
        compiler_params=pltpu.CompilerParams(
            dimension_semantics=("parallel", "arbitrary"),
            vmem_limit_bytes=_VMEM_LIMIT),
        name="gla_mix",
    )(p3, p3, p3, p3, p3, a_up_p, a_b, norm_w, incl, tri)


def _outproj_kernel(x_ref, ya_ref, yb_ref, yc_ref, w_ref, fw_ref, o_ref, *, final_norm):
    acc = x_ref[...]
    acc = acc + jnp.dot(ya_ref[...], w_ref[0:_RW, :], preferred_element_type=F32)
    acc = acc + jnp.dot(yb_ref[...], w_ref[_RW:_RW + _MW, :], preferred_element_type=F32)
    acc = acc + jnp.dot(yc_ref[...], w_ref[_RW + _MW:, :], preferred_element_type=F32)
    if final_norm:
        ms = jnp.mean(acc * acc, axis=-1, keepdims=True)
        acc = acc * lax.rsqrt(ms + _EPS) * fw_ref[...]
    o_ref[...] = acc


def _out_proj(x2, ya, yb, yc, w_bf16, fw, final_norm, tm=512):
    t, d = x2.shape
    return pl.pallas_call(
        functools.partial(_outproj_kernel, final_norm=final_norm),
        out_shape=jax.ShapeDtypeStruct((t, d), F32),
        grid=(t // tm,),
        in_specs=[
            pl.BlockSpec((tm, d), lambda i: (i, 0)),
            pl.BlockSpec((tm, _RW), lambda i: (i, 0)),
            pl.BlockSpec((tm, _MW), lambda i: (i, 0)),
            pl.BlockSpec((tm, _GW), lambda i: (i, 0)),
            pl.BlockSpec((d, d), lambda i: (0, 0)),
            pl.BlockSpec((1, d), lambda i: (0, 0)),
        ],
        out_specs=pl.BlockSpec((tm, d), lambda i: (i, 0)),
        compiler_params=pltpu.CompilerParams(
            dimension_semantics=("parallel",),
            vmem_limit_bytes=_VMEM_LIMIT),
        name="out_proj",
    )(x2, ya, yb, yc, w_bf16, fw)


_WTILE = 128


def _wprep_src_tile(j):
    src = jnp.int32(_D_IN // _WTILE)
    for a, b, o in _RUNS:
        lo = o // _WTILE
        hi = lo + -(-(b - a) // _WTILE)
        src = jnp.where((j >= lo) & (j < hi), j + (a - o) // _WTILE, src)
    return src


def _wprep_kernel(w_ref, o_ref):
    j = pl.program_id(1)
    last_dst = _RUNS[-1][2] + _RUNS[-1][1] - _RUNS[-1][0]
    valid = jnp.clip(last_dst - j * _WTILE, 0, _WTILE)
    x = w_ref[...]
    o_ref[...] = jnp.where(_iota(x.shape, 0) < valid, x, 0.0).astype(BF16)


def _prep_w_in(w_t):
    depth, n, d = w_t.shape
    assert n == _D_IN and all(a % _WTILE == 0 and o % _WTILE == 0 for a, _, o in _RUNS)
    return pl.pallas_call(
        _wprep_kernel,
        out_shape=jax.ShapeDtypeStruct((depth, _NP, d), BF16),
        grid=(depth, _NP // _WTILE),
        in_specs=[pl.BlockSpec((None, _WTILE, d), lambda l, j: (l, _wprep_src_tile(j), 0))],
        out_specs=pl.BlockSpec((None, _WTILE, d), lambda l, j: (l, j, 0)),
        compiler_params=pltpu.CompilerParams(
            dimension_semantics=("parallel", "parallel"),
            vmem_limit_bytes=_VMEM_LIMIT),
        name="w_prep",
    )(w_t)


def kernel(x, norm_w, w_in, w_out, rwkv_mu, rwkv_w0, rwkv_w2, rwkv_a0, rwkv_a2, rwkv_k_k, rwkv_k_a,
           rwkv_r_k, rwkv_ln_w, rwkv_ln_b, gla_a_up, gla_a_b, gla_norm_w, final_norm_w):
    b, s, d = x.shape
    depth = norm_w.shape[0]
    x2 = x.reshape(b * s, d)
    zero_lora = jnp.zeros((_RLORA, _RW), F32)
    fw = final_norm_w.reshape(1, d)
    w_in_p = _prep_w_in(jnp.swapaxes(w_in, 1, 2))
    for l in range(depth):
        w_out_l = w_out[l].astype(BF16)
        mu = rwkv_mu[l]
        mu_r, mu_k, mu_v = (mu[i * _RW:(i + 1) * _RW].reshape(1, _RW) for i in range(3))
        mu_wa = mu[3 * _RW:].reshape(1, 2 * _RLORA)
        w2p = jnp.concatenate([rwkv_w2[l], zero_lora], axis=0)
        a2p = jnp.concatenate([zero_lora, rwkv_a2[l]], axis=0)
        a_up_p = jnp.concatenate([gla_a_up[l], jnp.zeros((128 - _GRANK, _GKW), F32)], axis=0)

        p = _in_proj(x2, norm_w[l].reshape(1, d), w_in_p, l)
        p3 = p.reshape(b, s, _NP)
        row = lambda a: a.reshape(1, _RW)
        y_a = _rwkv_mix(p3, mu_r, mu_k, mu_v, mu_wa, row(rwkv_w0[l]), w2p, row(rwkv_a0[l]), a2p,
                        row(rwkv_k_k[l]), row(rwkv_k_a[l]), row(rwkv_r_k[l]),
                        row(rwkv_ln_w[l]), row(rwkv_ln_b[l]))
        y_b = _moba_mix(p3)
        y_c = _gla_mix(p3, a_up_p, gla_a_b[l].reshape(1, _GKW), gla_norm_w[l].reshape(1, _GDV))
        x2 = _out_proj(x2, y_a.reshape(b * s, _RW), y_b.reshape(b * s, _MW), y_c.reshape(b * s, _GW),
                       w_out_l, fw, final_norm=(l == depth - 1))
    return x2.reshape(b, s, d)
```

```python
import functools

import jax
import jax.numpy as jnp
from jax import lax
from jax.experimental import pallas as pl
from jax.experimental.pallas import tpu as pltpu

F32 = jnp.float32
BF16 = jnp.bfloat16

_D = 2048
_RW = 1024
_RN = 64
_RLORA = 64
_GN_EPS = 64e-5
_MW = 512
_MD = 128
_MBLK = 256
_MTOPK = 3
_GW = 512
_GKW = 256
_GDK = 64
_GDV = 128
_GRANK = 16
_GTEMP = 16.0
_EPS = 1e-6
_C = 64
_GRP = 256

_D_IN = 7824
_RUNS = ((0, 3072, 0),
         (3200, 4224, 3072),
         (4224, 6272, 4096),
         (6784, 7808, 6144),
         (6272, 6784, 7168),
         (3072, 3200, 7680),
         (7808, 7824, 7808))
_O_R, _O_K, _O_V, _O_GA = 0, 1024, 2048, 3072
_O_MQ, _O_MK, _O_MV, _O_MG = 4096, 4608, 5120, 5632
_O_GV, _O_GG = 6144, 6656
_O_GQ, _O_GK = 7168, 7424
_O_WA, _O_GL = 7680, 7808
_NP = 7936
assert all(o % _GW == 0 for o in (_O_GV, _O_GG))
assert all(o % _RW == 0 for o in (_O_R, _O_K, _O_V, _O_GA))
assert all(o % _GKW == 0 for o in (_O_GQ, _O_GK))
assert all(o % _MW == 0 for o in (_O_MQ, _O_MK, _O_MV, _O_MG))
assert all(o % 128 == 0 for o in (_O_WA, _O_GL))

_NN = (((1,), (0,)), ((), ()))
_NT = (((1,), (1,)), ((), ()))
_TN = (((0,), (0,)), ((), ()))

_VMEM_LIMIT = 56 * 1024 * 1024
_EXP_M05 = 0.6065306597126334
_LOG2E = 1.4426950408889634

_P_INV = 1
_P_MAIN = 1
_P_SUM = 2
_P_HEADSUM = 1


def _parts(x, n):
    if x.dtype == BF16:
        return [x]
    out, rem = [], x
    for i in range(n):
        p = rem.astype(BF16)
        out.append(p)
        if i + 1 < n:
            rem = rem - p.astype(F32)
    return out


def _mm(a, b, dims=_NN, na=1, nb=1):
    pa = a if isinstance(a, list) else _parts(a, na)
    pb = b if isinstance(b, list) else _parts(b, nb)
    order = max(len(pa), len(pb))
    free_axis = 1 if dims[0][0] == (0,) else 0
    m = pa[0].shape[free_axis]
    acc = None
    for j, y in enumerate(pb):
        xs = pa[:order - j]
        if not xs:
            continue
        x = xs[0] if len(xs) == 1 else jnp.concatenate(xs, axis=free_axis)
        t = lax.dot_general(x, y, dims, preferred_element_type=F32)
        for i in range(len(xs)):
            blk = t[i * m:(i + 1) * m]
            acc = blk if acc is None else acc + blk
    return acc


def _softplus(x):
    return jnp.maximum(x, 0.0) + jnp.log(1.0 + jnp.exp(-jnp.abs(x)))


def _sigmoid(x):
    return 1.0 / (1.0 + jnp.exp2(x * -_LOG2E))


def _silu(x):
    return x * _sigmoid(x)


def _iota(shape, dim):
    return lax.broadcasted_iota(jnp.int32, shape, dim)


def _idiv(x, n):
    assert n & (n - 1) == 0
    return jnp.right_shift(x, n.bit_length() - 1)


def _inproj_kernel(x_ref, nw_ref, w_ref, o_ref):
    x = x_ref[...]
    ms = jnp.mean(x * x, axis=-1, keepdims=True)
    h = (x * lax.rsqrt(ms + _EPS) * nw_ref[...]).astype(BF16)
    o_ref[...] = lax.dot_general(h, w_ref[...], _NT, preferred_element_type=F32)


def _in_proj(x2, nw, w_bf16, layer, tm=256):
    t, d = x2.shape
    n = w_bf16.shape[1]
    return pl.pallas_call(
        _inproj_kernel,
        out_shape=jax.ShapeDtypeStruct((t, n), F32),
        grid=(t // tm,),
        in_specs=[
            pl.BlockSpec((tm, d), lambda i: (i, 0)),
            pl.BlockSpec((1, d), lambda i: (0, 0)),
            pl.BlockSpec((None, n, d), lambda i: (layer, 0, 0), pipeline_mode=pl.Buffered(1)),
        ],
        out_specs=pl.BlockSpec((tm, n), lambda i: (i, 0)),
        compiler_params=pltpu.CompilerParams(
            dimension_semantics=("parallel",),
            vmem_limit_bytes=_VMEM_LIMIT),
        name="in_proj",
    )(x2, nw, w_bf16)


def _token_shift(x, prev_ref, mu):
    c = x.shape[0]
    rows = _iota(x.shape, 0)
    xp = jnp.where(rows == 0, prev_ref[7:8, :], pltpu.roll(x, 1, 0))
    prev_ref[...] = x[c - 8:c, :]
    return x + mu * (xp - x)


def _head_sums(x, head_ones):
    c = x.shape[0]
    ng = x.shape[1] // _GRP
    rows = [p[:, g * _GRP:(g + 1) * _GRP] for p in _parts(x, _P_HEADSUM) for g in range(ng)]
    t = lax.dot_general(jnp.concatenate(rows, axis=0), head_ones, _NN, preferred_element_type=F32)
    cols = []
    for g in range(ng):
        acc = t[g * c:(g + 1) * c]
        for i in range(1, _P_HEADSUM):
            acc = acc + t[(i * ng + g) * c:(i * ng + g + 1) * c]
        cols.append(acc)
    return jnp.concatenate(cols, axis=1)


def _stack_lanes(x, lane_masks):
    return jnp.concatenate([x * m for m in lane_masks], axis=0)


def _rwkv_groups(at, rt, bt, kt, bh, kh, v, w_end, st_ref, strict, incl, eye_c, eye, same):
    c = _C
    nh = _GRP // _RN
    gs = range(at.shape[1] // _GRP)
    lane_head = _idiv(_iota((1, _GRP), 1), _RN)
    hm = [jnp.where(lane_head == h, 1.0, 0.0).astype(BF16) for h in range(nh)]

    def grp(x, g):
        return x[:, g * _GRP:(g + 1) * _GRP]

    def bf(x):
        return x.astype(BF16)

    def dot(a, b, dims=_NN):
        return lax.dot_general(a, b, dims, preferred_element_type=F32)

    def stack(x):
        return _stack_lanes(bf(x), hm)

    at_b = [bf(grp(at, g)) for g in gs]
    v_b = [bf(grp(v, g)) for g in gs]
    bh_b = [bf(grp(bh, g)) for g in gs]
    lhs = [jnp.concatenate([at_b[g], bf(grp(rt, g))], axis=0) for g in gs]
    rhs = [jnp.concatenate([stack(grp(bt, g)), stack(grp(kt, g))], axis=0) for g in gs]
    a_all = [dot(lhs[g], rhs[g], _NT) for g in gs]
    a_ab = [a_all[g][:c, :nh * c] * strict for g in gs]
    a_ak = [bf(a_all[g][:c, nh * c:] * strict) for g in gs]
    a_rb = [bf(a_all[g][c:, :nh * c] * incl) for g in gs]
    a_rk = [bf(a_all[g][c:, nh * c:] * incl) for g in gs]

    x = [eye_c + a_ab[g] for g in gs]
    p_b = [bf(a_ab[g]) for g in gs]
    p = [dot(p_b[g], _stack_lanes(p_b[g], hm)) for g in gs]
    n_sq = c.bit_length() - 2
    for it in range(n_sq):
        p_bd = [stack(p[g]) for g in gs]
        if it + 1 < n_sq:
            xp = [dot(jnp.concatenate([bf(x[g]), bf(p[g])], axis=0), p_bd[g]) for g in gs]
            x = [x[g] + xp[g][:c] for g in gs]
            p = [xp[g][c:] for g in gs]
        else:
            x = [x[g] + dot(bf(x[g]), p_bd[g]) for g in gs]
    t = [bf(x[g]) for g in gs]

    v_s = [_stack_lanes(v_b[g], hm) for g in gs]
    abar = [dot(t[g], _stack_lanes(at_b[g], hm)) for g in gs]
    akv = [dot(jnp.concatenate([a_ak[g], a_rk[g]], axis=0), v_s[g]) for g in gs]
    ubar = [dot(t[g], stack(akv[g][:c])) for g in gs]
    abar_b = [bf(abar[g]) for g in gs]
    ubar_b = [bf(ubar[g]) for g in gs]
    rbar = [grp(rt, g) + dot(a_rb[g], _stack_lanes(abar_b[g], hm)) for g in gs]
    ybar = [dot(a_rb[g], _stack_lanes(ubar_b[g], hm)) + akv[g][c:] for g in gs]
    mt = [eye * grp(w_end, g) + dot(bh_b[g], abar_b[g], _TN) for g in gs]
    dt = [dot(jnp.concatenate([bh_b[g], bf(grp(kh, g))], axis=0),
              jnp.concatenate([ubar_b[g], v_b[g]], axis=0), _TN) for g in gs]

    ys = [dot(jnp.concatenate([bf(rbar[g]), bf(mt[g])], axis=0), bf(st_ref[g])) for g in gs]
    for g in gs:
        st_ref[g] = same * (ys[g][c:] + dt[g])
    return jnp.concatenate([ys[g][:c] + ybar[g] for g in gs], axis=1)


def _rwkv_kernel(pr_ref, pk_ref, pv_ref, pg_ref, pwa_ref,
                 mur_ref, muk_ref, muv_ref, muwa_ref,
                 w0_ref, w2_ref, a0_ref, a2_ref, kk_ref, ka_ref, rk_ref, lnw_ref, lnb_ref,
                 strict_ref, incl_ref, eyec_ref, eye_ref, same_ref, hones_ref, tri_ref,
                 o_ref,
                 s_ref, prev_r, prev_k, prev_v, prev_wa):
    c = _C
    nb = pr_ref.shape[0]

    @pl.when(pl.program_id(1) == 0)
    def _():
        s_ref[...] = jnp.zeros_like(s_ref)
        prev_r[...] = jnp.zeros_like(prev_r)
        prev_k[...] = jnp.zeros_like(prev_k)
        prev_v[...] = jnp.zeros_like(prev_v)
        prev_wa[...] = jnp.zeros_like(prev_wa)

    head_ones = hones_ref[...]

    def lanes(ref):
        return jnp.concatenate([ref[b] for b in range(nb)], axis=1)

    def rep(ref):
        return jnp.concatenate([ref[...]] * nb, axis=1)

    def unrows(x):
        return jnp.concatenate([x[b * c:(b + 1) * c] for b in range(nb)], axis=1)

    r = _token_shift(lanes(pr_ref), prev_r, rep(mur_ref))
    k = _token_shift(lanes(pk_ref), prev_k, rep(muk_ref))
    v = _token_shift(lanes(pv_ref), prev_v, rep(muv_ref))
    wa = _token_shift(lanes(pwa_ref), prev_wa, rep(muwa_ref))
    wa = jnp.concatenate([wa[:, b * 128:(b + 1) * 128] for b in range(nb)], axis=0)

    wdec = rep(w0_ref) + unrows(_mm(jnp.tanh(wa), w2_ref[...], na=2))
    lw = -(_EXP_M05 * _LOG2E) * _sigmoid(wdec)
    a_lr = _sigmoid(rep(a0_ref) + unrows(_mm(wa, a2_ref[...], na=2)))
    kk = k * rep(kk_ref)
    kk = kk * lax.rsqrt(jnp.maximum(_head_sums(kk * kk, head_ones), 1e-24))
    k2 = k * (1.0 + (a_lr - 1.0) * rep(ka_ref))
    bb = kk * a_lr
    cl = _mm(tri_ref[...], lw, nb=_P_SUM)
    cl_end = cl[c - 1:c, :]
    e_in = jnp.exp2(cl)
    e_neg = jnp.exp2(-cl)
    w_end = jnp.exp2(cl_end)
    e_end = w_end * e_neg
    at = -kk * jnp.exp2(cl - lw)
    rt = r * e_in
    bt = bb * e_neg
    kt = k2 * e_neg
    bh = bb * e_end
    kh = k2 * e_end

    y = _rwkv_groups(at, rt, bt, kt, bh, kh, v, w_end, s_ref,
                     strict_ref[...], incl_ref[...], eyec_ref[...], eye_ref[...], same_ref[...])

    inv_n = 1.0 / _RN
    mean = _head_sums(y, head_ones) * inv_n
    yc = y - mean
    var = _head_sums(yc * yc, head_ones) * inv_n
    yn = yc * lax.rsqrt(var + _GN_EPS) * rep(lnw_ref) + rep(lnb_ref)
    bonus = _head_sums(r * k2 * rep(rk_ref), head_ones) * v
    out = yn + bonus
    for b in range(nb):
        o_ref[b] = (out[:, b * _RW:(b + 1) * _RW] * _silu(pg_ref[b])).astype(o_ref.dtype)


def _rwkv_consts():
    n = _GRP
    t = lax.broadcasted_iota(jnp.int32, (_C, n // _RN * _C), 0)
    s = lax.broadcasted_iota(jnp.int32, (_C, n // _RN * _C), 1) % _C
    strict = (s < t).astype(F32)
    incl = (s <= t).astype(F32)
    eye_c = (s == t).astype(F32)
    ri = lax.broadcasted_iota(jnp.int32, (n, n), 0)
    ci = lax.broadcasted_iota(jnp.int32, (n, n), 1)
    eye = (ri == ci).astype(F32)
    same = (ri // _RN) == (ci // _RN)
    ti = lax.broadcasted_iota(jnp.int32, (_C, _C), 0)
    tj = lax.broadcasted_iota(jnp.int32, (_C, _C), 1)
    tri = (ti >= tj).astype(BF16)
    return strict, incl, eye_c, eye, same.astype(F32), same.astype(BF16), tri


def _rwkv_mix(p3, mu_r, mu_k, mu_v, mu_wa, w0, w2p, a0, a2p, k_k, k_a, r_k, ln_w, ln_b, bb=4):
    b, s, _ = p3.shape
    nc = s // _C
    bb = min(bb, b)
    assert b % bb == 0
    ng = bb * _RW // _GRP

    def pspec(off):
        return pl.BlockSpec((bb, _C, _RW), lambda bi, c, o=off // _RW: (bi, c, o))

    def full(shape):
        return pl.BlockSpec(shape, lambda bi, c: (0,) * len(shape))

    vec = full((1, _RW))
    lora = full((2 * _RLORA, _RW))
    sq = full((_GRP, _GRP))
    cat = full((_C, _GRP // _RN * _C))
    return pl.pallas_call(
        _rwkv_kernel,
        out_shape=jax.ShapeDtypeStruct((b, s, _RW), BF16),
        grid=(b // bb, nc),
        in_specs=[
            pspec(_O_R), pspec(_O_K), pspec(_O_V), pspec(_O_GA),
            pl.BlockSpec((bb, _C, 128), lambda bi, c: (bi, c, _O_WA // 128)),
            vec, vec, vec, full((1, 2 * _RLORA)),
            vec, lora, vec, lora, vec, vec, vec, vec, vec,
            cat, cat, cat, sq, sq, sq, full((_C, _C)),
        ],
        out_specs=pl.BlockSpec((bb, _C, _RW), lambda bi, c: (bi, c, 0)),
        scratch_shapes=[
            pltpu.VMEM((ng, _GRP, _GRP), F32),
            pltpu.VMEM((8, bb * _RW), F32), pltpu.VMEM((8, bb * _RW), F32),
            pltpu.VMEM((8, bb * _RW), F32), pltpu.VMEM((8, bb * 2 * _RLORA), F32),
        ],
        compiler_params=pltpu.CompilerParams(
            dimension_semantics=("parallel", "arbitrary"),
            vmem_limit_bytes=_VMEM_LIMIT),
        name="rwkv_mix",
    )(p3, p3, p3, p3, p3, mu_r, mu_k, mu_v, mu_wa, w0, w2p, a0, a2p, k_k, k_a, r_k, ln_w, ln_b,
      *_rwkv_consts())


def _moba_block(ii, q_ref, k_ref, v_ref, g_ref, o_ref, kmean_ref):
    blk = _MBLK
    nh = _MW // _MD
    hs = range(nh)
    scale = _MD ** -0.5 * _LOG2E
    neg = -1e30
    npast = ii * blk

    def hsl(h):
        return slice(h * _MD, (h + 1) * _MD)

    def dot(a, b, dims):
        return lax.dot_general(a, b, dims, preferred_element_type=F32)

    q = [q_ref[:, hsl(h)] for h in hs]
    qs = [(q[h] * scale).astype(BF16) for h in hs]
    own = slice(npast, npast + blk)
    kpos = _iota((blk, blk), 0)
    qpos = _iota((blk, blk), 1)
    s_own = [jnp.where(kpos <= qpos, dot(k_ref[own, hsl(h)].astype(BF16), qs[h], _NT), neg) for h in hs]
    m = [jnp.max(s_own[h], axis=0, keepdims=True) for h in hs]

    if ii > 0:
        npad = kmean_ref.shape[0]
        bid = _iota((npad, blk), 0)
        gate = [jnp.where(bid < ii, _mm(kmean_ref[:, hsl(h)], q[h], _NT, na=2, nb=2), -jnp.inf)
                for h in hs]
        s_past = [dot(k_ref[0:npast, hsl(h)].astype(BF16), qs[h], _NT) for h in hs]
        masked = []
        for h in hs:
            rows = []
            for j in range(ii):
                gj = gate[h][j:j + 1, :]
                ahead = (gate[h] > gj) | ((gate[h] == gj) & (bid < j))
                rank = jnp.sum(jnp.where(ahead, 1.0, 0.0), axis=0, keepdims=True)
                rows.append(jnp.where(rank < float(_MTOPK), s_past[h][j * blk:(j + 1) * blk], neg))
            masked.append(rows[0] if ii == 1 else jnp.concatenate(rows, axis=0))
        m = [jnp.maximum(m[h], jnp.max(masked[h], axis=0, keepdims=True)) for h in hs]
        p_past = [jnp.exp2(masked[h] - m[h]) for h in hs]

    p_own = [jnp.exp2(s_own[h] - m[h]) for h in hs]
    l = [jnp.sum(p_own[h], axis=0, keepdims=True) for h in hs]
    acc = [dot(v_ref[own, hsl(h)].astype(BF16), p_own[h].astype(BF16), _TN) for h in hs]
    if ii > 0:
        l = [l[h] + jnp.sum(p_past[h], axis=0, keepdims=True) for h in hs]
        acc = [acc[h] + dot(v_ref[0:npast, hsl(h)].astype(BF16), p_past[h].astype(BF16), _TN) for h in hs]

    o = jnp.concatenate([(acc[h] / l[h]).T for h in hs], axis=1)
    o_ref[...] = (o * _silu(g_ref[...])).astype(o_ref.dtype)


def _moba_kernel(q_ref, k_ref, v_ref, g_ref, o_ref, kmean_ref):
    i = pl.program_id(1)
    blk = _MBLK
    nb = k_ref.shape[0] // blk

    @pl.when(i == 0)
    def _():
        kmean_ref[...] = jnp.zeros_like(kmean_ref)
        for j in range(nb):
            kmean_ref[j:j + 1, :] = jnp.mean(k_ref[j * blk:(j + 1) * blk, :], axis=0, keepdims=True)

    for ii in range(nb):
        pl.when(i == ii)(functools.partial(_moba_block, ii, q_ref, k_ref, v_ref, g_ref, o_ref, kmean_ref))


def _moba_mix(p3):
    b, s, _ = p3.shape
    nq = s // _MBLK
    npad = -(-nq // 16) * 16
    return pl.pallas_call(
        _moba_kernel,
        out_shape=jax.ShapeDtypeStruct((b, s, _MW), BF16),
        grid=(b, nq),
        in_specs=[
            pl.BlockSpec((None, _MBLK, _MW), lambda bi, i: (bi, i, _O_MQ // _MW)),
            pl.BlockSpec((None, s, _MW), lambda bi, i: (bi, 0, _O_MK // _MW)),
            pl.BlockSpec((None, s, _MW), lambda bi, i: (bi, 0, _O_MV // _MW)),
            pl.BlockSpec((None, _MBLK, _MW), lambda bi, i: (bi, i, _O_MG // _MW)),
        ],
        out_specs=pl.BlockSpec((None, _MBLK, _MW), lambda bi, i: (bi, i, 0)),
        scratch_shapes=[pltpu.VMEM((npad, _MW), F32)],
        compiler_params=pltpu.CompilerParams(
            dimension_semantics=("parallel", "arbitrary"),
            vmem_limit_bytes=_VMEM_LIMIT),
        name="moba_mix",
    )(p3, p3, p3, p3)


def _gla_kernel(q_ref, k_ref, v_ref, g_ref, low_ref, aup_ref, ab_ref, nw_ref, incl_ref, tri_ref,
                o_ref, st_ref):
    c = _C
    nh = _GKW // _GDK
    bs = range(q_ref.shape[0])

    @pl.when(pl.program_id(1) == 0)
    def _():
        st_ref[...] = jnp.zeros_like(st_ref)

    def lane_masks(width, head_w):
        lane_head = _idiv(_iota((1, width), 1), head_w)
        return [jnp.where(lane_head == h, 1.0, 0.0).astype(BF16) for h in range(nh)]

    def dot(a, b, dims=_NN):
        return lax.dot_general(a, b, dims, preferred_element_type=F32)

    hm_k, hm_v = lane_masks(_GKW, _GDK), lane_masks(_GW, _GDV)
    incl, tri = incl_ref[...], tri_ref[...]
    z = [_mm(low_ref[b], aup_ref[...], na=2, nb=2) + ab_ref[...] for b in bs]
    la = [-_softplus(-z[b]) * (_LOG2E / _GTEMP) for b in bs]
    bcum = [_mm(tri, la[b], nb=_P_SUM) for b in bs]
    b_last = [bcum[b][c - 1:c, :] for b in bs]
    k = [k_ref[b] for b in bs]
    qe = [(q_ref[b] * (_GDK ** -0.5) * jnp.exp2(bcum[b])).astype(BF16) for b in bs]
    ke_s = [_stack_lanes((k[b] * jnp.exp2(-bcum[b])).astype(BF16), hm_k) for b in bs]
    kd_s = [_stack_lanes((k[b] * jnp.exp2(b_last[b] - bcum[b])).astype(BF16), hm_k) for b in bs]
    v_s = [_stack_lanes(v_ref[b].astype(BF16), hm_v) for b in bs]

    att = [(dot(qe[b], ke_s[b], _NT) * incl).astype(BF16) for b in bs]
    st = [st_ref[b] for b in bs]
    o = [dot(att[b], v_s[b]) + dot(qe[b], st[b].astype(BF16), _NT) for b in bs]
    for b in bs:
        st_ref[b] = st[b] * jnp.exp2(b_last[b]) + dot(v_s[b], kd_s[b], _TN)

    for b in bs:
        outs = []
        for h in range(nh):
            oh = o[b][:, h * _GDV:(h + 1) * _GDV]
            ms = jnp.mean(oh * oh, axis=-1, keepdims=True)
            outs.append(oh * lax.rsqrt(ms + _EPS) * nw_ref[...])
        o_ref[b] = (jnp.concatenate(outs, axis=1) * _silu(g_ref[b])).astype(o_ref.dtype)


def _gla_mix(p3, a_up_p, a_b, norm_w, bb=8):
    b, s, _ = p3.shape
    nc = s // _C
    bb = min(bb, b)
    assert b % bb == 0
    n = _GKW // _GDK * _C
    ti = lax.broadcasted_iota(jnp.int32, (_C, n), 0)
    si = lax.broadcasted_iota(jnp.int32, (_C, n), 1) % _C
    incl = (si <= ti).astype(F32)
    tri = (lax.broadcasted_iota(jnp.int32, (_C, _C), 0) >= lax.broadcasted_iota(jnp.int32, (_C, _C), 1)).astype(BF16)

    def full(shape):
        return pl.BlockSpec(shape, lambda bi, c: (0,) * len(shape))

    return pl.pallas_call(
        _gla_kernel,
        out_shape=jax.ShapeDtypeStruct((b, s, _GW), BF16),
        grid=(b // bb, nc),
        in_specs=[
            pl.BlockSpec((bb, _C, _GKW), lambda bi, c: (bi, c, _O_GQ // _GKW)),
            pl.BlockSpec((bb, _C, _GKW), lambda bi, c: (bi, c, _O_GK // _GKW)),
            pl.BlockSpec((bb, _C, _GW), lambda bi, c: (bi, c, _O_GV // _GW)),
            pl.BlockSpec((bb, _C, _GW), lambda bi, c: (bi, c, _O_GG // _GW)),
            pl.BlockSpec((bb, _C, 128), lambda bi, c: (bi, c, _O_GL // 128)),
            full((128, _GKW)), full((1, _GKW)), full((1, _GDV)), full((_C, n)), full((_C, _C)),
        ],
        out_specs=pl.BlockSpec((bb, _C, _GW), lambda bi, c: (bi, c, 0)),
        scratch_shapes=[pltpu.VMEM((bb, _GW, _GKW), F32)],
        compiler_params=pltpu.CompilerParams(
            dimension_semantics=("parallel", "arbitrary"),
            vmem_limit_bytes=_VMEM_LIMIT),
        name="gla_mix",
    )(p3, p3, p3, p3, p3, a_up_p, a_b, norm_w, incl, tri)


def _outproj_kernel(x_ref, ya_ref, yb_ref, yc_ref, w_ref, fw_ref, o_ref, *, final_norm):
    acc = x_ref[...]
    acc = acc + jnp.dot(ya_ref[...], w_ref[0:_RW, :], preferred_element_type=F32)
    acc = acc + jnp.dot(yb_ref[...], w_ref[_RW:_RW + _MW, :], preferred_element_type=F32)
    acc = acc + jnp.dot(yc_ref[...], w_ref[_RW + _MW:, :], preferred_element_type=F32)
    if final_norm:
        ms = jnp.mean(acc * acc, axis=-1, keepdims=True)
        acc = acc * lax.rsqrt(ms + _EPS) * fw_ref[...]
    o_ref[...] = acc


def _out_proj(x2, ya, yb, yc, w_bf16, fw, final_norm, tm=512):
    t, d = x2.shape
    return pl.pallas_call(
        functools.partial(_outproj_kernel, final_norm=final_norm),
        out_shape=jax.ShapeDtypeStruct((t, d), F32),
        grid=(t // tm,),
        in_specs=[
            pl.BlockSpec((tm, d), lambda i: (i, 0)),
            pl.BlockSpec((tm, _RW), lambda i: (i, 0)),
            pl.BlockSpec((tm, _MW), lambda i: (i, 0)),
            pl.BlockSpec((tm, _GW), lambda i: (i, 0)),
            pl.BlockSpec((d, d), lambda i: (0, 0)),
            pl.BlockSpec((1, d), lambda i: (0, 0)),
        ],
        out_specs=pl.BlockSpec((tm, d), lambda i: (i, 0)),
        compiler_params=pltpu.CompilerParams(
            dimension_semantics=("parallel",),
            vmem_limit_bytes=_VMEM_LIMIT),
        name="out_proj",
    )(x2, ya, yb, yc, w_bf16, fw)


_WTILE = 128
_WSUB = 4


def _wprep_src_tile(j):
    src = jnp.int32(_D_IN // _WTILE)
    for a, b, o in _RUNS:
        lo = o // _WTILE
        hi = lo + -(-(b - a) // _WTILE)
        src = jnp.where((j >= lo) & (j < hi), j + (a - o) // _WTILE, src)
    return src


def _wprep_kernel(*refs):
    w_refs, o_ref = refs[:-1], refs[-1]
    last_dst = _RUNS[-1][2] + _RUNS[-1][1] - _RUNS[-1][0]
    for i, w_ref in enumerate(w_refs):
        j = pl.program_id(1) * _WSUB + i
        valid = jnp.clip(last_dst - j * _WTILE, 0, _WTILE)
        x = w_ref[...]
        o_ref[i * _WTILE:(i + 1) * _WTILE, :] = jnp.where(_iota(x.shape, 0) < valid, x, 0.0).astype(BF16)


def _prep_w_in(w_t):
    depth, n, d = w_t.shape
    assert n == _D_IN and all(a % _WTILE == 0 and o % _WTILE == 0 for a, _, o in _RUNS)
    step = _WSUB * _WTILE
    return pl.pallas_call(
        _wprep_kernel,
        out_shape=jax.ShapeDtypeStruct((depth, _NP, d), BF16),
        grid=(depth, -(-_NP // step)),
        in_specs=[pl.BlockSpec((None, _WTILE, d), lambda l, j, i=i: (l, _wprep_src_tile(j * _WSUB + i), 0))
                  for i in range(_WSUB)],
        out_specs=pl.BlockSpec((None, step, d), lambda l, j: (l, j, 0)),
        compiler_params=pltpu.CompilerParams(
            dimension_semantics=("parallel", "parallel"),
            vmem_limit_bytes=_VMEM_LIMIT),
        name="w_prep",
    )(*([w_t] * _WSUB))


def kernel(x, norm_w, w_in, w_out, rwkv_mu, rwkv_w0, rwkv_w2, rwkv_a0, rwkv_a2, rwkv_k_k, rwkv_k_a,
           rwkv_r_k, rwkv_ln_w, rwkv_ln_b, gla_a_up, gla_a_b, gla_norm_w, final_norm_w):
    b, s, d = x.shape
    depth = norm_w.shape[0]
    x2 = x.reshape(b * s, d)
    zero_lora = jnp.zeros((_RLORA, _RW), F32)
    fw = final_norm_w.reshape(1, d)
    w_in_p = _prep_w_in(jnp.swapaxes(w_in, 1, 2))
    for l in range(depth):
        w_out_l = w_out[l].astype(BF16)
        mu = rwkv_mu[l]
        mu_r, mu_k, mu_v = (mu[i * _RW:(i + 1) * _RW].reshape(1, _RW) for i in range(3))
        mu_wa = mu[3 * _RW:].reshape(1, 2 * _RLORA)
        w2p = jnp.concatenate([rwkv_w2[l], zero_lora], axis=0)
        a2p = jnp.concatenate([zero_lora, rwkv_a2[l]], axis=0)
        a_up_p = jnp.concatenate([gla_a_up[l], jnp.zeros((128 - _GRANK, _GKW), F32)], axis=0)

        p = _in_proj(x2, norm_w[l].reshape(1, d), w_in_p, l)
        p3 = p.reshape(b, s, _NP)
        row = lambda a: a.reshape(1, _RW)
        y_a = _rwkv_mix(p3, mu_r, mu_k, mu_v, mu_wa, row(rwkv_w0[l]), w2p, row(rwkv_a0[l]), a2p,
                        row(rwkv_k_k[l]), row(rwkv_k_a[l]), row(rwkv_r_k[l]),
                        row(rwkv_ln_w[l]), row(rwkv_ln_b[l]))
        y_b = _moba_mix(p3)
        y_c = _gla_mix(p3, a_up_p, gla_a_b[l].reshape(1, _GKW), gla_norm_w[l].reshape(1, _GDV))
        x2 = _out_proj(x2, y_a.reshape(b * s, _RW), y_b.reshape(b * s, _MW), y_c.reshape(b * s, _GW),
                       w_out_l, fw, final_norm=(l == depth - 1))
    return x2.reshape(b, s, d)
```

```python
import functools

import jax
import jax.numpy as jnp
from jax import lax
from jax.experimental import pallas as pl
from jax.experimental.pallas import tpu as pltpu

F32 = jnp.float32
BF16 = jnp.bfloat16

_D = 2048
_RW = 1024
_RN = 64
_RLORA = 64
_GN_EPS = 64e-5
_MW = 512
_MD = 128
_MBLK = 256
_MTOPK = 3
_GW = 512
_GKW = 256
_GDK = 64
_GDV = 128
_GRANK = 16
_GTEMP = 16.0
_EPS = 1e-6
_C = 64
_GRP = 256

_D_IN = 7824
_RUNS = ((0, 3072, 0),
         (3200, 4224, 3072),
         (4224, 6272, 4096),
         (6784, 7808, 6144),
         (6272, 6784, 7168),
         (3072, 3200, 7680),
         (7808, 7824, 7808))
_O_R, _O_K, _O_V, _O_GA = 0, 1024, 2048, 3072
_O_MQ, _O_MK, _O_MV, _O_MG = 4096, 4608, 5120, 5632
_O_GV, _O_GG = 6144, 6656
_O_GQ, _O_GK = 7168, 7424
_O_WA, _O_GL = 7680, 7808
_NP = 7936
assert all(o % _GW == 0 for o in (_O_GV, _O_GG))
assert all(o % _RW == 0 for o in (_O_R, _O_K, _O_V, _O_GA))
assert all(o % _GKW == 0 for o in (_O_GQ, _O_GK))
assert all(o % _MW == 0 for o in (_O_MQ, _O_MK, _O_MV, _O_MG))
assert all(o % 128 == 0 for o in (_O_WA, _O_GL))

_NN = (((1,), (0,)), ((), ()))
_NT = (((1,), (1,)), ((), ()))
_TN = (((0,), (0,)), ((), ()))

_VMEM_LIMIT = 56 * 1024 * 1024
_EXP_M05 = 0.6065306597126334
_LOG2E = 1.4426950408889634

_P_INV = 1
_P_MAIN = 1
_P_SUM = 2
_P_HEADSUM = 1


def _parts(x, n):
    if x.dtype == BF16:
        return [x]
    out, rem = [], x
    for i in range(n):
        p = rem.astype(BF16)
        out.append(p)
        if i + 1 < n:
            rem = rem - p.astype(F32)
    return out


def _mm(a, b, dims=_NN, na=1, nb=1):
    pa = a if isinstance(a, list) else _parts(a, na)
    pb = b if isinstance(b, list) else _parts(b, nb)
    order = max(len(pa), len(pb))
    free_axis = 1 if dims[0][0] == (0,) else 0
    m = pa[0].shape[free_axis]
    acc = None
    for j, y in enumerate(pb):
        xs = pa[:order - j]
        if not xs:
            continue
        x = xs[0] if len(xs) == 1 else jnp.concatenate(xs, axis=free_axis)
        t = lax.dot_general(x, y, dims, preferred_element_type=F32)
        for i in range(len(xs)):
            blk = t[i * m:(i + 1) * m]
            acc = blk if acc is None else acc + blk
    return acc


def _softplus(x):
    return jnp.maximum(x, 0.0) + jnp.log(1.0 + jnp.exp(-jnp.abs(x)))


def _sigmoid(x):
    return 1.0 / (1.0 + jnp.exp2(x * -_LOG2E))


def _silu(x):
    return x * _sigmoid(x)


def _iota(shape, dim):
    return lax.broadcasted_iota(jnp.int32, shape, dim)


def _idiv(x, n):
    assert n & (n - 1) == 0
    return jnp.right_shift(x, n.bit_length() - 1)


def _inproj_kernel(x_ref, nw_ref, w_ref, o_ref):
    x = x_ref[...]
    ms = jnp.mean(x * x, axis=-1, keepdims=True)
    h = (x * lax.rsqrt(ms + _EPS) * nw_ref[...]).astype(BF16)
    o_ref[...] = jnp.dot(h, w_ref[...], preferred_element_type=F32)


def _in_proj(x2, nw, w_bf16, layer, tm=256):
    t, d = x2.shape
    n = w_bf16.shape[2]
    return pl.pallas_call(
        _inproj_kernel,
        out_shape=jax.ShapeDtypeStruct((t, n), F32),
        grid=(t // tm,),
        in_specs=[
            pl.BlockSpec((tm, d), lambda i: (i, 0)),
            pl.BlockSpec((1, d), lambda i: (0, 0)),
            pl.BlockSpec((None, d, n), lambda i: (layer, 0, 0), pipeline_mode=pl.Buffered(1)),
        ],
        out_specs=pl.BlockSpec((tm, n), lambda i: (i, 0)),
        compiler_params=pltpu.CompilerParams(
            dimension_semantics=("parallel",),
            vmem_limit_bytes=_VMEM_LIMIT),
        name="in_proj",
    )(x2, nw, w_bf16)


def _token_shift(x, prev_ref, mu):
    c = x.shape[0]
    rows = _iota(x.shape, 0)
    xp = jnp.where(rows == 0, prev_ref[7:8, :], pltpu.roll(x, 1, 0))
    prev_ref[...] = x[c - 8:c, :]
    return x + mu * (xp - x)


def _head_sums(x, head_ones):
    c = x.shape[0]
    ng = x.shape[1] // _GRP
    rows = [p[:, g * _GRP:(g + 1) * _GRP] for p in _parts(x, _P_HEADSUM) for g in range(ng)]
    t = lax.dot_general(jnp.concatenate(rows, axis=0), head_ones, _NN, preferred_element_type=F32)
    cols = []
    for g in range(ng):
        acc = t[g * c:(g + 1) * c]
        for i in range(1, _P_HEADSUM):
            acc = acc + t[(i * ng + g) * c:(i * ng + g + 1) * c]
        cols.append(acc)
    return jnp.concatenate(cols, axis=1)


def _stack_lanes(x, lane_masks):
    return jnp.concatenate([x * m for m in lane_masks], axis=0)


def _rwkv_groups(at, rt, bt, kt, bh, kh, v, w_end, st_ref, strict, incl, eye_c, eye, same):
    c = _C
    nh = _GRP // _RN
    gs = range(at.shape[1] // _GRP)
    lane_head = _idiv(_iota((1, _GRP), 1), _RN)
    hm = [jnp.where(lane_head == h, 1.0, 0.0).astype(BF16) for h in range(nh)]

    def grp(x, g):
        return x[:, g * _GRP:(g + 1) * _GRP]

    def bf(x):
        return x.astype(BF16)

    def dot(a, b, dims=_NN):
        return lax.dot_general(a, b, dims, preferred_element_type=F32)

    def stack(x):
        return _stack_lanes(bf(x), hm)

    at_b = [bf(grp(at, g)) for g in gs]
    v_b = [bf(grp(v, g)) for g in gs]
    bh_b = [bf(grp(bh, g)) for g in gs]
    lhs = [jnp.concatenate([at_b[g], bf(grp(rt, g))], axis=0) for g in gs]
    rhs = [jnp.concatenate([stack(grp(bt, g)), stack(grp(kt, g))], axis=0) for g in gs]
    a_all = [dot(lhs[g], rhs[g], _NT) for g in gs]
    a_ab = [a_all[g][:c, :nh * c] * strict for g in gs]
    a_ak = [bf(a_all[g][:c, nh * c:] * strict) for g in gs]
    a_rb = [bf(a_all[g][c:, :nh * c] * incl) for g in gs]
    a_rk = [bf(a_all[g][c:, nh * c:] * incl) for g in gs]

    x = [eye_c + a_ab[g] for g in gs]
    p_b = [bf(a_ab[g]) for g in gs]
    p = [dot(p_b[g], _stack_lanes(p_b[g], hm)) for g in gs]
    n_sq = c.bit_length() - 2
    for it in range(n_sq):
        p_bd = [stack(p[g]) for g in gs]
        if it + 1 < n_sq:
            xp = [dot(jnp.concatenate([bf(x[g]), bf(p[g])], axis=0), p_bd[g]) for g in gs]
            x = [x[g] + xp[g][:c] for g in gs]
            p = [xp[g][c:] for g in gs]
        else:
            x = [x[g] + dot(bf(x[g]), p_bd[g]) for g in gs]
    t = [bf(x[g]) for g in gs]

    v_s = [_stack_lanes(v_b[g], hm) for g in gs]
    abar = [dot(t[g], _stack_lanes(at_b[g], hm)) for g in gs]
    akv = [dot(jnp.concatenate([a_ak[g], a_rk[g]], axis=0), v_s[g]) for g in gs]
    ubar = [dot(t[g], stack(akv[g][:c])) for g in gs]
    abar_b = [bf(abar[g]) for g in gs]
    ubar_b = [bf(ubar[g]) for g in gs]
    rbar = [grp(rt, g) + dot(a_rb[g], _stack_lanes(abar_b[g], hm)) for g in gs]
    ybar = [dot(a_rb[g], _stack_lanes(ubar_b[g], hm)) + akv[g][c:] for g in gs]
    mt = [eye * grp(w_end, g) + dot(bh_b[g], abar_b[g], _TN) for g in gs]
    dt = [dot(jnp.concatenate([bh_b[g], bf(grp(kh, g))], axis=0),
              jnp.concatenate([ubar_b[g], v_b[g]], axis=0), _TN) for g in gs]

    ys = [dot(jnp.concatenate([bf(rbar[g]), bf(mt[g])], axis=0), bf(st_ref[g])) for g in gs]
    for g in gs:
        st_ref[g] = same * (ys[g][c:] + dt[g])
    return jnp.concatenate([ys[g][:c] + ybar[g] for g in gs], axis=1)


def _rwkv_kernel(pr_ref, pk_ref, pv_ref, pg_ref, pwa_ref,
                 mur_ref, muk_ref, muv_ref, muwa_ref,
                 w0_ref, w2_ref, a0_ref, a2_ref, kk_ref, ka_ref, rk_ref, lnw_ref, lnb_ref,
                 strict_ref, incl_ref, eyec_ref, eye_ref, same_ref, hones_ref, tri_ref,
                 o_ref,
                 s_ref, prev_r, prev_k, prev_v, prev_wa):
    c = _C
    nb = pr_ref.shape[0]

    @pl.when(pl.program_id(1) == 0)
    def _():
        s_ref[...] = jnp.zeros_like(s_ref)
        prev_r[...] = jnp.zeros_like(prev_r)
        prev_k[...] = jnp.zeros_like(prev_k)
        prev_v[...] = jnp.zeros_like(prev_v)
        prev_wa[...] = jnp.zeros_like(prev_wa)

    head_ones = hones_ref[...]

    def lanes(ref):
        return jnp.concatenate([ref[b] for b in range(nb)], axis=1)

    def rep(ref):
        return jnp.concatenate([ref[...]] * nb, axis=1)

    def unrows(x):
        return jnp.concatenate([x[b * c:(b + 1) * c] for b in range(nb)], axis=1)

    r = _token_shift(lanes(pr_ref), prev_r, rep(mur_ref))
    k = _token_shift(lanes(pk_ref), prev_k, rep(muk_ref))
    v = _token_shift(lanes(pv_ref), prev_v, rep(muv_ref))
    wa = _token_shift(lanes(pwa_ref), prev_wa, rep(muwa_ref))
    wa = jnp.concatenate([wa[:, b * 128:(b + 1) * 128] for b in range(nb)], axis=0)

    wdec = rep(w0_ref) + unrows(_mm(jnp.tanh(wa), w2_ref[...], na=2))
    lw = -(_EXP_M05 * _LOG2E) * _sigmoid(wdec)
    a_lr = _sigmoid(rep(a0_ref) + unrows(_mm(wa, a2_ref[...], na=2)))
    kk = k * rep(kk_ref)
    kk = kk * lax.rsqrt(jnp.maximum(_head_sums(kk * kk, head_ones), 1e-24))
    k2 = k * (1.0 + (a_lr - 1.0) * rep(ka_ref))
    bb = kk * a_lr
    cl = _mm(tri_ref[...], lw, nb=_P_SUM)
    cl_end = cl[c - 1:c, :]
    e_in = jnp.exp2(cl)
    e_neg = jnp.exp2(-cl)
    w_end = jnp.exp2(cl_end)
    e_end = w_end * e_neg
    at = -kk * jnp.exp2(cl - lw)
    rt = r * e_in
    bt = bb * e_neg
    kt = k2 * e_neg
    bh = bb * e_end
    kh = k2 * e_end

    y = _rwkv_groups(at, rt, bt, kt, bh, kh, v, w_end, s_ref,
                     strict_ref[...], incl_ref[...], eyec_ref[...], eye_ref[...], same_ref[...])

    inv_n = 1.0 / _RN
    mean = _head_sums(y, head_ones) * inv_n
    yc = y - mean
    var = _head_sums(yc * yc, head_ones) * inv_n
    yn = yc * lax.rsqrt(var + _GN_EPS) * rep(lnw_ref) + rep(lnb_ref)
    bonus = _head_sums(r * k2 * rep(rk_ref), head_ones) * v
    out = yn + bonus
    for b in range(nb):
        o_ref[b] = (out[:, b * _RW:(b + 1) * _RW] * _silu(pg_ref[b])).astype(o_ref.dtype)


def _rwkv_consts():
    n = _GRP
    t = lax.broadcasted_iota(jnp.int32, (_C, n // _RN * _C), 0)
    s = lax.broadcasted_iota(jnp.int32, (_C, n // _RN * _C), 1) % _C
    strict = (s < t).astype(F32)
    incl = (s <= t).astype(F32)
    eye_c = (s == t).astype(F32)
    ri = lax.broadcasted_iota(jnp.int32, (n, n), 0)
    ci = lax.broadcasted_iota(jnp.int32, (n, n), 1)
    eye = (ri == ci).astype(F32)
    same = (ri // _RN) == (ci // _RN)
    ti = lax.broadcasted_iota(jnp.int32, (_C, _C), 0)
    tj = lax.broadcasted_iota(jnp.int32, (_C, _C), 1)
    tri = (ti >= tj).astype(BF16)
    return strict, incl, eye_c, eye, same.astype(F32), same.astype(BF16), tri


def _rwkv_mix(p3, mu_r, mu_k, mu_v, mu_wa, w0, w2p, a0, a2p, k_k, k_a, r_k, ln_w, ln_b, bb=4):
    b, s, _ = p3.shape
    nc = s // _C
    bb = min(bb, b)
    assert b % bb == 0
    ng = bb * _RW // _GRP

    def pspec(off):
        return pl.BlockSpec((bb, _C, _RW), lambda bi, c, o=off // _RW: (bi, c, o))

    def full(shape):
        return pl.BlockSpec(shape, lambda bi, c: (0,) * len(shape))

    vec = full((1, _RW))
    lora = full((2 * _RLORA, _RW))
    sq = full((_GRP, _GRP))
    cat = full((_C, _GRP // _RN * _C))
    return pl.pallas_call(
        _rwkv_kernel,
        out_shape=jax.ShapeDtypeStruct((b, s, _RW), BF16),
        grid=(b // bb, nc),
        in_specs=[
            pspec(_O_R), pspec(_O_K), pspec(_O_V), pspec(_O_GA),
            pl.BlockSpec((bb, _C, 128), lambda bi, c: (bi, c, _O_WA // 128)),
            vec, vec, vec, full((1, 2 * _RLORA)),
            vec, lora, vec, lora, vec, vec, vec, vec, vec,
            cat, cat, cat, sq, sq, sq, full((_C, _C)),
        ],
        out_specs=pl.BlockSpec((bb, _C, _RW), lambda bi, c: (bi, c, 0)),
        scratch_shapes=[
            pltpu.VMEM((ng, _GRP, _GRP), F32),
            pltpu.VMEM((8, bb * _RW), F32), pltpu.VMEM((8, bb * _RW), F32),
            pltpu.VMEM((8, bb * _RW), F32), pltpu.VMEM((8, bb * 2 * _RLORA), F32),
        ],
        compiler_params=pltpu.CompilerParams(
            dimension_semantics=("parallel", "arbitrary"),
            vmem_limit_bytes=_VMEM_LIMIT),
        name="rwkv_mix",
    )(p3, p3, p3, p3, p3, mu_r, mu_k, mu_v, mu_wa, w0, w2p, a0, a2p, k_k, k_a, r_k, ln_w, ln_b,
      *_rwkv_consts())


def _moba_block(ii, q_ref, k_ref, v_ref, g_ref, o_ref, kmean_ref):
    blk = _MBLK
    nh = _MW // _MD
    hs = range(nh)
    scale = _MD ** -0.5 * _LOG2E
    neg = -1e30
    npast = ii * blk

    def hsl(h):
        return slice(h * _MD, (h + 1) * _MD)

    def dot(a, b, dims):
        return lax.dot_general(a, b, dims, preferred_element_type=F32)

    q = [q_ref[:, hsl(h)] for h in hs]
    qs = [(q[h] * scale).astype(BF16) for h in hs]
    own = slice(npast, npast + blk)
    kpos = _iota((blk, blk), 0)
    qpos = _iota((blk, blk), 1)
    s_own = [jnp.where(kpos <= qpos, dot(k_ref[own, hsl(h)].astype(BF16), qs[h], _NT), neg) for h in hs]
    m = [jnp.max(s_own[h], axis=0, keepdims=True) for h in hs]

    if ii > 0:
        npad = kmean_ref.shape[0]
        bid = _iota((npad, blk), 0)
        gate = [jnp.where(bid < ii, _mm(kmean_ref[:, hsl(h)], q[h], _NT, na=2, nb=2), -jnp.inf)
                for h in hs]
        s_past = [dot(k_ref[0:npast, hsl(h)].astype(BF16), qs[h], _NT) for h in hs]
        masked = []
        for h in hs:
            rows = []
            for j in range(ii):
                gj = gate[h][j:j + 1, :]
                ahead = (gate[h] > gj) | ((gate[h] == gj) & (bid < j))
                rank = jnp.sum(jnp.where(ahead, 1.0, 0.0), axis=0, keepdims=True)
                rows.append(jnp.where(rank < float(_MTOPK), s_past[h][j * blk:(j + 1) * blk], neg))
            masked.append(rows[0] if ii == 1 else jnp.concatenate(rows, axis=0))
        m = [jnp.maximum(m[h], jnp.max(masked[h], axis=0, keepdims=True)) for h in hs]
        p_past = [jnp.exp2(masked[h] - m[h]) for h in hs]

    p_own = [jnp.exp2(s_own[h] - m[h]) for h in hs]
    l = [jnp.sum(p_own[h], axis=0, keepdims=True) for h in hs]
    acc = [dot(v_ref[own, hsl(h)].astype(BF16), p_own[h].astype(BF16), _TN) for h in hs]
    if ii > 0:
        l = [l[h] + jnp.sum(p_past[h], axis=0, keepdims=True) for h in hs]
        acc = [acc[h] + dot(v_ref[0:npast, hsl(h)].astype(BF16), p_past[h].astype(BF16), _TN) for h in hs]

    o = jnp.concatenate([(acc[h] / l[h]).T for h in hs], axis=1)
    o_ref[...] = (o * _silu(g_ref[...])).astype(o_ref.dtype)


def _moba_kernel(q_ref, k_ref, v_ref, g_ref, o_ref, kmean_ref):
    i = pl.program_id(1)
    blk = _MBLK
    nb = k_ref.shape[0] // blk

    @pl.when(i == 0)
    def _():
        kmean_ref[...] = jnp.zeros_like(kmean_ref)
        for j in range(nb):
            kmean_ref[j:j + 1, :] = jnp.mean(k_ref[j * blk:(j + 1) * blk, :], axis=0, keepdims=True)

    for ii in range(nb):
        pl.when(i == ii)(functools.partial(_moba_block, ii, q_ref, k_ref, v_ref, g_ref, o_ref, kmean_ref))


def _moba_mix(p3):
    b, s, _ = p3.shape
    nq = s // _MBLK
    npad = -(-nq // 16) * 16
    return pl.pallas_call(
        _moba_kernel,
        out_shape=jax.ShapeDtypeStruct((b, s, _MW), BF16),
        grid=(b, nq),
        in_specs=[
            pl.BlockSpec((None, _MBLK, _MW), lambda bi, i: (bi, i, _O_MQ // _MW)),
            pl.BlockSpec((None, s, _MW), lambda bi, i: (bi, 0, _O_MK // _MW)),
            pl.BlockSpec((None, s, _MW), lambda bi, i: (bi, 0, _O_MV // _MW)),
            pl.BlockSpec((None, _MBLK, _MW), lambda bi, i: (bi, i, _O_MG // _MW)),
        ],
        out_specs=pl.BlockSpec((None, _MBLK, _MW), lambda bi, i: (bi, i, 0)),
        scratch_shapes=[pltpu.VMEM((npad, _MW), F32)],
        compiler_params=pltpu.CompilerParams(
            dimension_semantics=("parallel", "arbitrary"),
            vmem_limit_bytes=_VMEM_LIMIT),
        name="moba_mix",
    )(p3, p3, p3, p3)


def _gla_kernel(q_ref, k_ref, v_ref, g_ref, low_ref, aup_ref, ab_ref, nw_ref, incl_ref, tri_ref,
                o_ref, st_ref):
    c = _C
    nh = _GKW // _GDK
    bs = range(q_ref.shape[0])

    @pl.when(pl.program_id(1) == 0)
    def _():
        st_ref[...] = jnp.zeros_like(st_ref)

    def lane_masks(width, head_w):
        lane_head = _idiv(_iota((1, width), 1), head_w)
        return [jnp.where(lane_head == h, 1.0, 0.0).astype(BF16) for h in range(nh)]

    def dot(a, b, dims=_NN):
        return lax.dot_general(a, b, dims, preferred_element_type=F32)

    hm_k, hm_v = lane_masks(_GKW, _GDK), lane_masks(_GW, _GDV)
    incl, tri = incl_ref[...], tri_ref[...]
    z = [_mm(low_ref[b], aup_ref[...], na=2, nb=2) + ab_ref[...] for b in bs]
    la = [-_softplus(-z[b]) * (_LOG2E / _GTEMP) for b in bs]
    bcum = [_mm(tri, la[b], nb=_P_SUM) for b in bs]
    b_last = [bcum[b][c - 1:c, :] for b in bs]
    k = [k_ref[b] for b in bs]
    qe = [(q_ref[b] * (_GDK ** -0.5) * jnp.exp2(bcum[b])).astype(BF16) for b in bs]
    ke_s = [_stack_lanes((k[b] * jnp.exp2(-bcum[b])).astype(BF16), hm_k) for b in bs]
    kd_s = [_stack_lanes((k[b] * jnp.exp2(b_last[b] - bcum[b])).astype(BF16), hm_k) for b in bs]
    v_s = [_stack_lanes(v_ref[b].astype(BF16), hm_v) for b in bs]

    att = [(dot(qe[b], ke_s[b], _NT) * incl).astype(BF16) for b in bs]
    st = [st_ref[b] for b in bs]
    o = [dot(att[b], v_s[b]) + dot(qe[b], st[b].astype(BF16), _NT) for b in bs]
    for b in bs:
        st_ref[b] = st[b] * jnp.exp2(b_last[b]) + dot(v_s[b], kd_s[b], _TN)

    for b in bs:
        outs = []
        for h in range(nh):
            oh = o[b][:, h * _GDV:(h + 1) * _GDV]
            ms = jnp.mean(oh * oh, axis=-1, keepdims=True)
            outs.append(oh * lax.rsqrt(ms + _EPS) * nw_ref[...])
        o_ref[b] = (jnp.concatenate(outs, axis=1) * _silu(g_ref[b])).astype(o_ref.dtype)


def _gla_mix(p3, a_up_p, a_b, norm_w, bb=8):
    b, s, _ = p3.shape
    nc = s // _C
    bb = min(bb, b)
    assert b % bb == 0
    n = _GKW // _GDK * _C
    ti = lax.broadcasted_iota(jnp.int32, (_C, n), 0)
    si = lax.broadcasted_iota(jnp.int32, (_C, n), 1) % _C
    incl = (si <= ti).astype(F32)
    tri = (lax.broadcasted_iota(jnp.int32, (_C, _C), 0) >= lax.broadcasted_iota(jnp.int32, (_C, _C), 1)).astype(BF16)

    def full(shape):
        return pl.BlockSpec(shape, lambda bi, c: (0,) * len(shape))

    return pl.pallas_call(
        _gla_kernel,
        out_shape=jax.ShapeDtypeStruct((b, s, _GW), BF16),
        grid=(b // bb, nc),
        in_specs=[
            pl.BlockSpec((bb, _C, _GKW), lambda bi, c: (bi, c, _O_GQ // _GKW)),
            pl.BlockSpec((bb, _C, _GKW), lambda bi, c: (bi, c, _O_GK // _GKW)),
            pl.BlockSpec((bb, _C, _GW), lambda bi, c: (bi, c, _O_GV // _GW)),
            pl.BlockSpec((bb, _C, _GW), lambda bi, c: (bi, c, _O_GG // _GW)),
            pl.BlockSpec((bb, _C, 128), lambda bi, c: (bi, c, _O_GL // 128)),
            full((128, _GKW)), full((1, _GKW)), full((1, _GDV)), full((_C, n)), full((_C, _C)),
        ],
        out_specs=pl.BlockSpec((bb, _C, _GW), lambda bi, c: (bi, c, 0)),
        scratch_shapes=[pltpu.VMEM((bb, _GW, _GKW), F32)],
        compiler_params=pltpu.CompilerParams(
            dimension_semantics=("parallel", "arbitrary"),
            vmem_limit_bytes=_VMEM_LIMIT),
        name="gla_mix",
    )(p3, p3, p3, p3, p3, a_up_p, a_b, norm_w, incl, tri)


def _outproj_kernel(x_ref, ya_ref, yb_ref, yc_ref, w_ref, fw_ref, o_ref, *, final_norm):
    mix = jnp.concatenate([ya_ref[...], yb_ref[...], yc_ref[...]], axis=1)
    acc = x_ref[...] + jnp.dot(mix, w_ref[...], preferred_element_type=F32)
    if final_norm:
        ms = jnp.mean(acc * acc, axis=-1, keepdims=True)
        acc = acc * lax.rsqrt(ms + _EPS) * fw_ref[...]
    o_ref[...] = acc


def _out_proj(x2, ya, yb, yc, w_bf16, fw, final_norm, tm=512):
    t, d = x2.shape
    return pl.pallas_call(
        functools.partial(_outproj_kernel, final_norm=final_norm),
        out_shape=jax.ShapeDtypeStruct((t, d), F32),
        grid=(t // tm,),
        in_specs=[
            pl.BlockSpec((tm, d), lambda i: (i, 0)),
            pl.BlockSpec((tm, _RW), lambda i: (i, 0)),
            pl.BlockSpec((tm, _MW), lambda i: (i, 0)),
            pl.BlockSpec((tm, _GW), lambda i: (i, 0)),
            pl.BlockSpec((d, d), lambda i: (0, 0)),
            pl.BlockSpec((1, d), lambda i: (0, 0)),
        ],
        out_specs=pl.BlockSpec((tm, d), lambda i: (i, 0)),
        compiler_params=pltpu.CompilerParams(
            dimension_semantics=("parallel",),
            vmem_limit_bytes=_VMEM_LIMIT),
        name="out_proj",
    )(x2, ya, yb, yc, w_bf16, fw)


_WTILE = 128
_WSUB = 4


def _wprep_src_tile(j):
    src = jnp.int32(_D_IN // _WTILE)
    for a, b, o in _RUNS:
        lo = o // _WTILE
        hi = lo + -(-(b - a) // _WTILE)
        src = jnp.where((j >= lo) & (j < hi), j + (a - o) // _WTILE, src)
    return src


def _wprep_kernel(*refs):
    w_refs, o_ref = refs[:-1], refs[-1]
    last_dst = _RUNS[-1][2] + _RUNS[-1][1] - _RUNS[-1][0]
    for i, w_ref in enumerate(w_refs):
        j = pl.program_id(1) * _WSUB + i
        valid = jnp.clip(last_dst - j * _WTILE, 0, _WTILE)
        x = w_ref[...]
        x = jnp.where(_iota(x.shape, 0) < valid, x, 0.0)
        o_ref[:, i * _WTILE:(i + 1) * _WTILE] = x.T.astype(BF16)


def _prep_w_in(w_t):
    depth, n, d = w_t.shape
    assert n == _D_IN and all(a % _WTILE == 0 and o % _WTILE == 0 for a, _, o in _RUNS)
    step = _WSUB * _WTILE
    return pl.pallas_call(
        _wprep_kernel,
        out_shape=jax.ShapeDtypeStruct((depth, d, _NP), BF16),
        grid=(depth, -(-_NP // step)),
        in_specs=[pl.BlockSpec((None, _WTILE, d), lambda l, j, i=i: (l, _wprep_src_tile(j * _WSUB + i), 0))
                  for i in range(_WSUB)],
        out_specs=pl.BlockSpec((None, d, step), lambda l, j: (l, 0, j)),
        compiler_params=pltpu.CompilerParams(
            dimension_semantics=("parallel", "parallel"),
            vmem_limit_bytes=_VMEM_LIMIT),
        name="w_prep",
    )(*([w_t] * _WSUB))


def kernel(x, norm_w, w_in, w_out, rwkv_mu, rwkv_w0, rwkv_w2, rwkv_a0, rwkv_a2, rwkv_k_k, rwkv_k_a,
           rwkv_r_k, rwkv_ln_w, rwkv_ln_b, gla_a_up, gla_a_b, gla_norm_w, final_norm_w):
    b, s, d = x.shape
    depth = norm_w.shape[0]
    x2 = x.reshape(b * s, d)
    zero_lora = jnp.zeros((_RLORA, _RW), F32)
    fw = final_norm_w.reshape(1, d)
    w_in_p = _prep_w_in(jnp.swapaxes(w_in, 1, 2))
    for l in range(depth):
        w_out_l = w_out[l].astype(BF16)
        mu = rwkv_mu[l]
        mu_r, mu_k, mu_v = (mu[i * _RW:(i + 1) * _RW].reshape(1, _RW) for i in range(3))
        mu_wa = mu[3 * _RW:].reshape(1, 2 * _RLORA)
        w2p = jnp.concatenate([rwkv_w2[l], zero_lora], axis=0)
        a2p = jnp.concatenate([zero_lora, rwkv_a2[l]], axis=0)
        a_up_p = jnp.concatenate([gla_a_up[l], jnp.zeros((128 - _GRANK, _GKW), F32)], axis=0)

        p = _in_proj(x2, norm_w[l].reshape(1, d), w_in_p, l)
        p3 = p.reshape(b, s, _NP)
        row = lambda a: a.reshape(1, _RW)
        y_a = _rwkv_mix(p3, mu_r, mu_k, mu_v, mu_wa, row(rwkv_w0[l]), w2p, row(rwkv_a0[l]), a2p,
                        row(rwkv_k_k[l]), row(rwkv_k_a[l]), row(rwkv_r_k[l]),
                        row(rwkv_ln_w[l]), row(rwkv_ln_b[l]))
        y_b = _moba_mix(p3)
        y_c = _gla_mix(p3, a_up_p, gla_a_b[l].reshape(1, _GKW), gla_norm_w[l].reshape(1, _GDV))
        x2 = _out_proj(x2, y_a.reshape(b * s, _RW), y_b.reshape(b * s, _MW), y_c.reshape(b * s, _GW),
                       w_out_l, fw, final_norm=(l == depth - 1))
    return x2.reshape(b, s, d)
```

```python
import functools

import jax
import jax.numpy as jnp
from jax import lax
from jax.experimental import pallas as pl
from jax.experimental.pallas import tpu as pltpu

F32 = jnp.float32
BF16 = jnp.bfloat16

_D = 2048
_RW = 1024
_RN = 64
_RLORA = 64
_GN_EPS = 64e-5
_MW = 512
_MD = 128
_MBLK = 256
_MTOPK = 3
_GW = 512
_GKW = 256
_GDK = 64
_GDV = 128
_GRANK = 16
_GTEMP = 16.0
_EPS = 1e-6
_C = 64
_GRP = 256

_D_IN = 7824
_RUNS = ((0, 3072, 0),
         (3200, 4224, 3072),
         (4224, 6272, 4096),
         (6784, 7808, 6144),
         (6272, 6784, 7168),
         (3072, 3200, 7680),
         (7808, 7824, 7808))
_O_R, _O_K, _O_V, _O_GA = 0, 1024, 2048, 3072
_O_MQ, _O_MK, _O_MV, _O_MG = 4096, 4608, 5120, 5632
_O_GV, _O_GG = 6144, 6656
_O_GQ, _O_GK = 7168, 7424
_O_WA, _O_GL = 7680, 7808
_NP = 7936
assert all(o % _GW == 0 for o in (_O_GV, _O_GG))
assert all(o % _RW == 0 for o in (_O_R, _O_K, _O_V, _O_GA))
assert all(o % _GKW == 0 for o in (_O_GQ, _O_GK))
assert all(o % _MW == 0 for o in (_O_MQ, _O_MK, _O_MV, _O_MG))
assert all(o % 128 == 0 for o in (_O_WA, _O_GL))

_NN = (((1,), (0,)), ((), ()))
_NT = (((1,), (1,)), ((), ()))
_TN = (((0,), (0,)), ((), ()))

_VMEM_LIMIT = 56 * 1024 * 1024
_EXP_M05 = 0.6065306597126334
_LOG2E = 1.4426950408889634

_P_INV = 1
_P_MAIN = 1
_P_SUM = 2
_P_HEADSUM = 1


def _parts(x, n):
    if x.dtype == BF16:
        return [x]
    out, rem = [], x
    for i in range(n):
        p = rem.astype(BF16)
        out.append(p)
        if i + 1 < n:
            rem = rem - p.astype(F32)
    return out


def _mm(a, b, dims=_NN, na=1, nb=1):
    pa = a if isinstance(a, list) else _parts(a, na)
    pb = b if isinstance(b, list) else _parts(b, nb)
    order = max(len(pa), len(pb))
    free_axis = 1 if dims[0][0] == (0,) else 0
    m = pa[0].shape[free_axis]
    acc = None
    for j, y in enumerate(pb):
        xs = pa[:order - j]
        if not xs:
            continue
        x = xs[0] if len(xs) == 1 else jnp.concatenate(xs, axis=free_axis)
        t = lax.dot_general(x, y, dims, preferred_element_type=F32)
        for i in range(len(xs)):
            blk = t[i * m:(i + 1) * m]
            acc = blk if acc is None else acc + blk
    return acc


def _softplus(x):
    return jnp.maximum(x, 0.0) + jnp.log(1.0 + jnp.exp(-jnp.abs(x)))


def _sigmoid(x):
    return 1.0 / (1.0 + jnp.exp2(x * -_LOG2E))


def _silu(x):
    return x * _sigmoid(x)


def _iota(shape, dim):
    return lax.broadcasted_iota(jnp.int32, shape, dim)


def _idiv(x, n):
    assert n & (n - 1) == 0
    return jnp.right_shift(x, n.bit_length() - 1)


def _inproj_kernel(x_ref, nw_ref, w_ref, o_ref):
    x = x_ref[...]
    ms = jnp.mean(x * x, axis=-1, keepdims=True)
    h = (x * lax.rsqrt(ms + _EPS) * nw_ref[...]).astype(BF16)
    o_ref[...] = jnp.dot(h, w_ref[...], preferred_element_type=F32)


def _in_proj(x2, nw, w_bf16, layer, tm=256):
    t, d = x2.shape
    n = w_bf16.shape[2]
    return pl.pallas_call(
        _inproj_kernel,
        out_shape=jax.ShapeDtypeStruct((t, n), F32),
        grid=(t // tm,),
        in_specs=[
            pl.BlockSpec((tm, d), lambda i: (i, 0)),
            pl.BlockSpec((1, d), lambda i: (0, 0)),
            pl.BlockSpec((None, d, n), lambda i: (layer, 0, 0), pipeline_mode=pl.Buffered(1)),
        ],
        out_specs=pl.BlockSpec((tm, n), lambda i: (i, 0)),
        compiler_params=pltpu.CompilerParams(
            dimension_semantics=("parallel",),
            vmem_limit_bytes=_VMEM_LIMIT),
        name="in_proj",
    )(x2, nw, w_bf16)


def _token_shift(x, prev_ref, mu):
    c = x.shape[0]
    rows = _iota(x.shape, 0)
    xp = jnp.where(rows == 0, prev_ref[7:8, :], pltpu.roll(x, 1, 0))
    prev_ref[...] = x[c - 8:c, :]
    return x + mu * (xp - x)


def _head_sums(x, head_ones):
    c = x.shape[0]
    ng = x.shape[1] // _GRP
    rows = [p[:, g * _GRP:(g + 1) * _GRP] for p in _parts(x, _P_HEADSUM) for g in range(ng)]
    t = lax.dot_general(jnp.concatenate(rows, axis=0), head_ones, _NN, preferred_element_type=F32)
    cols = []
    for g in range(ng):
        acc = t[g * c:(g + 1) * c]
        for i in range(1, _P_HEADSUM):
            acc = acc + t[(i * ng + g) * c:(i * ng + g + 1) * c]
        cols.append(acc)
    return jnp.concatenate(cols, axis=1)


def _stack_lanes(x, lane_masks):
    return jnp.concatenate([x * m for m in lane_masks], axis=0)


def _rwkv_groups(at, rt, bt, kt, bh, kh, v, w_end, st_ref, strict, incl, eye_c, eye, same):
    c = _C
    nh = _GRP // _RN
    gs = range(at.shape[1] // _GRP)
    lane_head = _idiv(_iota((1, _GRP), 1), _RN)
    hm = [jnp.where(lane_head == h, 1.0, 0.0).astype(BF16) for h in range(nh)]

    def grp(x, g):
        return x[:, g * _GRP:(g + 1) * _GRP]

    def bf(x):
        return x.astype(BF16)

    def dot(a, b, dims=_NN):
        return lax.dot_general(a, b, dims, preferred_element_type=F32)

    def stack(x):
        return _stack_lanes(bf(x), hm)

    at_b = [bf(grp(at, g)) for g in gs]
    v_b = [bf(grp(v, g)) for g in gs]
    bh_b = [bf(grp(bh, g)) for g in gs]
    lhs = [jnp.concatenate([at_b[g], bf(grp(rt, g))], axis=0) for g in gs]
    rhs = [jnp.concatenate([stack(grp(bt, g)), stack(grp(kt, g))], axis=0) for g in gs]
    a_all = [dot(lhs[g], rhs[g], _NT) for g in gs]
    a_ab = [a_all[g][:c, :nh * c] * strict for g in gs]
    a_ak = [bf(a_all[g][:c, nh * c:] * strict) for g in gs]
    a_rb = [bf(a_all[g][c:, :nh * c] * incl) for g in gs]
    a_rk = [bf(a_all[g][c:, nh * c:] * incl) for g in gs]

    x = [eye_c + a_ab[g] for g in gs]
    p_b = [bf(a_ab[g]) for g in gs]
    p = [dot(p_b[g], _stack_lanes(p_b[g], hm)) for g in gs]
    n_sq = c.bit_length() - 2
    for it in range(n_sq):
        p_bd = [stack(p[g]) for g in gs]
        if it + 1 < n_sq:
            xp = [dot(jnp.concatenate([bf(x[g]), bf(p[g])], axis=0), p_bd[g]) for g in gs]
            x = [x[g] + xp[g][:c] for g in gs]
            p = [xp[g][c:] for g in gs]
        else:
            x = [x[g] + dot(bf(x[g]), p_bd[g]) for g in gs]
    t = [bf(x[g]) for g in gs]

    v_s = [_stack_lanes(v_b[g], hm) for g in gs]
    abar = [dot(t[g], _stack_lanes(at_b[g], hm)) for g in gs]
    akv = [dot(jnp.concatenate([a_ak[g], a_rk[g]], axis=0), v_s[g]) for g in gs]
    ubar = [dot(t[g], stack(akv[g][:c])) for g in gs]
    abar_b = [bf(abar[g]) for g in gs]
    ubar_b = [bf(ubar[g]) for g in gs]
    rbar = [grp(rt, g) + dot(a_rb[g], _stack_lanes(abar_b[g], hm)) for g in gs]
    ybar = [dot(a_rb[g], _stack_lanes(ubar_b[g], hm)) + akv[g][c:] for g in gs]
    mt = [eye * grp(w_end, g) + dot(bh_b[g], abar_b[g], _TN) for g in gs]
    dt = [dot(jnp.concatenate([bh_b[g], bf(grp(kh, g))], axis=0),
              jnp.concatenate([ubar_b[g], v_b[g]], axis=0), _TN) for g in gs]

    ys = [dot(jnp.concatenate([bf(rbar[g]), bf(mt[g])], axis=0), bf(st_ref[g])) for g in gs]
    for g in gs:
        st_ref[g] = same * (ys[g][c:] + dt[g])
    return jnp.concatenate([ys[g][:c] + ybar[g] for g in gs], axis=1)


def _rwkv_kernel(pr_ref, pk_ref, pv_ref, pg_ref, pwa_ref,
                 mur_ref, muk_ref, muv_ref, muwa_ref,
                 w0_ref, w2_ref, a0_ref, a2_ref, kk_ref, ka_ref, rk_ref, lnw_ref, lnb_ref,
                 strict_ref, incl_ref, eyec_ref, eye_ref, same_ref, hones_ref, tri_ref,
                 o_ref,
                 s_ref, prev_r, prev_k, prev_v, prev_wa):
    c = _C
    nb = pr_ref.shape[0]

    @pl.when(pl.program_id(1) == 0)
    def _():
        s_ref[...] = jnp.zeros_like(s_ref)
        prev_r[...] = jnp.zeros_like(prev_r)
        prev_k[...] = jnp.zeros_like(prev_k)
        prev_v[...] = jnp.zeros_like(prev_v)
        prev_wa[...] = jnp.zeros_like(prev_wa)

    head_ones = hones_ref[...]

    def lanes(ref):
        return jnp.concatenate([ref[b] for b in range(nb)], axis=1)

    def rep(ref):
        return jnp.concatenate([ref[...]] * nb, axis=1)

    def unrows(x):
        return jnp.concatenate([x[b * c:(b + 1) * c] for b in range(nb)], axis=1)

    r = _token_shift(lanes(pr_ref), prev_r, rep(mur_ref))
    k = _token_shift(lanes(pk_ref), prev_k, rep(muk_ref))
    v = _token_shift(lanes(pv_ref), prev_v, rep(muv_ref))
    wa = _token_shift(lanes(pwa_ref), prev_wa, rep(muwa_ref))
    wa = jnp.concatenate([wa[:, b * 128:(b + 1) * 128] for b in range(nb)], axis=0)

    wdec = rep(w0_ref) + unrows(_mm(jnp.tanh(wa), w2_ref[...]))
    lw = -(_EXP_M05 * _LOG2E) * _sigmoid(wdec)
    a_lr = _sigmoid(rep(a0_ref) + unrows(_mm(wa, a2_ref[...])))
    kk = k * rep(kk_ref)
    kk = kk * lax.rsqrt(jnp.maximum(_head_sums(kk * kk, head_ones), 1e-24))
    k2 = k * (1.0 + (a_lr - 1.0) * rep(ka_ref))
    bb = kk * a_lr
    cl = _mm(tri_ref[...], lw, nb=_P_SUM)
    cl_end = cl[c - 1:c, :]
    e_in = jnp.exp2(cl)
    e_neg = jnp.exp2(-cl)
    w_end = jnp.exp2(cl_end)
    e_end = w_end * e_neg
    at = -kk * jnp.exp2(cl - lw)
    rt = r * e_in
    bt = bb * e_neg
    kt = k2 * e_neg
    bh = bb * e_end
    kh = k2 * e_end

    y = _rwkv_groups(at, rt, bt, kt, bh, kh, v, w_end, s_ref,
                     strict_ref[...], incl_ref[...], eyec_ref[...], eye_ref[...], same_ref[...])

    inv_n = 1.0 / _RN
    mean = _head_sums(y, head_ones) * inv_n
    yc = y - mean
    var = _head_sums(yc * yc, head_ones) * inv_n
    yn = yc * lax.rsqrt(var + _GN_EPS) * rep(lnw_ref) + rep(lnb_ref)
    bonus = _head_sums(r * k2 * rep(rk_ref), head_ones) * v
    out = yn + bonus
    for b in range(nb):
        o_ref[b] = (out[:, b * _RW:(b + 1) * _RW] * _silu(pg_ref[b])).astype(o_ref.dtype)


def _rwkv_consts():
    n = _GRP
    t = lax.broadcasted_iota(jnp.int32, (_C, n // _RN * _C), 0)
    s = lax.broadcasted_iota(jnp.int32, (_C, n // _RN * _C), 1) % _C
    strict = (s < t).astype(F32)
    incl = (s <= t).astype(F32)
    eye_c = (s == t).astype(F32)
    ri = lax.broadcasted_iota(jnp.int32, (n, n), 0)
    ci = lax.broadcasted_iota(jnp.int32, (n, n), 1)
    eye = (ri == ci).astype(F32)
    same = (ri // _RN) == (ci // _RN)
    ti = lax.broadcasted_iota(jnp.int32, (_C, _C), 0)
    tj = lax.broadcasted_iota(jnp.int32, (_C, _C), 1)
    tri = (ti >= tj).astype(BF16)
    return strict, incl, eye_c, eye, same.astype(F32), same.astype(BF16), tri


def _rwkv_mix(p3, mu_r, mu_k, mu_v, mu_wa, w0, w2p, a0, a2p, k_k, k_a, r_k, ln_w, ln_b, bb=4):
    b, s, _ = p3.shape
    nc = s // _C
    bb = min(bb, b)
    assert b % bb == 0
    ng = bb * _RW // _GRP

    def pspec(off):
        return pl.BlockSpec((bb, _C, _RW), lambda bi, c, o=off // _RW: (bi, c, o))

    def full(shape):
        return pl.BlockSpec(shape, lambda bi, c: (0,) * len(shape))

    vec = full((1, _RW))
    lora = full((2 * _RLORA, _RW))
    sq = full((_GRP, _GRP))
    cat = full((_C, _GRP // _RN * _C))
    return pl.pallas_call(
        _rwkv_kernel,
        out_shape=jax.ShapeDtypeStruct((b, s, _RW), BF16),
        grid=(b // bb, nc),
        in_specs=[
            pspec(_O_R), pspec(_O_K), pspec(_O_V), pspec(_O_GA),
            pl.BlockSpec((bb, _C, 128), lambda bi, c: (bi, c, _O_WA // 128)),
            vec, vec, vec, full((1, 2 * _RLORA)),
            vec, lora, vec, lora, vec, vec, vec, vec, vec,
            cat, cat, cat, sq, sq, sq, full((_C, _C)),
        ],
        out_specs=pl.BlockSpec((bb, _C, _RW), lambda bi, c: (bi, c, 0)),
        scratch_shapes=[
            pltpu.VMEM((ng, _GRP, _GRP), F32),
            pltpu.VMEM((8, bb * _RW), F32), pltpu.VMEM((8, bb * _RW), F32),
            pltpu.VMEM((8, bb * _RW), F32), pltpu.VMEM((8, bb * 2 * _RLORA), F32),
        ],
        compiler_params=pltpu.CompilerParams(
            dimension_semantics=("parallel", "arbitrary"),
            vmem_limit_bytes=_VMEM_LIMIT),
        name="rwkv_mix",
    )(p3, p3, p3, p3, p3, mu_r, mu_k, mu_v, mu_wa, w0, w2p, a0, a2p, k_k, k_a, r_k, ln_w, ln_b,
      *_rwkv_consts())


def _moba_block(ii, q_ref, k_ref, v_ref, g_ref, o_ref, kmean_ref):
    blk = _MBLK
    nh = _MW // _MD
    hs = range(nh)
    scale = _MD ** -0.5 * _LOG2E
    neg = -1e30
    npast = ii * blk

    def hsl(h):
        return slice(h * _MD, (h + 1) * _MD)

    def dot(a, b, dims):
        return lax.dot_general(a, b, dims, preferred_element_type=F32)

    q = [q_ref[:, hsl(h)] for h in hs]
    qs = [(q[h] * scale).astype(BF16) for h in hs]
    own = slice(npast, npast + blk)
    kpos = _iota((blk, blk), 0)
    qpos = _iota((blk, blk), 1)
    s_own = [jnp.where(kpos <= qpos, dot(k_ref[own, hsl(h)].astype(BF16), qs[h], _NT), neg) for h in hs]
    m = [jnp.max(s_own[h], axis=0, keepdims=True) for h in hs]

    if ii > 0:
        npad = kmean_ref.shape[0]
        bid = _iota((npad, blk), 0)
        gate = [jnp.where(bid < ii, _mm(kmean_ref[:, hsl(h)], q[h], _NT, na=2, nb=2), -jnp.inf)
                for h in hs]
        s_past = [dot(k_ref[0:npast, hsl(h)].astype(BF16), qs[h], _NT) for h in hs]
        sel, bmax = [], []
        for h in hs:
            sel_h, bmax_h = [], []
            for j in range(ii):
                gj = gate[h][j:j + 1, :]
                ahead = (gate[h] > gj) | ((gate[h] == gj) & (bid < j))
                rank = jnp.sum(jnp.where(ahead, 1.0, 0.0), axis=0, keepdims=True)
                sel_h.append(rank < float(_MTOPK))
                bmax_h.append(jnp.max(s_past[h][j * blk:(j + 1) * blk], axis=0, keepdims=True))
            sel.append(sel_h)
            bmax.append(bmax_h)
        for h in hs:
            for j in range(ii):
                m[h] = jnp.maximum(m[h], jnp.where(sel[h][j], bmax[h][j], neg))
        p_past = []
        for h in hs:
            rows = [jnp.exp2(s_past[h][j * blk:(j + 1) * blk] + jnp.where(sel[h][j], -m[h], neg)) for j in range(ii)]
            p_past.append(rows[0] if ii == 1 else jnp.concatenate(rows, axis=0))

    p_own = [jnp.exp2(s_own[h] - m[h]) for h in hs]
    l = [jnp.sum(p_own[h], axis=0, keepdims=True) for h in hs]
    acc = [dot(v_ref[own, hsl(h)].astype(BF16), p_own[h].astype(BF16), _TN) for h in hs]
    if ii > 0:
        l = [l[h] + jnp.sum(p_past[h], axis=0, keepdims=True) for h in hs]
        acc = [acc[h] + dot(v_ref[0:npast, hsl(h)].astype(BF16), p_past[h].astype(BF16), _TN) for h in hs]

    o = jnp.concatenate([(acc[h] / l[h]).T for h in hs], axis=1)
    o_ref[...] = (o * _silu(g_ref[...])).astype(o_ref.dtype)


def _moba_kernel(q_ref, k_ref, v_ref, g_ref, o_ref, kmean_ref):
    i = pl.program_id(1)
    blk = _MBLK
    nb = k_ref.shape[0] // blk

    @pl.when(i == 0)
    def _():
        kmean_ref[...] = jnp.zeros_like(kmean_ref)
        for j in range(nb):
            kmean_ref[j:j + 1, :] = jnp.mean(k_ref[j * blk:(j + 1) * blk, :], axis=0, keepdims=True)

    for ii in range(nb):
        pl.when(i == ii)(functools.partial(_moba_block, ii, q_ref, k_ref, v_ref, g_ref, o_ref, kmean_ref))


def _moba_mix(p3):
    b, s, _ = p3.shape
    nq = s // _MBLK
    npad = -(-nq // 16) * 16
    return pl.pallas_call(
        _moba_kernel,
        out_shape=jax.ShapeDtypeStruct((b, s, _MW), BF16),
        grid=(b, nq),
        in_specs=[
            pl.BlockSpec((None, _MBLK, _MW), lambda bi, i: (bi, i, _O_MQ // _MW)),
            pl.BlockSpec((None, s, _MW), lambda bi, i: (bi, 0, _O_MK // _MW)),
            pl.BlockSpec((None, s, _MW), lambda bi, i: (bi, 0, _O_MV // _MW)),
            pl.BlockSpec((None, _MBLK, _MW), lambda bi, i: (bi, i, _O_MG // _MW)),
        ],
        out_specs=pl.BlockSpec((None, _MBLK, _MW), lambda bi, i: (bi, i, 0)),
        scratch_shapes=[pltpu.VMEM((npad, _MW), F32)],
        compiler_params=pltpu.CompilerParams(
            dimension_semantics=("parallel", "arbitrary"),
            vmem_limit_bytes=_VMEM_LIMIT),
        name="moba_mix",
    )(p3, p3, p3, p3)


def _gla_kernel(q_ref, k_ref, v_ref, g_ref, low_ref, aup_ref, ab_ref, nw_ref, incl_ref, tri_ref,
                o_ref, st_ref):
    c = _C
    nh = _GKW // _GDK
    bs = range(q_ref.shape[0])

    @pl.when(pl.program_id(1) == 0)
    def _():
        st_ref[...] = jnp.zeros_like(st_ref)

    def lane_masks(width, head_w):
        lane_head = _idiv(_iota((1, width), 1), head_w)
        return [jnp.where(lane_head == h, 1.0, 0.0).astype(BF16) for h in range(nh)]

    def dot(a, b, dims=_NN):
        return lax.dot_general(a, b, dims, preferred_element_type=F32)

    hm_k, hm_v = lane_masks(_GKW, _GDK), lane_masks(_GW, _GDV)
    incl, tri = incl_ref[...], tri_ref[...]
    z = [_mm(low_ref[b], aup_ref[...], na=2, nb=2) + ab_ref[...] for b in bs]
    la = [-_softplus(-z[b]) * (_LOG2E / _GTEMP) for b in bs]
    bcum = [_mm(tri, la[b], nb=_P_SUM) for b in bs]
    b_last = [bcum[b][c - 1:c, :] for b in bs]
    k = [k_ref[b] for b in bs]
    qe = [(q_ref[b] * (_GDK ** -0.5) * jnp.exp2(bcum[b])).astype(BF16) for b in bs]
    ke_s = [_stack_lanes((k[b] * jnp.exp2(-bcum[b])).astype(BF16), hm_k) for b in bs]
    kd_s = [_stack_lanes((k[b] * jnp.exp2(b_last[b] - bcum[b])).astype(BF16), hm_k) for b in bs]
    v_s = [_stack_lanes(v_ref[b].astype(BF16), hm_v) for b in bs]

    att = [(dot(qe[b], ke_s[b], _NT) * incl).astype(BF16) for b in bs]
    st = [st_ref[b] for b in bs]
    o = [dot(att[b], v_s[b]) + dot(qe[b], st[b].astype(BF16), _NT) for b in bs]
    for b in bs:
        st_ref[b] = st[b] * jnp.exp2(b_last[b]) + dot(v_s[b], kd_s[b], _TN)

    for b in bs:
        outs = []
        for h in range(nh):
            oh = o[b][:, h * _GDV:(h + 1) * _GDV]
            ms = jnp.mean(oh * oh, axis=-1, keepdims=True)
            outs.append(oh * lax.rsqrt(ms + _EPS) * nw_ref[...])
        o_ref[b] = (jnp.concatenate(outs, axis=1) * _silu(g_ref[b])).astype(o_ref.dtype)


def _gla_mix(p3, a_up_p, a_b, norm_w, bb=8):
    b, s, _ = p3.shape
    nc = s // _C
    bb = min(bb, b)
    assert b % bb == 0
    n = _GKW // _GDK * _C
    ti = lax.broadcasted_iota(jnp.int32, (_C, n), 0)
    si = lax.broadcasted_iota(jnp.int32, (_C, n), 1) % _C
    incl = (si <= ti).astype(F32)
    tri = (lax.broadcasted_iota(jnp.int32, (_C, _C), 0) >= lax.broadcasted_iota(jnp.int32, (_C, _C), 1)).astype(BF16)

    def full(shape):
        return pl.BlockSpec(shape, lambda bi, c: (0,) * len(shape))

    return pl.pallas_call(
        _gla_kernel,
        out_shape=jax.ShapeDtypeStruct((b, s, _GW), BF16),
        grid=(b // bb, nc),
        in_specs=[
            pl.BlockSpec((bb, _C, _GKW), lambda bi, c: (bi, c, _O_GQ // _GKW)),
            pl.BlockSpec((bb, _C, _GKW), lambda bi, c: (bi, c, _O_GK // _GKW)),
            pl.BlockSpec((bb, _C, _GW), lambda bi, c: (bi, c, _O_GV // _GW)),
            pl.BlockSpec((bb, _C, _GW), lambda bi, c: (bi, c, _O_GG // _GW)),
            pl.BlockSpec((bb, _C, 128), lambda bi, c: (bi, c, _O_GL // 128)),
            full((128, _GKW)), full((1, _GKW)), full((1, _GDV)), full((_C, n)), full((_C, _C)),
        ],
        out_specs=pl.BlockSpec((bb, _C, _GW), lambda bi, c: (bi, c, 0)),
        scratch_shapes=[pltpu.VMEM((bb, _GW, _GKW), F32)],
        compiler_params=pltpu.CompilerParams(
            dimension_semantics=("parallel", "arbitrary"),
            vmem_limit_bytes=_VMEM_LIMIT),
        name="gla_mix",
    )(p3, p3, p3, p3, p3, a_up_p, a_b, norm_w, incl, tri)


def _outproj_kernel(x_ref, ya_ref, yb_ref, yc_ref, w_ref, fw_ref, o_ref, *, final_norm):
    mix = jnp.concatenate([ya_ref[...], yb_ref[...], yc_ref[...]], axis=1)
    acc = x_ref[...] + jnp.dot(mix, w_ref[...], preferred_element_type=F32)
    if final_norm:
        ms = jnp.mean(acc * acc, axis=-1, keepdims=True)
        acc = acc * lax.rsqrt(ms + _EPS) * fw_ref[...]
    o_ref[...] = acc


def _out_proj(x2, ya, yb, yc, w_bf16, fw, final_norm, tm=512):
    t, d = x2.shape
    return pl.pallas_call(
        functools.partial(_outproj_kernel, final_norm=final_norm),
        out_shape=jax.ShapeDtypeStruct((t, d), F32),
        grid=(t // tm,),
        in_specs=[
            pl.BlockSpec((tm, d), lambda i: (i, 0)),
            pl.BlockSpec((tm, _RW), lambda i: (i, 0)),
            pl.BlockSpec((tm, _MW), lambda i: (i, 0)),
            pl.BlockSpec((tm, _GW), lambda i: (i, 0)),
            pl.BlockSpec((d, d), lambda i: (0, 0)),
            pl.BlockSpec((1, d), lambda i: (0, 0)),
        ],
        out_specs=pl.BlockSpec((tm, d), lambda i: (i, 0)),
        compiler_params=pltpu.CompilerParams(
            dimension_semantics=("parallel",),
            vmem_limit_bytes=_VMEM_LIMIT),
        name="out_proj",
    )(x2, ya, yb, yc, w_bf16, fw)


_WTILE = 128
_WSUB = 4


def _wprep_src_tile(j):
    src = jnp.int32(_D_IN // _WTILE)
    for a, b, o in _RUNS:
        lo = o // _WTILE
        hi = lo + -(-(b - a) // _WTILE)
        src = jnp.where((j >= lo) & (j < hi), j + (a - o) // _WTILE, src)
    return src


def _wprep_kernel(*refs):
    w_refs, o_ref = refs[:-1], refs[-1]
    last_dst = _RUNS[-1][2] + _RUNS[-1][1] - _RUNS[-1][0]
    for i, w_ref in enumerate(w_refs):
        j = pl.program_id(1) * _WSUB + i
        valid = jnp.clip(last_dst - j * _WTILE, 0, _WTILE)
        x = w_ref[...]
        x = jnp.where(_iota(x.shape, 0) < valid, x, 0.0)
        o_ref[:, i * _WTILE:(i + 1) * _WTILE] = x.T.astype(BF16)


def _prep_w_in(w_t):
    depth, n, d = w_t.shape
    assert n == _D_IN and all(a % _WTILE == 0 and o % _WTILE == 0 for a, _, o in _RUNS)
    step = _WSUB * _WTILE
    return pl.pallas_call(
        _wprep_kernel,
        out_shape=jax.ShapeDtypeStruct((depth, d, _NP), BF16),
        grid=(depth, -(-_NP // step)),
        in_specs=[pl.BlockSpec((None, _WTILE, d), lambda l, j, i=i: (l, _wprep_src_tile(j * _WSUB + i), 0))
                  for i in range(_WSUB)],
        out_specs=pl.BlockSpec((None, d, step), lambda l, j: (l, 0, j)),
        compiler_params=pltpu.CompilerParams(
            dimension_semantics=("parallel", "parallel"),
            vmem_limit_bytes=_VMEM_LIMIT),
        name="w_prep",
    )(*([w_t] * _WSUB))


def kernel(x, norm_w, w_in, w_out, rwkv_mu, rwkv_w0, rwkv_w2, rwkv_a0, rwkv_a2, rwkv_k_k, rwkv_k_a,
           rwkv_r_k, rwkv_ln_w, rwkv_ln_b, gla_a_up, gla_a_b, gla_norm_w, final_norm_w):
    b, s, d = x.shape
    depth = norm_w.shape[0]
    x2 = x.reshape(b * s, d)
    zero_lora = jnp.zeros((_RLORA, _RW), F32)
    fw = final_norm_w.reshape(1, d)
    w_in_p = _prep_w_in(jnp.swapaxes(w_in, 1, 2))
    for l in range(depth):
        w_out_l = w_out[l].astype(BF16)
        mu = rwkv_mu[l]
        mu_r, mu_k, mu_v = (mu[i * _RW:(i + 1) * _RW].reshape(1, _RW) for i in range(3))
        mu_wa = mu[3 * _RW:].reshape(1, 2 * _RLORA)
        w2p = jnp.concatenate([rwkv_w2[l], zero_lora], axis=0)
        a2p = jnp.concatenate([zero_lora, rwkv_a2[l]], axis=0)
        a_up_p = jnp.concatenate([gla_a_up[l], jnp.zeros((128 - _GRANK, _GKW), F32)], axis=0)

        p = _in_proj(x2, norm_w[l].reshape(1, d), w_in_p, l)
        p3 = p.reshape(b, s, _NP)
        row = lambda a: a.reshape(1, _RW)
        y_a = _rwkv_mix(p3, mu_r, mu_k, mu_v, mu_wa, row(rwkv_w0[l]), w2p, row(rwkv_a0[l]), a2p,
                        row(rwkv_k_k[l]), row(rwkv_k_a[l]), row(rwkv_r_k[l]),
                        row(rwkv_ln_w[l]), row(rwkv_ln_b[l]))
        y_b = _moba_mix(p3)
        y_c = _gla_mix(p3, a_up_p, gla_a_b[l].reshape(1, _GKW), gla_norm_w[l].reshape(1, _GDV))
        x2 = _out_proj(x2, y_a.reshape(b * s, _RW), y_b.reshape(b * s, _MW), y_c.reshape(b * s, _GW),
                       w_out_l, fw, final_norm=(l == depth - 1))
    return x2.reshape(b, s, d)
```

```python
import functools

import jax
import jax.numpy as jnp
from jax import lax
from jax.experimental import pallas as pl
from jax.experimental.pallas import tpu as pltpu

F32 = jnp.float32
BF16 = jnp.bfloat16

_D = 2048
_RW = 1024
_RN = 64
_RLORA = 64
_GN_EPS = 64e-5
_MW = 512
_MD = 128
_MBLK = 256
_MTOPK = 3
_GW = 512
_GKW = 256
_GDK = 64
_GDV = 128
_GRANK = 16
_GTEMP = 16.0
_EPS = 1e-6
_C = 64
_GRP = 256

_D_IN = 7824
_RUNS = ((0, 3072, 0),
         (3200, 4224, 3072),
         (4224, 6272, 4096),
         (6784, 7808, 6144),
         (6272, 6784, 7168),
         (3072, 3200, 7680),
         (7808, 7824, 7808))
_O_R, _O_K, _O_V, _O_GA = 0, 1024, 2048, 3072
_O_MQ, _O_MK, _O_MV, _O_MG = 4096, 4608, 5120, 5632
_O_GV, _O_GG = 6144, 6656
_O_GQ, _O_GK = 7168, 7424
_O_WA, _O_GL = 7680, 7808
_NP = 7936
assert all(o % _GW == 0 for o in (_O_GV, _O_GG))
assert all(o % _RW == 0 for o in (_O_R, _O_K, _O_V, _O_GA))
assert all(o % _GKW == 0 for o in (_O_GQ, _O_GK))
assert all(o % _MW == 0 for o in (_O_MQ, _O_MK, _O_MV, _O_MG))
assert all(o % 128 == 0 for o in (_O_WA, _O_GL))

_NN = (((1,), (0,)), ((), ()))
_NT = (((1,), (1,)), ((), ()))
_TN = (((0,), (0,)), ((), ()))

_VMEM_LIMIT = 56 * 1024 * 1024
_EXP_M05 = 0.6065306597126334
_LOG2E = 1.4426950408889634

_P_INV = 1
_P_MAIN = 1
_P_SUM = 2
_P_HEADSUM = 1


def _parts(x, n):
    if x.dtype == BF16:
        return [x]
    out, rem = [], x
    for i in range(n):
        p = rem.astype(BF16)
        out.append(p)
        if i + 1 < n:
            rem = rem - p.astype(F32)
    return out


def _mm(a, b, dims=_NN, na=1, nb=1):
    pa = a if isinstance(a, list) else _parts(a, na)
    pb = b if isinstance(b, list) else _parts(b, nb)
    order = max(len(pa), len(pb))
    free_axis = 1 if dims[0][0] == (0,) else 0
    m = pa[0].shape[free_axis]
    acc = None
    for j, y in enumerate(pb):
        xs = pa[:order - j]
        if not xs:
            continue
        x = xs[0] if len(xs) == 1 else jnp.concatenate(xs, axis=free_axis)
        t = lax.dot_general(x, y, dims, preferred_element_type=F32)
        for i in range(len(xs)):
            blk = t[i * m:(i + 1) * m]
            acc = blk if acc is None else acc + blk
    return acc


def _softplus(x):
    return jnp.maximum(x, 0.0) + jnp.log(1.0 + jnp.exp(-jnp.abs(x)))


def _sigmoid(x):
    return 1.0 / (1.0 + jnp.exp2(x * -_LOG2E))


def _silu(x):
    return x * _sigmoid(x)


def _iota(shape, dim):
    return lax.broadcasted_iota(jnp.int32, shape, dim)


def _idiv(x, n):
    assert n & (n - 1) == 0
    return jnp.right_shift(x, n.bit_length() - 1)


def _inproj_kernel(x_ref, nw_ref, w_ref, o_ref):
    x = x_ref[...]
    ms = jnp.mean(x * x, axis=-1, keepdims=True)
    h = (x * lax.rsqrt(ms + _EPS) * nw_ref[...]).astype(BF16)
    o_ref[...] = jnp.dot(h, w_ref[...], preferred_element_type=F32)


def _in_proj(x2, nw, w_bf16, layer, tm=256):
    t, d = x2.shape
    n = w_bf16.shape[2]
    return pl.pallas_call(
        _inproj_kernel,
        out_shape=jax.ShapeDtypeStruct((t, n), F32),
        grid=(t // tm,),
        in_specs=[
            pl.BlockSpec((tm, d), lambda i: (i, 0)),
            pl.BlockSpec((1, d), lambda i: (0, 0)),
            pl.BlockSpec((None, d, n), lambda i: (layer, 0, 0), pipeline_mode=pl.Buffered(1)),
        ],
        out_specs=pl.BlockSpec((tm, n), lambda i: (i, 0)),
        compiler_params=pltpu.CompilerParams(
            dimension_semantics=("parallel",),
            vmem_limit_bytes=_VMEM_LIMIT),
        name="in_proj",
    )(x2, nw, w_bf16)


def _token_shift(x, prev_ref, mu):
    c = x.shape[0]
    rows = _iota(x.shape, 0)
    xp = jnp.where(rows == 0, prev_ref[7:8, :], pltpu.roll(x, 1, 0))
    prev_ref[...] = x[c - 8:c, :]
    return x + mu * (xp - x)


def _head_sums(x, head_ones):
    c = x.shape[0]
    ng = x.shape[1] // _GRP
    rows = [p[:, g * _GRP:(g + 1) * _GRP] for p in _parts(x, _P_HEADSUM) for g in range(ng)]
    t = lax.dot_general(jnp.concatenate(rows, axis=0), head_ones, _NN, preferred_element_type=F32)
    cols = []
    for g in range(ng):
        acc = t[g * c:(g + 1) * c]
        for i in range(1, _P_HEADSUM):
            acc = acc + t[(i * ng + g) * c:(i * ng + g + 1) * c]
        cols.append(acc)
    return jnp.concatenate(cols, axis=1)


def _stack_lanes(x, lane_masks):
    return jnp.concatenate([x * m for m in lane_masks], axis=0)


def _rwkv_groups(at, rt, bt, kt, bh, kh, v, w_end, st_ref, strict, incl, eye_c, eye, same):
    c = _C
    nh = _GRP // _RN
    gs = range(at.shape[1] // _GRP)
    lane_head = _idiv(_iota((1, _GRP), 1), _RN)
    hm = [jnp.where(lane_head == h, 1.0, 0.0).astype(BF16) for h in range(nh)]

    def grp(x, g):
        return x[:, g * _GRP:(g + 1) * _GRP]

    def bf(x):
        return x.astype(BF16)

    def dot(a, b, dims=_NN):
        return lax.dot_general(a, b, dims, preferred_element_type=F32)

    def stack(x):
        return _stack_lanes(bf(x), hm)

    at_b = [bf(grp(at, g)) for g in gs]
    v_b = [bf(grp(v, g)) for g in gs]
    bh_b = [bf(grp(bh, g)) for g in gs]
    lhs = [jnp.concatenate([at_b[g], bf(grp(rt, g))], axis=0) for g in gs]
    rhs = [jnp.concatenate([stack(grp(bt, g)), stack(grp(kt, g))], axis=0) for g in gs]
    a_all = [dot(lhs[g], rhs[g], _NT) for g in gs]
    a_ab = [a_all[g][:c, :nh * c] * strict for g in gs]
    a_ak = [bf(a_all[g][:c, nh * c:] * strict) for g in gs]
    a_rb = [bf(a_all[g][c:, :nh * c] * incl) for g in gs]
    a_rk = [bf(a_all[g][c:, nh * c:] * incl) for g in gs]

    x = [eye_c + a_ab[g] for g in gs]
    p_b = [bf(a_ab[g]) for g in gs]
    p = [dot(p_b[g], _stack_lanes(p_b[g], hm)) for g in gs]
    n_sq = c.bit_length() - 2
    for it in range(n_sq):
        p_bd = [stack(p[g]) for g in gs]
        if it + 1 < n_sq:
            xp = [dot(jnp.concatenate([bf(x[g]), bf(p[g])], axis=0), p_bd[g]) for g in gs]
            x = [x[g] + xp[g][:c] for g in gs]
            p = [xp[g][c:] for g in gs]
        else:
            x = [x[g] + dot(bf(x[g]), p_bd[g]) for g in gs]
    t = [bf(x[g]) for g in gs]

    v_s = [_stack_lanes(v_b[g], hm) for g in gs]
    abar = [dot(t[g], _stack_lanes(at_b[g], hm)) for g in gs]
    akv = [dot(jnp.concatenate([a_ak[g], a_rk[g]], axis=0), v_s[g]) for g in gs]
    ubar = [dot(t[g], stack(akv[g][:c])) for g in gs]
    abar_b = [bf(abar[g]) for g in gs]
    ubar_b = [bf(ubar[g]) for g in gs]
    rbar = [grp(rt, g) + dot(a_rb[g], _stack_lanes(abar_b[g], hm)) for g in gs]
    ybar = [dot(a_rb[g], _stack_lanes(ubar_b[g], hm)) + akv[g][c:] for g in gs]
    mt = [eye * grp(w_end, g) + dot(bh_b[g], abar_b[g], _TN) for g in gs]
    dt = [dot(jnp.concatenate([bh_b[g], bf(grp(kh, g))], axis=0),
              jnp.concatenate([ubar_b[g], v_b[g]], axis=0), _TN) for g in gs]

    ys = [dot(jnp.concatenate([bf(rbar[g]), bf(mt[g])], axis=0), bf(st_ref[g])) for g in gs]
    for g in gs:
        st_ref[g] = same * (ys[g][c:] + dt[g])
    return jnp.concatenate([ys[g][:c] + ybar[g] for g in gs], axis=1)


def _rwkv_kernel(pr_ref, pk_ref, pv_ref, pg_ref, pwa_ref,
                 mur_ref, muk_ref, muv_ref, muwa_ref,
                 w0_ref, w2_ref, a0_ref, a2_ref, kk_ref, ka_ref, rk_ref, lnw_ref, lnb_ref,
                 strict_ref, incl_ref, eyec_ref, eye_ref, same_ref, hones_ref, tri_ref,
                 o_ref,
                 s_ref, prev_r, prev_k, prev_v, prev_wa):
    c = _C
    nb = pr_ref.shape[0]

    @pl.when(pl.program_id(1) == 0)
    def _():
        s_ref[...] = jnp.zeros_like(s_ref)
        prev_r[...] = jnp.zeros_like(prev_r)
        prev_k[...] = jnp.zeros_like(prev_k)
        prev_v[...] = jnp.zeros_like(prev_v)
        prev_wa[...] = jnp.zeros_like(prev_wa)

    head_ones = hones_ref[...]

    def lanes(ref):
        return jnp.concatenate([ref[b] for b in range(nb)], axis=1)

    def rep(ref):
        return jnp.concatenate([ref[...]] * nb, axis=1)

    def unrows(x):
        return jnp.concatenate([x[b * c:(b + 1) * c] for b in range(nb)], axis=1)

    r = _token_shift(lanes(pr_ref), prev_r, rep(mur_ref))
    k = _token_shift(lanes(pk_ref), prev_k, rep(muk_ref))
    v = _token_shift(lanes(pv_ref), prev_v, rep(muv_ref))
    wa = _token_shift(lanes(pwa_ref), prev_wa, rep(muwa_ref))
    wa = jnp.concatenate([wa[:, b * 128:(b + 1) * 128] for b in range(nb)], axis=0)

    wdec = rep(w0_ref) + unrows(_mm(jnp.tanh(wa), w2_ref[...], na=2))
    lw = -(_EXP_M05 * _LOG2E) * _sigmoid(wdec)
    a_lr = _sigmoid(rep(a0_ref) + unrows(_mm(wa, a2_ref[...], na=2)))
    kk = k * rep(kk_ref)
    kk = kk * lax.rsqrt(jnp.maximum(_head_sums(kk * kk, head_ones), 1e-24))
    k2 = k * (1.0 + (a_lr - 1.0) * rep(ka_ref))
    bb = kk * a_lr
    cl = _mm(tri_ref[...], lw, nb=_P_SUM)
    cl_end = cl[c - 1:c, :]
    e_in = jnp.exp2(cl)
    e_neg = jnp.exp2(-cl)
    w_end = jnp.exp2(cl_end)
    e_end = w_end * e_neg
    at = -kk * jnp.exp2(cl - lw)
    rt = r * e_in
    bt = bb * e_neg
    kt = k2 * e_neg
    bh = bb * e_end
    kh = k2 * e_end

    y = _rwkv_groups(at, rt, bt, kt, bh, kh, v, w_end, s_ref,
                     strict_ref[...], incl_ref[...], eyec_ref[...], eye_ref[...], same_ref[...])

    inv_n = 1.0 / _RN
    mean = _head_sums(y, head_ones) * inv_n
    yc = y - mean
    var = _head_sums(yc * yc, head_ones) * inv_n
    yn = yc * lax.rsqrt(var + _GN_EPS) * rep(lnw_ref) + rep(lnb_ref)
    bonus = _head_sums(r * k2 * rep(rk_ref), head_ones) * v
    out = yn + bonus
    for b in range(nb):
        o_ref[b] = (out[:, b * _RW:(b + 1) * _RW] * _silu(pg_ref[b])).astype(o_ref.dtype)


def _rwkv_consts():
    n = _GRP
    t = lax.broadcasted_iota(jnp.int32, (_C, n // _RN * _C), 0)
    s = lax.broadcasted_iota(jnp.int32, (_C, n // _RN * _C), 1) % _C
    strict = (s < t).astype(F32)
    incl = (s <= t).astype(F32)
    eye_c = (s == t).astype(F32)
    ri = lax.broadcasted_iota(jnp.int32, (n, n), 0)
    ci = lax.broadcasted_iota(jnp.int32, (n, n), 1)
    eye = (ri == ci).astype(F32)
    same = (ri // _RN) == (ci // _RN)
    ti = lax.broadcasted_iota(jnp.int32, (_C, _C), 0)
    tj = lax.broadcasted_iota(jnp.int32, (_C, _C), 1)
    tri = (ti >= tj).astype(BF16)
    return strict, incl, eye_c, eye, same.astype(F32), same.astype(BF16), tri


def _rwkv_mix(p3, mu_r, mu_k, mu_v, mu_wa, w0, w2p, a0, a2p, k_k, k_a, r_k, ln_w, ln_b, bb=4):
    b, s, _ = p3.shape
    nc = s // _C
    bb = min(bb, b)
    assert b % bb == 0
    ng = bb * _RW // _GRP

    def pspec(off):
        return pl.BlockSpec((bb, _C, _RW), lambda bi, c, o=off // _RW: (bi, c, o))

    def full(shape):
        return pl.BlockSpec(shape, lambda bi, c: (0,) * len(shape))

    vec = full((1, _RW))
    lora = full((2 * _RLORA, _RW))
    sq = full((_GRP, _GRP))
    cat = full((_C, _GRP // _RN * _C))
    return pl.pallas_call(
        _rwkv_kernel,
        out_shape=jax.ShapeDtypeStruct((b, s, _RW), BF16),
        grid=(b // bb, nc),
        in_specs=[
            pspec(_O_R), pspec(_O_K), pspec(_O_V), pspec(_O_GA),
            pl.BlockSpec((bb, _C, 128), lambda bi, c: (bi, c, _O_WA // 128)),
            vec, vec, vec, full((1, 2 * _RLORA)),
            vec, lora, vec, lora, vec, vec, vec, vec, vec,
            cat, cat, cat, sq, sq, sq, full((_C, _C)),
        ],
        out_specs=pl.BlockSpec((bb, _C, _RW), lambda bi, c: (bi, c, 0)),
        scratch_shapes=[
            pltpu.VMEM((ng, _GRP, _GRP), F32),
            pltpu.VMEM((8, bb * _RW), F32), pltpu.VMEM((8, bb * _RW), F32),
            pltpu.VMEM((8, bb * _RW), F32), pltpu.VMEM((8, bb * 2 * _RLORA), F32),
        ],
        compiler_params=pltpu.CompilerParams(
            dimension_semantics=("parallel", "arbitrary"),
            vmem_limit_bytes=_VMEM_LIMIT),
        name="rwkv_mix",
    )(p3, p3, p3, p3, p3, mu_r, mu_k, mu_v, mu_wa, w0, w2p, a0, a2p, k_k, k_a, r_k, ln_w, ln_b,
      *_rwkv_consts())


def _moba_block(ii, q_ref, k_ref, v_ref, g_ref, o_ref, kmean_ref):
    blk = _MBLK
    nh = _MW // _MD
    hs = range(nh)
    scale = _MD ** -0.5 * _LOG2E
    neg = -1e30
    npast = ii * blk

    def hsl(h):
        return slice(h * _MD, (h + 1) * _MD)

    def dot(a, b, dims):
        return lax.dot_general(a, b, dims, preferred_element_type=F32)

    q = [q_ref[:, hsl(h)] for h in hs]
    qs = [(q[h] * scale).astype(BF16) for h in hs]
    own = slice(npast, npast + blk)
    kpos = _iota((blk, blk), 0)
    qpos = _iota((blk, blk), 1)
    s_own = [jnp.where(kpos <= qpos, dot(k_ref[own, hsl(h)], qs[h], _NT), neg) for h in hs]
    m = [jnp.max(s_own[h], axis=0, keepdims=True) for h in hs]

    if ii > 0:
        npad = kmean_ref.shape[0]
        bid = _iota((npad, blk), 0)
        gate = [jnp.where(bid < ii, _mm(kmean_ref[:, hsl(h)], q[h], _NT, na=2, nb=2), -jnp.inf)
                for h in hs]
        s_past = [dot(k_ref[0:npast, hsl(h)], qs[h], _NT) for h in hs]
        masked = []
        for h in hs:
            rows = []
            for j in range(ii):
                gj = gate[h][j:j + 1, :]
                ahead = (gate[h] > gj) | ((gate[h] == gj) & (bid < j))
                rank = jnp.sum(jnp.where(ahead, 1.0, 0.0), axis=0, keepdims=True)
                rows.append(jnp.where(rank < float(_MTOPK), s_past[h][j * blk:(j + 1) * blk], neg))
            masked.append(rows[0] if ii == 1 else jnp.concatenate(rows, axis=0))
        m = [jnp.maximum(m[h], jnp.max(masked[h], axis=0, keepdims=True)) for h in hs]
        p_past = [jnp.exp2(masked[h] - m[h]) for h in hs]

    p_own = [jnp.exp2(s_own[h] - m[h]) for h in hs]
    l = [jnp.sum(p_own[h], axis=0, keepdims=True) for h in hs]
    acc = [dot(v_ref[own, hsl(h)], p_own[h].astype(BF16), _TN) for h in hs]
    if ii > 0:
        l = [l[h] + jnp.sum(p_past[h], axis=0, keepdims=True) for h in hs]
        acc = [acc[h] + dot(v_ref[0:npast, hsl(h)], p_past[h].astype(BF16), _TN) for h in hs]

    o = jnp.concatenate([(acc[h] / l[h]).T for h in hs], axis=1)
    o_ref[...] = (o * _silu(g_ref[...])).astype(o_ref.dtype)


def _moba_kernel(q_ref, k_ref, v_ref, g_ref, o_ref, kmean_ref, kb_ref, vb_ref):
    i = pl.program_id(1)
    blk = _MBLK
    nb = k_ref.shape[0] // blk

    @pl.when(i == 0)
    def _():
        kmean_ref[...] = jnp.zeros_like(kmean_ref)
        for j in range(nb):
            rows = slice(j * blk, (j + 1) * blk)
            kj = k_ref[rows, :]
            kmean_ref[j:j + 1, :] = jnp.mean(kj, axis=0, keepdims=True)
            kb_ref[rows, :] = kj.astype(BF16)
            vb_ref[rows, :] = v_ref[rows, :].astype(BF16)

    for ii in range(nb):
        pl.when(i == ii)(functools.partial(_moba_block, ii, q_ref, kb_ref, vb_ref, g_ref, o_ref, kmean_ref))


def _moba_mix(p3):
    b, s, _ = p3.shape
    nq = s // _MBLK
    npad = -(-nq // 16) * 16
    return pl.pallas_call(
        _moba_kernel,
        out_shape=jax.ShapeDtypeStruct((b, s, _MW), BF16),
        grid=(b, nq),
        in_specs=[
            pl.BlockSpec((None, _MBLK, _MW), lambda bi, i: (bi, i, _O_MQ // _MW)),
            pl.BlockSpec((None, s, _MW), lambda bi, i: (bi, 0, _O_MK // _MW)),
            pl.BlockSpec((None, s, _MW), lambda bi, i: (bi, 0, _O_MV // _MW)),
            pl.BlockSpec((None, _MBLK, _MW), lambda bi, i: (bi, i, _O_MG // _MW)),
        ],
        out_specs=pl.BlockSpec((None, _MBLK, _MW), lambda bi, i: (bi, i, 0)),
        scratch_shapes=[pltpu.VMEM((npad, _MW), F32), pltpu.VMEM((s, _MW), BF16), pltpu.VMEM((s, _MW), BF16)],
        compiler_params=pltpu.CompilerParams(
            dimension_semantics=("parallel", "arbitrary"),
            vmem_limit_bytes=_VMEM_LIMIT),
        name="moba_mix",
    )(p3, p3, p3, p3)


def _gla_kernel(q_ref, k_ref, v_ref, g_ref, low_ref, aup_ref, ab_ref, nw_ref, incl_ref, tri_ref,
                o_ref, st_ref):
    c = _C
    nh = _GKW // _GDK
    bs = range(q_ref.shape[0])

    @pl.when(pl.program_id(1) == 0)
    def _():
        st_ref[...] = jnp.zeros_like(st_ref)

    def lane_masks(width, head_w):
        lane_head = _idiv(_iota((1, width), 1), head_w)
        return [jnp.where(lane_head == h, 1.0, 0.0).astype(BF16) for h in range(nh)]

    def dot(a, b, dims=_NN):
        return lax.dot_general(a, b, dims, preferred_element_type=F32)

    hm_k, hm_v = lane_masks(_GKW, _GDK), lane_masks(_GW, _GDV)
    incl, tri = incl_ref[...], tri_ref[...]
    z = [_mm(low_ref[b], aup_ref[...], na=2, nb=2) + ab_ref[...] for b in bs]
    la = [-_softplus(-z[b]) * (_LOG2E / _GTEMP) for b in bs]
    bcum = [_mm(tri, la[b], nb=_P_SUM) for b in bs]
    b_last = [bcum[b][c - 1:c, :] for b in bs]
    k = [k_ref[b] for b in bs]
    qe = [(q_ref[b] * (_GDK ** -0.5) * jnp.exp2(bcum[b])).astype(BF16) for b in bs]
    ke_s = [_stack_lanes((k[b] * jnp.exp2(-bcum[b])).astype(BF16), hm_k) for b in bs]
    kd_s = [_stack_lanes((k[b] * jnp.exp2(b_last[b] - bcum[b])).astype(BF16), hm_k) for b in bs]
    v_s = [_stack_lanes(v_ref[b].astype(BF16), hm_v) for b in bs]

    att = [(dot(qe[b], ke_s[b], _NT) * incl).astype(BF16) for b in bs]
    st = [st_ref[b] for b in bs]
    o = [dot(att[b], v_s[b]) + dot(qe[b], st[b].astype(BF16), _NT) for b in bs]
    for b in bs:
        st_ref[b] = st[b] * jnp.exp2(b_last[b]) + dot(v_s[b], kd_s[b], _TN)

    for b in bs:
        outs = []
        for h in range(nh):
            oh = o[b][:, h * _GDV:(h + 1) * _GDV]
            ms = jnp.mean(oh * oh, axis=-1, keepdims=True)
            outs.append(oh * lax.rsqrt(ms + _EPS) * nw_ref[...])
        o_ref[b] = (jnp.concatenate(outs, axis=1) * _silu(g_ref[b])).astype(o_ref.dtype)


def _gla_mix(p3, a_up_p, a_b, norm_w, bb=8):
    b, s, _ = p3.shape
    nc = s // _C
    bb = min(bb, b)
    assert b % bb == 0
    n = _GKW // _GDK * _C
    ti = lax.broadcasted_iota(jnp.int32, (_C, n), 0)
    si = lax.broadcasted_iota(jnp.int32, (_C, n), 1) % _C
    incl = (si <= ti).astype(F32)
    tri = (lax.broadcasted_iota(jnp.int32, (_C, _C), 0) >= lax.broadcasted_iota(jnp.int32, (_C, _C), 1)).astype(BF16)

    def full(shape):
        return pl.BlockSpec(shape, lambda bi, c: (0,) * len(shape))

    return pl.pallas_call(
        _gla_kernel,
        out_shape=jax.ShapeDtypeStruct((b, s, _GW), BF16),
        grid=(b // bb, nc),
        in_specs=[
            pl.BlockSpec((bb, _C, _GKW), lambda bi, c: (bi, c, _O_GQ // _GKW)),
            pl.BlockSpec((bb, _C, _GKW), lambda bi, c: (bi, c, _O_GK // _GKW)),
            pl.BlockSpec((bb, _C, _GW), lambda bi, c: (bi, c, _O_GV // _GW)),
            pl.BlockSpec((bb, _C, _GW), lambda bi, c: (bi, c, _O_GG // _GW)),
            pl.BlockSpec((bb, _C, 128), lambda bi, c: (bi, c, _O_GL // 128)),
            full((128, _GKW)), full((1, _GKW)), full((1, _GDV)), full((_C, n)), full((_C, _C)),
        ],
        out_specs=pl.BlockSpec((bb, _C, _GW), lambda bi, c: (bi, c, 0)),
        scratch_shapes=[pltpu.VMEM((bb, _GW, _GKW), F32)],
        compiler_params=pltpu.CompilerParams(
            dimension_semantics=("parallel", "arbitrary"),
            vmem_limit_bytes=_VMEM_LIMIT),
        name="gla_mix",
    )(p3, p3, p3, p3, p3, a_up_p, a_b, norm_w, incl, tri)


def _outproj_kernel(x_ref, ya_ref, yb_ref, yc_ref, w_ref, fw_ref, o_ref, *, final_norm):
    mix = jnp.concatenate([ya_ref[...], yb_ref[...], yc_ref[...]], axis=1)
    acc = x_ref[...] + jnp.dot(mix, w_ref[...], preferred_element_type=F32)
    if final_norm:
        ms = jnp.mean(acc * acc, axis=-1, keepdims=True)
        acc = acc * lax.rsqrt(ms + _EPS) * fw_ref[...]
    o_ref[...] = acc


def _out_proj(x2, ya, yb, yc, w_bf16, fw, final_norm, tm=512):
    t, d = x2.shape
    return pl.pallas_call(
        functools.partial(_outproj_kernel, final_norm=final_norm),
        out_shape=jax.ShapeDtypeStruct((t, d), F32),
        grid=(t // tm,),
        in_specs=[
            pl.BlockSpec((tm, d), lambda i: (i, 0)),
            pl.BlockSpec((tm, _RW), lambda i: (i, 0)),
            pl.BlockSpec((tm, _MW), lambda i: (i, 0)),
            pl.BlockSpec((tm, _GW), lambda i: (i, 0)),
            pl.BlockSpec((d, d), lambda i: (0, 0)),
            pl.BlockSpec((1, d), lambda i: (0, 0)),
        ],
        out_specs=pl.BlockSpec((tm, d), lambda i: (i, 0)),
        compiler_params=pltpu.CompilerParams(
            dimension_semantics=("parallel",),
            vmem_limit_bytes=_VMEM_LIMIT),
        name="out_proj",
    )(x2, ya, yb, yc, w_bf16, fw)


_WTILE = 128
_WSUB = 4


def _wprep_src_tile(j):
    src = jnp.int32(_D_IN // _WTILE)
    for a, b, o in _RUNS:
        lo = o // _WTILE
        hi = lo + -(-(b - a) // _WTILE)
        src = jnp.where((j >= lo) & (j < hi), j + (a - o) // _WTILE, src)
    return src


def _wprep_kernel(*refs):
    w_refs, o_ref = refs[:-1], refs[-1]
    last_dst = _RUNS[-1][2] + _RUNS[-1][1] - _RUNS[-1][0]
    for i, w_ref in enumerate(w_refs):
        j = pl.program_id(1) * _WSUB + i
        valid = jnp.clip(last_dst - j * _WTILE, 0, _WTILE)
        x = w_ref[...]
        x = jnp.where(_iota(x.shape, 0) < valid, x, 0.0)
        o_ref[:, i * _WTILE:(i + 1) * _WTILE] = x.T.astype(BF16)


def _prep_w_in(w_t):
    depth, n, d = w_t.shape
    assert n == _D_IN and all(a % _WTILE == 0 and o % _WTILE == 0 for a, _, o in _RUNS)
    step = _WSUB * _WTILE
    return pl.pallas_call(
        _wprep_kernel,
        out_shape=jax.ShapeDtypeStruct((depth, d, _NP), BF16),
        grid=(depth, -(-_NP // step)),
        in_specs=[pl.BlockSpec((None, _WTILE, d), lambda l, j, i=i: (l, _wprep_src_tile(j * _WSUB + i), 0))
                  for i in range(_WSUB)],
        out_specs=pl.BlockSpec((None, d, step), lambda l, j: (l, 0, j)),
        compiler_params=pltpu.CompilerParams(
            dimension_semantics=("parallel", "parallel"),
            vmem_limit_bytes=_VMEM_LIMIT),
        name="w_prep",
    )(*([w_t] * _WSUB))


def kernel(x, norm_w, w_in, w_out, rwkv_mu, rwkv_w0, rwkv_w2, rwkv_a0, rwkv_a2, rwkv_k_k, rwkv_k_a,
           rwkv_r_k, rwkv_ln_w, rwkv_ln_b, gla_a_up, gla_a_b, gla_norm_w, final_norm_w):
    b, s, d = x.shape
    depth = norm_w.shape[0]
    x2 = x.reshape(b * s, d)
    zero_lora = jnp.zeros((_RLORA, _RW), F32)
    fw = final_norm_w.reshape(1, d)
    w_in_p = _prep_w_in(jnp.swapaxes(w_in, 1, 2))
    for l in range(depth):
        w_out_l = w_out[l].astype(BF16)
        mu = rwkv_mu[l]
        mu_r, mu_k, mu_v = (mu[i * _RW:(i + 1) * _RW].reshape(1, _RW) for i in range(3))
        mu_wa = mu[3 * _RW:].reshape(1, 2 * _RLORA)
        w2p = jnp.concatenate([rwkv_w2[l], zero_lora], axis=0)
        a2p = jnp.concatenate([zero_lora, rwkv_a2[l]], axis=0)
        a_up_p = jnp.concatenate([gla_a_up[l], jnp.zeros((128 - _GRANK, _GKW), F32)], axis=0)

        p = _in_proj(x2, norm_w[l].reshape(1, d), w_in_p, l)
        p3 = p.reshape(b, s, _NP)
        row = lambda a: a.reshape(1, _RW)
        y_a = _rwkv_mix(p3, mu_r, mu_k, mu_v, mu_wa, row(rwkv_w0[l]), w2p, row(rwkv_a0[l]), a2p,
                        row(rwkv_k_k[l]), row(rwkv_k_a[l]), row(rwkv_r_k[l]),
                        row(rwkv_ln_w[l]), row(rwkv_ln_b[l]))
        y_b = _moba_mix(p3)
        y_c = _gla_mix(p3, a_up_p, gla_a_b[l].reshape(1, _GKW), gla_norm_w[l].reshape(1, _GDV))
        x2 = _out_proj(x2, y_a.reshape(b * s, _RW), y_b.reshape(b * s, _MW), y_c.reshape(b * s, _GW),
                       w_out_l, fw, final_norm=(l == depth - 1))
    return x2.reshape(b, s, d)
```
